```python
import jax, jax.numpy as jnp
from jax import lax
import numpy as np

D_MODEL = 2048
BATCH = 4
SEQ = 8192
DEPTH = 4

N_META = 16
CHUNK = 64
META_PAD = CHUNK - N_META
RET_HEADS = 8
RET_HD = 128
RET_W = RET_HEADS * RET_HD
ATT_HEADS = 8
ATT_KV_HEADS = 2
ATT_HD = 128
ATT_Q_W = ATT_HEADS * ATT_HD
ATT_KV_W = ATT_KV_HEADS * ATT_HD
WINDOW = 128
ATT_BLOCK = 128
EVEN_SPLITS = (RET_W, 2 * RET_W, 3 * RET_W, 4 * RET_W, 4 * RET_W + ATT_Q_W, 4 * RET_W + ATT_Q_W + ATT_KV_W)
EVEN_IN = 4 * RET_W + ATT_Q_W + 2 * ATT_KV_W
EVEN_MIX = RET_W + ATT_Q_W
GDN_QK_HEADS = 16
GDN_V_HEADS = 32
GDN_HD = 128
GDN_K_W = GDN_QK_HEADS * GDN_HD
GDN_V_W = GDN_V_HEADS * GDN_HD
GDN_CONV_CH = 2 * GDN_K_W + GDN_V_W
CONV_WIDTH = 5
ODD_IN = GDN_CONV_CH + GDN_V_W + 4 * GDN_V_HEADS
N_EXPERTS = 64
TOP_K = 8
N_GROUPS = 8
TOPK_GROUPS = 4
D_EXPERT = 384
D_SHARED = 384
ROUTE_SCALE = 2.5
MOE_BLOCK = 256
DN_ALPHA = (2 * DEPTH) ** 0.25
DN_BETA = (8 * DEPTH) ** -0.25
LN_EPS = 1e-5
NORM_EPS = 1e-6

kernel_name = 'hybrid_retention_swa_gdeltanet_moe_encoder'


def _layer_norm(x, g, b):
    xf = x.astype(jnp.float32)
    mu = xf.mean(-1, keepdims=True)
    var = jnp.mean(jnp.square(xf - mu), -1, keepdims=True)
    y = (xf - mu) * lax.rsqrt(var + LN_EPS) * g.astype(jnp.float32) + b.astype(jnp.float32)
    return y.astype(x.dtype)


def _head_norm(t):
    mu = t.mean(-1, keepdims=True)
    var = jnp.mean(jnp.square(t - mu), -1, keepdims=True)
    return (t - mu) * lax.rsqrt(var + NORM_EPS)


def _l2norm(t):
    return t * lax.rsqrt(jnp.sum(t * t, -1, keepdims=True) + NORM_EPS)


def _left_pad(t, n):
    return jnp.pad(t, [(0, 0), (n, 0)] + [(0, 0)] * (t.ndim - 2))


def _retention_dir(q, k, v, include_diag):
    B, Lp, H, dk = q.shape
    dv = v.shape[-1]
    n = Lp // CHUNK
    lg = jnp.log(1.0 - 2.0 ** (-5.0 - jnp.arange(H, dtype=jnp.float32)))
    pos = jnp.arange(CHUNK, dtype=jnp.float32)
    rel = pos[:, None] - pos[None, :]
    keep = (rel >= 0) if include_diag else (rel > 0)
    dmat = jnp.where(keep, jnp.exp(lg[:, None, None] * jnp.maximum(rel, 0.0)), 0.0)
    q_in = jnp.exp(lg[None, :] * (pos[:, None] + 1.0))
    k_out = jnp.exp(lg[None, :] * (CHUNK - 1.0 - pos[:, None]))
    g_chunk = jnp.exp(lg * CHUNK)

    def to_chunks(t):
        return jnp.moveaxis(t.reshape(B, n, CHUNK, H, t.shape[-1]), 1, 0)

    def step(state, inp):
        qc, kc, vc = inp
        intra = jnp.einsum('bihd,bjhd->bhij', qc, kc) * dmat
        out = (jnp.einsum('bhij,bjhe->bihe', intra, vc)
               + jnp.einsum('bihd,bhde->bihe', qc * q_in[..., None], state))
        state = state * g_chunk[:, None, None] + jnp.einsum('bjhd,bjhe->bhde', kc * k_out[..., None], vc)
        return state, out

    state0 = jnp.zeros((B, H, dk, dv), jnp.float32)
    _, out = lax.scan(step, state0, (to_chunks(q), to_chunks(k), to_chunks(v)))
    return jnp.moveaxis(out, 0, 1).reshape(B, Lp, H, dv)


def retention_bidir(q, k, v):
    qp, kp, vp = _left_pad(q, META_PAD), _left_pad(k, META_PAD), _left_pad(v, META_PAD)
    fwd = _retention_dir(qp, kp, vp, True)
    bwd = jnp.flip(_retention_dir(jnp.flip(qp, 1), jnp.flip(kp, 1), jnp.flip(vp, 1), False), 1)
    return (fwd + bwd)[:, META_PAD:]


def window_attention(q, k, v, sink):
    B, L, _, d = q.shape
    S = L - N_META
    nb = S // ATT_BLOCK
    G = ATT_HEADS // ATT_KV_HEADS
    slopes = (2.0 ** (-8.0 * (jnp.arange(ATT_HEADS, dtype=jnp.float32) + 1.0) / ATT_HEADS)).reshape(ATT_KV_HEADS, G)
    sink = sink.astype(jnp.float32).reshape(ATT_KV_HEADS, G)
    q = q.reshape(B, L, ATT_KV_HEADS, G, d) * (d ** -0.5)
    qm, qr = q[:, :N_META], q[:, N_META:]
    km, kr = k[:, :N_META], k[:, N_META:]
    vm, vr = v[:, :N_META], v[:, N_META:]

    qb = qr.reshape(B, nb, ATT_BLOCK, ATT_KV_HEADS, G, d)

    def neighbours(t):
        tp = jnp.pad(t, ((0, 0), (ATT_BLOCK, ATT_BLOCK), (0, 0), (0, 0))).reshape(B, nb + 2, ATT_BLOCK, ATT_KV_HEADS, d)
        return jnp.concatenate([tp[:, :-2], tp[:, 1:-1], tp[:, 2:]], axis=2)

    kb, vb = neighbours(kr), neighbours(vr)
    qi = jnp.arange(ATT_BLOCK)
    kj = jnp.arange(3 * ATT_BLOCK)
    dist = jnp.abs(kj[None, :] - ATT_BLOCK - qi[:, None])
    kpos = (jnp.arange(nb)[:, None] - 1) * ATT_BLOCK + kj[None, :]
    valid = (dist <= WINDOW)[None] & ((kpos >= 0) & (kpos < S))[:, None, :]
    alibi = -slopes[..., None, None] * dist.astype(jnp.float32)
    s_band = jnp.einsum('bnikgd,bnjkd->bnkgij', qb, kb) + alibi
    s_band = jnp.where(valid[None, :, None, None], s_band, -jnp.inf)
    s_meta = jnp.einsum('bnikgd,bmkd->bnkgim', qb, km)
    s_sink = jnp.broadcast_to(sink[:, :, None, None], s_meta.shape[:-1] + (1,))
    p = jax.nn.softmax(jnp.concatenate([s_sink, s_meta, s_band], -1), axis=-1)
    o_real = (jnp.einsum('bnkgim,bmkd->bnikgd', p[..., 1:1 + N_META], vm)
              + jnp.einsum('bnkgij,bnjkd->bnikgd', p[..., 1 + N_META:], vb))
    o_real = o_real.reshape(B, S, ATT_HEADS, d)

    mi = jnp.arange(N_META)
    rj = jnp.arange(ATT_BLOCK)
    mdist = (rj[None, :] + N_META - mi[:, None]).astype(jnp.float32)
    sm_band = jnp.einsum('bikgd,bjkd->bkgij', qm, kr[:, :ATT_BLOCK]) - slopes[..., None, None] * mdist
    sm_band = jnp.where(mdist <= WINDOW, sm_band, -jnp.inf)
    sm_meta = jnp.einsum('bikgd,bmkd->bkgim', qm, km)
    sm_sink = jnp.broadcast_to(sink[:, :, None, None], sm_meta.shape[:-1] + (1,))
    pm = jax.nn.softmax(jnp.concatenate([sm_sink, sm_meta, sm_band], -1), axis=-1)
    o_meta = (jnp.einsum('bkgim,bmkd->bikgd', pm[..., 1:1 + N_META], vm)
              + jnp.einsum('bkgij,bjkd->bikgd', pm[..., 1 + N_META:], vr[:, :ATT_BLOCK]))
    o_meta = o_meta.reshape(B, N_META, ATT_HEADS, d)
    return jnp.concatenate([o_meta, o_real], axis=1)


def retention_attention_mixer(h, w_in, w_out, sink):
    B, L, _ = h.shape
    rq, rk, rv, rg, aq, ak, av = jnp.split(h @ w_in, list(EVEN_SPLITS), axis=-1)
    f32 = jnp.float32
    rq = rq.astype(f32).reshape(B, L, RET_HEADS, RET_HD)
    rk = rk.astype(f32).reshape(B, L, RET_HEADS, RET_HD) * (RET_HD ** -0.5)
    rv = rv.astype(f32).reshape(B, L, RET_HEADS, RET_HD)
    ret = _head_norm(retention_bidir(rq, rk, rv)).reshape(B, L, RET_W)
    ret = jax.nn.silu(rg.astype(f32)) * ret
    att = window_attention(aq.astype(f32).reshape(B, L, ATT_HEADS, ATT_HD),
                           ak.astype(f32).reshape(B, L, ATT_KV_HEADS, ATT_HD),
                           av.astype(f32).reshape(B, L, ATT_KV_HEADS, ATT_HD), sink).reshape(B, L, ATT_Q_W)
    return jnp.concatenate([ret, att], axis=-1).astype(h.dtype) @ w_out


def _gated_delta_dir(q, k, v, g, beta):
    B, Lp, H, dk = k.shape
    dv = v.shape[-1]
    n = Lp // CHUNK
    pos = jnp.arange(CHUNK)
    tril = pos[:, None] >= pos[None, :]
    strict = pos[:, None] > pos[None, :]
    eye = jnp.eye(CHUNK, dtype=jnp.float32)

    def to_chunks(t):
        return jnp.moveaxis(t.reshape((B, n, CHUNK) + t.shape[2:]), (1, 3), (0, 2))

    def step(S, inp):
        qc, kc, vc, gc, bc = inp
        gc = jnp.cumsum(gc, axis=-1)
        gd = gc[..., :, None] - gc[..., None, :]
        decay = jnp.where(tril, jnp.exp(jnp.where(tril, gd, 0.0)), 0.0)
        kb = kc * bc[..., None]
        a = jnp.where(strict, jnp.einsum('bhid,bhjd->bhij', kb, kc) * decay, 0.0) + eye
        rhs = jnp.concatenate([vc * bc[..., None], kb * jnp.exp(gc)[..., None]], axis=-1)
        sol = lax.linalg.triangular_solve(a, rhs, left_side=True, lower=True, unit_diagonal=True)
        u, w = sol[..., :dv], sol[..., dv:]
        v_new = u - jnp.einsum('bhcd,bhde->bhce', w, S)
        attn = jnp.einsum('bhid,bhjd->bhij', qc, kc) * decay
        out = (jnp.einsum('bhcd,bhde->bhce', qc * jnp.exp(gc)[..., None], S)
               + jnp.einsum('bhij,bhje->bhie', attn, v_new))
        g_last = gc[..., -1:]
        S = S * jnp.exp(g_last)[..., None] + jnp.einsum('bhcd,bhce->bhde', kc * jnp.exp(g_last - gc)[..., None], v_new)
        return S, out

    S0 = jnp.zeros((B, H, dk, dv), jnp.float32)
    _, out = lax.scan(step, S0, (to_chunks(q), to_chunks(k), to_chunks(v), to_chunks(g), to_chunks(beta)))
    return jnp.moveaxis(out, (0, 2), (1, 3)).reshape(B, Lp, H, dv)


def gated_deltanet_mixer(h, w_in, conv_w, a_log, dt_bias, norm_w, w_out):
    B, L, _ = h.shape
    f32 = jnp.float32
    proj = h @ w_in
    qkv = proj[..., :GDN_CONV_CH]
    z = proj[..., GDN_CONV_CH:GDN_CONV_CH + GDN_V_W].astype(f32).reshape(B, L, GDN_V_HEADS, GDN_HD)
    ba = proj[..., GDN_CONV_CH + GDN_V_W:].astype(f32).reshape(B, L, 2, 2, GDN_V_HEADS)
    qkv = lax.conv_general_dilated(qkv, conv_w[:, None, :], window_strides=(1,),
                                   padding=[(CONV_WIDTH // 2, CONV_WIDTH // 2)],
                                   dimension_numbers=('NWC', 'WIO', 'NWC'),
                                   feature_group_count=GDN_CONV_CH)
    qkv = jax.nn.silu(qkv.astype(f32))
    rep = GDN_V_HEADS // GDN_QK_HEADS
    q = _l2norm(qkv[..., :GDN_K_W].reshape(B, L, GDN_QK_HEADS, GDN_HD)) * (GDN_HD ** -0.5)
    k = _l2norm(qkv[..., GDN_K_W:2 * GDN_K_W].reshape(B, L, GDN_QK_HEADS, GDN_HD))
    v = qkv[..., 2 * GDN_K_W:].reshape(B, L, GDN_V_HEADS, GDN_HD)
    q = jnp.repeat(q, rep, axis=2)
    k = jnp.repeat(k, rep, axis=2)
    beta = jax.nn.sigmoid(ba[:, :, :, 0])
    g = -jnp.exp(a_log.astype(f32)) * jax.nn.softplus(ba[:, :, :, 1] + dt_bias.astype(f32))
    qp, kp, vp = _left_pad(q, META_PAD), _left_pad(k, META_PAD), _left_pad(v, META_PAD)
    gp, bp = _left_pad(g, META_PAD), _left_pad(beta, META_PAD)
    fwd = _gated_delta_dir(qp, kp, vp, gp[:, :, 0], bp[:, :, 0])
    fl = lambda t: jnp.flip(t, 1)
    bwd = fl(_gated_delta_dir(fl(qp), fl(kp), fl(vp), fl(gp[:, :, 1]), fl(bp[:, :, 1])))
    o = (fwd + bwd)[:, META_PAD:]
    o = o * lax.rsqrt(jnp.mean(o * o, -1, keepdims=True) + NORM_EPS) * norm_w.astype(f32) * jax.nn.silu(z)
    return o.reshape(B, L, GDN_V_W).astype(h.dtype) @ w_out


def moe(h, w_router, router_bias, w_gate, w_up, w_down, ws_gate, ws_up, ws_down):
    B, L, D = h.shape
    T = B * L
    f32 = jnp.float32
    xt = h.reshape(T, D)
    scores = jax.nn.sigmoid((xt @ w_router).astype(f32))
    choice = scores + router_bias.astype(f32)
    grp = choice.reshape(T, N_GROUPS, N_EXPERTS // N_GROUPS)
    grp_score = lax.top_k(grp, 2)[0].sum(-1)
    gsel = lax.top_k(grp_score, TOPK_GROUPS)[1]
    gmask = jnp.any(gsel[:, :, None] == jnp.arange(N_GROUPS)[None, None, :], axis=1)
    emask = jnp.repeat(gmask, N_EXPERTS // N_GROUPS, axis=1)
    eidx = lax.top_k(jnp.where(emask, choice, -jnp.inf), TOP_K)[1]
    wts = jnp.take_along_axis(scores, eidx, axis=1)
    wts = wts / wts.sum(-1, keepdims=True) * ROUTE_SCALE

    TK = T * TOP_K
    flat_e = eidx.reshape(-1)
    order = jnp.argsort(flat_e)
    se = flat_e[order]
    stok = (order // TOP_K).astype(jnp.int32)
    sw = wts.reshape(-1)[order]
    counts = jnp.bincount(flat_e, length=N_EXPERTS)
    starts = jnp.cumsum(counts) - counts
    pcounts = (counts + MOE_BLOCK - 1) // MOE_BLOCK * MOE_BLOCK
    pends = jnp.cumsum(pcounts)
    pstarts = pends - pcounts
    dest = pstarts[se] + jnp.arange(TK) - starts[se]
    nblk = -(-TK // MOE_BLOCK) + N_EXPERTS
    rows = nblk * MOE_BLOCK
    buf_tok = jnp.full((rows,), T, jnp.int32).at[dest].set(stok)
    buf_w = jnp.zeros((rows,), f32).at[dest].set(sw)
    blk_e = jnp.minimum(jnp.searchsorted(pends, jnp.arange(nblk) * MOE_BLOCK, side='right'), N_EXPERTS - 1)
    x_pad = jnp.concatenate([xt, jnp.zeros((1, D), xt.dtype)], axis=0)

    def step(acc, inp):
        tok, w, e = inp
        xb = x_pad[tok]
        hb = jax.nn.silu(xb @ w_gate[e]) * (xb @ w_up[e])
        yb = (hb @ w_down[e]).astype(f32) * w[:, None]
        return acc.at[tok].add(yb), None

    acc0 = jnp.zeros((T + 1, D), f32)
    acc, _ = lax.scan(step, acc0, (buf_tok.reshape(nblk, MOE_BLOCK), buf_w.reshape(nblk, MOE_BLOCK), blk_e))
    shared = (jax.nn.silu(xt @ ws_gate) * (xt @ ws_up)) @ ws_down
    return (acc[:T] + shared.astype(f32)).astype(h.dtype).reshape(B, L, D)


def setup_inputs(seed: int = 0) -> dict:
    key = jax.random.key(seed)
    ks = jax.random.split(key, 21)
    n_ev = (DEPTH + 1) // 2
    n_od = DEPTH // 2
    f32 = jnp.float32

    def nrm(k, shape, scale):
        return jax.random.normal(k, shape, f32) * scale

    dt = jnp.exp(jax.random.uniform(ks[8], (n_od, 2, GDN_V_HEADS), f32) * (np.log(0.1) - np.log(0.001)) + np.log(0.001))
    return {
        'x': nrm(ks[0], (BATCH, SEQ, D_MODEL), 1.0),
        'meta_tokens': nrm(ks[1], (N_META, D_MODEL), 1.0),
        'ev_w_in': nrm(ks[2], (n_ev, D_MODEL, EVEN_IN), D_MODEL ** -0.5),
        'ev_w_out': nrm(ks[3], (n_ev, EVEN_MIX, D_MODEL), DN_BETA * EVEN_MIX ** -0.5),
        'ev_sink': nrm(ks[4], (n_ev, ATT_HEADS), 1.0),
        'od_w_in': nrm(ks[5], (n_od, D_MODEL, ODD_IN), D_MODEL ** -0.5),
        'od_conv_w': nrm(ks[6], (n_od, CONV_WIDTH, GDN_CONV_CH), CONV_WIDTH ** -0.5),
        'od_a_log': jnp.log(jax.random.uniform(ks[7], (n_od, 2, GDN_V_HEADS), f32, 1.0, 16.0)),
        'od_dt_bias': dt + jnp.log(-jnp.expm1(-dt)),
        'od_norm_w': 1.0 + nrm(ks[9], (n_od, GDN_HD), 0.02),
        'od_w_out': nrm(ks[10], (n_od, GDN_V_W, D_MODEL), DN_BETA * GDN_V_W ** -0.5),
        'ln_g': 1.0 + nrm(ks[11], (DEPTH, 2, D_MODEL), 0.02),
        'ln_b': nrm(ks[12], (DEPTH, 2, D_MODEL), 0.02),
        'w_router': nrm(ks[13], (DEPTH, D_MODEL, N_EXPERTS), D_MODEL ** -0.5),
        'router_bias': nrm(ks[14], (DEPTH, N_EXPERTS), 0.01),
        'w_gate': nrm(ks[15], (DEPTH, N_EXPERTS, D_MODEL, D_EXPERT), D_MODEL ** -0.5),
        'w_up': nrm(ks[16], (DEPTH, N_EXPERTS, D_MODEL, D_EXPERT), D_MODEL ** -0.5),
        'w_down': nrm(ks[17], (DEPTH, N_EXPERTS, D_EXPERT, D_MODEL), DN_BETA * D_EXPERT ** -0.5),
        'ws_gate': nrm(ks[18], (DEPTH, D_MODEL, D_SHARED), D_MODEL ** -0.5),
        'ws_up': nrm(ks[19], (DEPTH, D_MODEL, D_SHARED), D_MODEL ** -0.5),
        'ws_down': nrm(ks[20], (DEPTH, D_SHARED, D_MODEL), DN_BETA * D_SHARED ** -0.5),
    }


def reference(x, meta_tokens, ev_w_in, ev_w_out, ev_sink, od_w_in, od_conv_w, od_a_log, od_dt_bias,
              od_norm_w, od_w_out, ln_g, ln_b, w_router, router_bias, w_gate, w_up, w_down,
              ws_gate, ws_up, ws_down):
    B = x.shape[0]
    meta = jnp.broadcast_to(meta_tokens.astype(x.dtype)[None], (B, N_META, D_MODEL))
    h = jnp.concatenate([meta, x], axis=1)
    for layer in range(DEPTH):
        i = layer // 2
        if layer % 2 == 0:
            mix = retention_attention_mixer(h, ev_w_in[i], ev_w_out[i], ev_sink[i])
        else:
            mix = gated_deltanet_mixer(h, od_w_in[i], od_conv_w[i], od_a_log[i], od_dt_bias[i],
                                       od_norm_w[i], od_w_out[i])
        h = _layer_norm(DN_ALPHA * h + mix, ln_g[layer, 0], ln_b[layer, 0])
        ffn = moe(h, w_router[layer], router_bias[layer], w_gate[layer], w_up[layer], w_down[layer],
                  ws_gate[layer], ws_up[layer], ws_down[layer])
        h = _layer_norm(DN_ALPHA * h + ffn, ln_g[layer, 1], ln_b[layer, 1])
    return h[:, N_META:]
```

```python
import functools
import math

import numpy as np
import jax
import jax.numpy as jnp
from jax import lax
from jax.experimental import pallas as pl
from jax.experimental.pallas import tpu as pltpu

F32 = jnp.float32
BF16 = jnp.bfloat16

D_MODEL = 2048
DEPTH = 4
N_META = 16
FRONT = 128
DEAD = FRONT - N_META
RET_HEADS = 8
RET_HD = 128
RET_W = RET_HEADS * RET_HD
RET_CHUNK = 128
ATT_HEADS = 8
ATT_KV_HEADS = 2
ATT_GROUP = ATT_HEADS // ATT_KV_HEADS
ATT_HD = 128
ATT_BLOCK = 128
WINDOW = 128
EVEN_IN = 4 * RET_W + ATT_HEADS * ATT_HD + 2 * ATT_KV_HEADS * ATT_HD
GDN_QK_HEADS = 16
GDN_V_HEADS = 32
GDN_HD = 128
GDN_K_W = GDN_QK_HEADS * GDN_HD
GDN_V_W = GDN_V_HEADS * GDN_HD
GDN_CONV_CH = 2 * GDN_K_W + GDN_V_W
GDN_CHUNK = 64
GDN_HG = 8
CONV_WIDTH = 5
N_EXPERTS = 64
TOP_K = 8
N_GROUPS = 8
GROUP_SIZE = N_EXPERTS // N_GROUPS
TOPK_GROUPS = 4
D_EXPERT = 384
ROUTE_SCALE = 2.5
MOE_BLOCK = 256
DN_ALPHA = (2 * DEPTH) ** 0.25
LN_EPS = 1e-5
NORM_EPS = 1e-6
NEG = -1e30

VMEM_LIMIT = 56 * 2**20


def _cp(*sem, vmem=VMEM_LIMIT):
    return pltpu.CompilerParams(dimension_semantics=sem, vmem_limit_bytes=vmem)


def _dot(a, b):
    return jnp.dot(a, b, preferred_element_type=F32)


def _dot_nt(a, b):
    return lax.dot_general(a, b, (((1,), (1,)), ((), ())), preferred_element_type=F32)


def _dot_tn(a, b):
    return lax.dot_general(a, b, (((0,), (0,)), ((), ())), preferred_element_type=F32)


def _split3(a):
    hi = a.astype(BF16)
    r1 = a - hi.astype(F32)
    mid = r1.astype(BF16)
    lo = (r1 - mid.astype(F32)).astype(BF16)
    return hi, mid, lo


def _pick_tile(n, cap):
    for t in range(min(cap, n), 0, -1):
        if n % t == 0 and t % 8 == 0:
            return t
    return n


def _mm_kernel(*refs, ksplits):
    nx = len(ksplits)
    x_refs, w_ref, o_ref, wb_ref = refs[:nx], refs[nx], refs[nx + 1], refs[nx + 2]

    @pl.when(pl.program_id(1) == 0)
    def _():
        wb_ref[...] = w_ref[...].astype(BF16)

    acc = None
    k0 = 0
    for x_ref, kk in zip(x_refs, ksplits):
        part = _dot(x_ref[...].astype(BF16), wb_ref[k0:k0 + kk, :])
        acc = part if acc is None else acc + part
        k0 += kk
    o_ref[...] = acc.astype(o_ref.dtype)


def _matmul(xs, w, col0, ncols, tn, out_dtype=F32, tm_cap=512):
    M = xs[0].shape[0]
    ksplits = tuple(x.shape[1] for x in xs)
    K = sum(ksplits)
    assert w.shape[0] == K and ncols % tn == 0 and col0 % tn == 0
    tm = _pick_tile(M, tm_cap)
    in_specs = [pl.BlockSpec((tm, kk), lambda j, i: (i, 0)) for kk in ksplits]
    in_specs.append(pl.BlockSpec((K, tn), lambda j, i: (0, col0 // tn + j)))
    return pl.pallas_call(
        functools.partial(_mm_kernel, ksplits=ksplits),
        grid=(ncols // tn, M // tm),
        in_specs=in_specs,
        out_specs=pl.BlockSpec((tm, tn), lambda j, i: (i, j)),
        out_shape=jax.ShapeDtypeStruct((M, ncols), out_dtype),
        scratch_shapes=[pltpu.VMEM((K, tn), BF16)],
        compiler_params=_cp("arbitrary", "arbitrary"),
        name="matmul",
    )(*xs, w)


def _ln_kernel(*refs, n_add):
    h_ref = refs[0]
    add_refs = refs[1:1 + n_add]
    g_ref, b_ref, o_ref, ob_ref = refs[1 + n_add:]
    s = add_refs[0][...]
    for a in add_refs[1:]:
        s = s + a[...]
    y = DN_ALPHA * h_ref[...] + s
    mu = jnp.mean(y, axis=-1, keepdims=True)
    d = y - mu
    var = jnp.mean(d * d, axis=-1, keepdims=True)
    out = d * lax.rsqrt(var + LN_EPS) * g_ref[...] + b_ref[...]
    o_ref[...] = out
    ob_ref[...] = out.astype(BF16)


def _residual_ln(h, adds, g, b):
    M, Dm = h.shape
    tm = _pick_tile(M, 256)
    row = pl.BlockSpec((tm, Dm), lambda i: (i, 0))
    vec = pl.BlockSpec((1, Dm), lambda i: (0, 0))
    return pl.pallas_call(
        functools.partial(_ln_kernel, n_add=len(adds)),
        grid=(M // tm,),
        in_specs=[row] * (1 + len(adds)) + [vec, vec],
        out_specs=[row, row],
        out_shape=[jax.ShapeDtypeStruct((M, Dm), F32), jax.ShapeDtypeStruct((M, Dm), BF16)],
        compiler_params=_cp("arbitrary"),
        name="residual_ln",
    )(h, *adds, g.reshape(1, Dm), b.reshape(1, Dm))


def _ret_tables(C):
    hh = np.arange(RET_HEADS, dtype=np.float64)
    lg = np.log(1.0 - 2.0 ** (-5.0 - hh))[:, None]
    pos = np.arange(C, dtype=np.float64)[None, :]
    vecs = np.stack([np.exp(lg * (pos + 1.0)),
                     np.exp(lg * (C - pos)),
                     np.exp(lg * (C - 1.0 - pos)),
                     np.exp(lg * pos),
                     np.exp(lg * C) * np.ones_like(pos)], axis=1)
    tab = np.broadcast_to(vecs[..., None], (RET_HEADS, 5, C, RET_HD))
    rel = np.abs(pos.T - pos)
    dsym = np.exp(lg[:, :, None] * rel[None])
    return jnp.asarray(tab, F32), jnp.asarray(dsym, F32)


def _ret_kernel(q_ref, k_ref, v_ref, g_ref, tab_ref, d_ref, o_ref, fst_ref, run_ref, *, nc, C):
    s = pl.program_id(2)
    fwd = s < nc
    c = jnp.where(fwd, s, 2 * nc - 1 - s)
    row = c * C + lax.broadcasted_iota(jnp.int32, (C, 1), 0)
    live = row >= DEAD
    k = jnp.where(live, k_ref[0] * (RET_HD ** -0.5), 0.0)
    v = jnp.where(live, v_ref[0], 0.0)
    vb = v.astype(BF16)
    g_chunk = tab_ref[0, 4]

    @pl.when(jnp.logical_or(s == 0, s == nc))
    def _():
        run_ref[...] = jnp.zeros_like(run_ref)

    @pl.when(fwd)
    def _():
        fst_ref[c] = run_ref[...].astype(BF16)
        kk = (k * tab_ref[0, 2]).astype(BF16)
        run_ref[...] = run_ref[...] * g_chunk + _dot_tn(kk, vb)

    @pl.when(jnp.logical_not(fwd))
    def _():
        q = q_ref[0]
        sc = _dot_nt(q.astype(BF16), k.astype(BF16)) * d_ref[0]
        out = (_dot(sc.astype(BF16), vb)
               + _dot((q * tab_ref[0, 0]).astype(BF16), fst_ref[c])
               + _dot((q * tab_ref[0, 1]).astype(BF16), run_ref[...].astype(BF16)))
        kk = (k * tab_ref[0, 3]).astype(BF16)
        run_ref[...] = run_ref[...] * g_chunk + _dot_tn(kk, vb)
        mu = jnp.mean(out, axis=-1, keepdims=True)
        dlt = out - mu
        var = jnp.mean(dlt * dlt, axis=-1, keepdims=True)
        normed = dlt * lax.rsqrt(var + NORM_EPS)
        g = g_ref[0]
        o_ref[0] = (g * jax.nn.sigmoid(g) * normed).astype(o_ref.dtype)


def _retention(proj):
    B, Lp, _ = proj.shape
    C = RET_CHUNK
    nc = Lp // C
    tab, dsym = _ret_tables(C)

    def cidx(s):
        return jnp.where(s < nc, s, 2 * nc - 1 - s)

    def cidx_out(s):
        return jnp.where(s < nc, nc - 1, 2 * nc - 1 - s)

    return pl.pallas_call(
        functools.partial(_ret_kernel, nc=nc, C=C),
        grid=(B, RET_HEADS, 2 * nc),
        in_specs=[
            pl.BlockSpec((1, C, RET_HD), lambda b, h, s: (b, cidx_out(s), h)),
            pl.BlockSpec((1, C, RET_HD), lambda b, h, s: (b, cidx(s), RET_HEADS + h)),
            pl.BlockSpec((1, C, RET_HD), lambda b, h, s: (b, cidx(s), 2 * RET_HEADS + h)),
            pl.BlockSpec((1, C, RET_HD), lambda b, h, s: (b, cidx_out(s), 3 * RET_HEADS + h)),
            pl.BlockSpec((1, 5, C, RET_HD), lambda b, h, s: (h, 0, 0, 0)),
            pl.BlockSpec((1, C, C), lambda b, h, s: (h, 0, 0)),
        ],
        out_specs=pl.BlockSpec((1, C, RET_HD), lambda b, h, s: (b, cidx_out(s), h)),
        out_shape=jax.ShapeDtypeStruct((B, Lp, RET_W), BF16),
        scratch_shapes=[pltpu.VMEM((nc, RET_HD, RET_HD), BF16), pltpu.VMEM((RET_HD, RET_HD), F32)],
        compiler_params=_cp("arbitrary", "arbitrary", "arbitrary"),
        name="retention",
    )(proj, proj, proj, proj, tab, dsym)


def _att_kernel(sink_ref, slope_ref, q_ref, km_ref, kp_ref, kc_ref, kn_ref,
                vm_ref, vp_ref, vc_ref, vn_ref, o_ref, *, nb):
    j = pl.program_id(1)
    qb = pl.program_id(2)
    T = ATT_BLOCK
    ii = lax.broadcasted_iota(jnp.int32, (T, T), 0)
    jj = lax.broadcasted_iota(jnp.int32, (T, T), 1)
    meta_ok = jj >= DEAD
    pieces = []
    for off, k_ref, v_ref, ok in ((-T, kp_ref, vp_ref, qb >= 2),
                                  (0, kc_ref, vc_ref, qb >= 1),
                                  (T, kn_ref, vn_ref, qb + 1 <= nb)):
        dist = jnp.abs(jj + off - ii)
        pieces.append((dist.astype(F32), dist <= WINDOW, k_ref[0].astype(BF16), v_ref[0].astype(BF16), ok))
    km = km_ref[0].astype(BF16)
    vm = vm_ref[0].astype(BF16)
    for g in range(ATT_GROUP):
        slope = slope_ref[j, g]
        sink = sink_ref[j, g]
        q = (q_ref[0][:, g * ATT_HD:(g + 1) * ATT_HD] * (ATT_HD ** -0.5)).astype(BF16)
        s_list = [jnp.where(meta_ok, _dot_nt(q, km), NEG)]
        for dist, inwin, kb, _, ok in pieces:
            sb = jnp.where(inwin, _dot_nt(q, kb) - slope * dist, NEG)
            s_list.append(jnp.where(ok, sb, NEG))
        m = jnp.full((T, 1), sink, F32)
        for sb in s_list:
            m = jnp.maximum(m, jnp.max(sb, axis=-1, keepdims=True))
        den = jnp.exp(sink - m)
        acc = jnp.zeros((T, ATT_HD), F32)
        for sb, vb in zip(s_list, [vm] + [p[3] for p in pieces]):
            e = jnp.exp(sb - m)
            den = den + jnp.sum(e, axis=-1, keepdims=True)
            acc = acc + _dot(e.astype(BF16), vb)
        o_ref[0, :, g * ATT_HD:(g + 1) * ATT_HD] = (acc / den).astype(o_ref.dtype)


def _window_attention(proj, sink):
    B, Lp, _ = proj.shape
    nb = Lp // ATT_BLOCK - 1
    gw = ATT_GROUP * ATT_HD
    q0 = 4 * RET_W // gw
    k0 = (4 * RET_W + ATT_HEADS * ATT_HD) // ATT_HD
    v0 = k0 + ATT_KV_HEADS
    slopes = np.asarray(2.0 ** (-8.0 * (np.arange(ATT_HEADS) + 1.0) / ATT_HEADS), np.float32)
    smem = pl.BlockSpec(memory_space=pltpu.SMEM)

    def kv(c0, f):
        return pl.BlockSpec((1, ATT_BLOCK, ATT_HD), lambda b, j, t: (b, f(t), c0 + j))

    rows = [lambda t: 0, lambda t: jnp.maximum(t - 1, 0), lambda t: t, lambda t: jnp.minimum(t + 1, nb)]
    return pl.pallas_call(
        functools.partial(_att_kernel, nb=nb),
        grid=(B, ATT_KV_HEADS, nb + 1),
        in_specs=[smem, smem, pl.BlockSpec((1, ATT_BLOCK, gw), lambda b, j, t: (b, t, q0 + j))]
        + [kv(k0, f) for f in rows] + [kv(v0, f) for f in rows],
        out_specs=pl.BlockSpec((1, ATT_BLOCK, gw), lambda b, j, t: (b, t, j)),
        out_shape=jax.ShapeDtypeStruct((B, Lp, ATT_HEADS * ATT_HD), BF16),
        compiler_params=_cp("arbitrary", "arbitrary", "arbitrary"),
        name="window_attention",
    )(sink.astype(F32).reshape(ATT_KV_HEADS, ATT_GROUP), jnp.asarray(slopes).reshape(ATT_KV_HEADS, ATT_GROUP),
      *([proj] * 9))


def _conv_kernel(xp_ref, x_ref, xn_ref, w_ref, o_ref, *, mode, TT, nt):
    t = pl.program_id(1)
    rows = t * TT + lax.broadcasted_iota(jnp.int32, (TT, 1), 0)
    live = rows >= DEAD
    x = jnp.where(live, x_ref[0], 0.0)
    prow = t * TT - 8 + lax.broadcasted_iota(jnp.int32, (8, 1), 0)
    prev = jnp.where(prow >= DEAD, xp_ref[0], 0.0)
    nrow = (t + 1) * TT + lax.broadcasted_iota(jnp.int32, (8, 1), 0)
    nxt = jnp.where(jnp.logical_and(nrow >= DEAD, t < nt - 1), xn_ref[0], 0.0)
    xx = jnp.concatenate([prev, x, nxt], axis=0)
    w = w_ref[...]
    half = CONV_WIDTH // 2
    acc = None
    for tap in range(CONV_WIDTH):
        term = xx[8 - half + tap:8 - half + tap + TT, :] * w[tap:tap + 1, :]
        acc = term if acc is None else acc + term
    y = acc * jax.nn.sigmoid(acc)
    if mode in ("q", "k"):
        scale = GDN_HD ** -0.5 if mode == "q" else 1.0
        for a in range(y.shape[1] // GDN_HD):
            ya = y[:, a * GDN_HD:(a + 1) * GDN_HD]
            ya = ya * lax.rsqrt(jnp.sum(ya * ya, axis=-1, keepdims=True) + NORM_EPS)
            if mode == "q":
                ya = ya * scale
            o_ref[0, :, a * GDN_HD:(a + 1) * GDN_HD] = jnp.where(live, ya, 0.0).astype(o_ref.dtype)
    else:
        o_ref[0] = jnp.where(live, y, 0.0).astype(o_ref.dtype)


def _short_conv(qkv, conv_w, mode, col0, ncols):
    B, Lp, _ = qkv.shape
    TT = _pick_tile(Lp, 256)
    TC = 512
    nt = Lp // TT
    c0 = col0 // TC
    return pl.pallas_call(
        functools.partial(_conv_kernel, mode=mode, TT=TT, nt=nt),
        grid=(B, nt, ncols // TC),
        in_specs=[
            pl.BlockSpec((1, 8, TC), lambda b, t, c: (b, jnp.maximum(t * (TT // 8) - 1, 0), c0 + c)),
            pl.BlockSpec((1, TT, TC), lambda b, t, c: (b, t, c0 + c)),
            pl.BlockSpec((1, 8, TC), lambda b, t, c: (b, jnp.minimum((t + 1) * (TT // 8), Lp // 8 - 1), c0 + c)),
            pl.BlockSpec((CONV_WIDTH, TC), lambda b, t, c: (0, c0 + c)),
        ],
        out_specs=pl.BlockSpec((1, TT, TC), lambda b, t, c: (b, t, c)),
        out_shape=jax.ShapeDtypeStruct((B, Lp, ncols), BF16),
        compiler_params=_cp("arbitrary", "arbitrary", "arbitrary"),
        name="short_conv_" + mode,
    )(qkv, qkv, qkv, conv_w)


def _gate_kernel(ba_ref, alog_ref, dtb_ref, o_ref, *, TT):
    t = pl.program_id(1)
    rows = t * TT + lax.broadcasted_iota(jnp.int32, (TT, 1), 0)
    lane = lax.broadcasted_iota(jnp.int32, (1, 4 * GDN_V_HEADS), 1)
    is_g = (lane // GDN_V_HEADS) % 2 == 1
    x = ba_ref[0]
    beta = jax.nn.sigmoid(x)
    xs = x + dtb_ref[...]
    softplus = jnp.maximum(xs, 0.0) + jnp.log(1.0 + jnp.exp(-jnp.abs(xs)))
    g = -jnp.exp(alog_ref[...]) * softplus
    o_ref[0] = jnp.where(rows >= DEAD, jnp.where(is_g, g, beta), 0.0)


def _gates(ba, a_log, dt_bias):
    B, Lp, W = ba.shape
    TT = _pick_tile(Lp, 1024)
    zeros = jnp.zeros((2, 1, GDN_V_HEADS), F32)
    alog = jnp.concatenate([zeros, a_log.astype(F32)[:, None, :]], axis=1).reshape(1, W)
    dtb = jnp.concatenate([zeros, dt_bias.astype(F32)[:, None, :]], axis=1).reshape(1, W)
    vec = pl.BlockSpec((1, W), lambda b, t: (0, 0))
    return pl.pallas_call(
        functools.partial(_gate_kernel, TT=TT),
        grid=(B, Lp // TT),
        in_specs=[pl.BlockSpec((1, TT, W), lambda b, t: (b, t, 0)), vec, vec],
        out_specs=pl.BlockSpec((1, TT, W), lambda b, t: (b, t, 0)),
        out_shape=jax.ShapeDtypeStruct((B, Lp, W), F32),
        compiler_params=_cp("arbitrary", "arbitrary"),
        name="gdn_gates",
    )(ba, alog, dtb)


def _gdn_kernel(*refs, rev, final):
    if final:
        q_ref, k_ref, v_ref, gc_ref, gr_ref, of_ref, z_ref, nw_ref, o_ref, s_ref = refs
    else:
        q_ref, k_ref, v_ref, gc_ref, gr_ref, o_ref, s_ref = refs
    C = GDN_CHUNK
    HG = GDN_HG

    @pl.when(pl.program_id(2) == 0)
    def _():
        s_ref[...] = jnp.zeros_like(s_ref)

    ii = lax.broadcasted_iota(jnp.int32, (C, C), 0)
    jj = lax.broadcasted_iota(jnp.int32, (C, C), 1)
    incl = (jj >= ii) if rev else (jj <= ii)
    strict = (jj > ii) if rev else (jj < ii)
    tri_col = jnp.where(incl, 1.0, 0.0).astype(BF16)
    tri_row = jnp.where((ii >= jj) if rev else (ii <= jj), 1.0, 0.0).astype(BF16)
    gcol = gc_ref[0, 0]
    grow = gr_ref[0, 0, 0]
    gc_col = sum(_dot(tri_col, p) for p in _split3(gcol))
    gc_row = sum(_dot(p, tri_row) for p in _split3(grow))
    last = 0 if rev else C - 1
    d0 = 2 * HG if rev else 0
    q_all = q_ref[0]
    k_all = k_ref[0]
    v_all = v_ref[0]
    for hh in range(HG):
        lb = d0 + hh
        lg = d0 + HG + hh
        beta = gcol[:, lb:lb + 1]
        gcc = gc_col[:, lg:lg + 1]
        gcr = gc_row[lg:lg + 1, :]
        gtot = gcc[last:last + 1, :]
        decay = jnp.where(incl, jnp.exp(jnp.where(incl, gcc - gcr, 0.0)), 0.0)
        hq = hh // 2
        qh = q_all[:, hq * GDN_HD:(hq + 1) * GDN_HD]
        kh = k_all[:, hq * GDN_HD:(hq + 1) * GDN_HD]
        vh = v_all[:, hh * GDN_HD:(hh + 1) * GDN_HD].astype(F32)
        kf = kh.astype(F32)
        kbeta = kf * beta
        eg = jnp.exp(gcc)
        x = -jnp.where(strict, _dot_nt(kbeta.astype(BF16), kh) * decay, 0.0)
        m = x
        y = x
        for _ in range(5):
            yb = y.astype(BF16)
            y = _dot(yb, yb)
            m = m + y + _dot(m.astype(BF16), y.astype(BF16))
        mb = m.astype(BF16)
        rhs_u = vh * beta
        rhs_w = kbeta * eg
        u = rhs_u + _dot(mb, rhs_u.astype(BF16))
        w = rhs_w + _dot(mb, rhs_w.astype(BF16))
        st = s_ref[hh]
        stb = st.astype(BF16)
        v_new = u - _dot(w.astype(BF16), stb)
        vnb = v_new.astype(BF16)
        attn = _dot_nt(qh, kh) * decay
        out = _dot((qh.astype(F32) * eg).astype(BF16), stb) + _dot(attn.astype(BF16), vnb)
        s_ref[hh] = st * jnp.exp(gtot) + _dot_tn((kf * jnp.exp(gtot - gcc)).astype(BF16), vnb)
        sl = slice(hh * GDN_HD, (hh + 1) * GDN_HD)
        if final:
            o = out + of_ref[0, :, sl]
            z = z_ref[0, :, sl]
            o = o * lax.rsqrt(jnp.mean(o * o, axis=-1, keepdims=True) + NORM_EPS) * nw_ref[...]
            o_ref[0, :, sl] = (o * (z * jax.nn.sigmoid(z))).astype(o_ref.dtype)
        else:
            o_ref[0, :, sl] = out


def _gdn_dir(q, k, v, gcol, grow, rev, o_fwd=None, z=None, norm_w=None):
    B, Lp, _ = v.shape
    C = GDN_CHUNK
    nc = Lp // C
    ng = GDN_V_HEADS // GDN_HG
    qw = GDN_HG // 2 * GDN_HD
    vw = GDN_HG * GDN_HD
    final = o_fwd is not None

    def cc(c):
        return nc - 1 - c if rev else c

    in_specs = [
        pl.BlockSpec((1, C, qw), lambda b, g, c: (b, cc(c), g)),
        pl.BlockSpec((1, C, qw), lambda b, g, c: (b, cc(c), g)),
        pl.BlockSpec((1, C, vw), lambda b, g, c: (b, cc(c), g)),
        pl.BlockSpec((1, 1, C, 4 * GDN_HG), lambda b, g, c: (b, g, cc(c), 0)),
        pl.BlockSpec((1, 1, 1, 4 * GDN_HG, C), lambda b, g, c: (b, g, cc(c), 0, 0)),
    ]
    args = [q, k, v, gcol, grow]
    if final:
        in_specs += [pl.BlockSpec((1, C, vw), lambda b, g, c: (b, cc(c), g)),
                     pl.BlockSpec((1, C, vw), lambda b, g, c: (b, cc(c), g)),
                     pl.BlockSpec((1, GDN_HD), lambda b, g, c: (0, 0))]
        args += [o_fwd, z, norm_w.astype(F32).reshape(1, GDN_HD)]
    return pl.pallas_call(
        functools.partial(_gdn_kernel, rev=rev, final=final),
        grid=(B, ng, nc),
        in_specs=in_specs,
        out_specs=pl.BlockSpec((1, C, vw), lambda b, g, c: (b, cc(c), g)),
        out_shape=jax.ShapeDtypeStruct((B, Lp, GDN_V_W), BF16 if final else F32),
        scratch_shapes=[pltpu.VMEM((GDN_HG, GDN_HD, GDN_HD), F32)],
        compiler_params=_cp("arbitrary", "arbitrary", "arbitrary"),
        name="gdn_bwd" if rev else "gdn_fwd",
    )(*args)


def _gdn_mixer(hb, B, Lp, w_in, conv_w, a_log, dt_bias, norm_w, w_out):
    qkv = _matmul([hb], w_in, 0, GDN_CONV_CH, 1024).reshape(B, Lp, GDN_CONV_CH)
    z = _matmul([hb], w_in, GDN_CONV_CH, GDN_V_W, 1024).reshape(B, Lp, GDN_V_W)
    ba = _matmul([hb], w_in, GDN_CONV_CH + GDN_V_W, 4 * GDN_V_HEADS, 128).reshape(B, Lp, 4 * GDN_V_HEADS)
    q = _short_conv(qkv, conv_w, "q", 0, GDN_K_W)
    k = _short_conv(qkv, conv_w, "k", GDN_K_W, GDN_K_W)
    v = _short_conv(qkv, conv_w, "v", 2 * GDN_K_W, GDN_V_W)
    gb = _gates(ba, a_log, dt_bias)
    ng = GDN_V_HEADS // GDN_HG
    nc = Lp // GDN_CHUNK
    gcol = gb.reshape(B, Lp, 4, ng, GDN_HG).transpose(0, 3, 1, 2, 4).reshape(B, ng, Lp, 4 * GDN_HG)
    grow = gcol.reshape(B, ng, nc, GDN_CHUNK, 4 * GDN_HG).transpose(0, 1, 2, 4, 3)
    o_f = _gdn_dir(q, k, v, gcol, grow, rev=False)
    o = _gdn_dir(q, k, v, gcol, grow, rev=True, o_fwd=o_f, z=z, norm_w=norm_w)
    return _matmul([o.reshape(B * Lp, GDN_V_W)], w_out, 0, D_MODEL, 512)


def _even_mixer(hb, B, Lp, w_in, w_out, sink):
    proj = _matmul([hb], w_in, 0, EVEN_IN, 512).reshape(B, Lp, EVEN_IN)
    ret = _retention(proj).reshape(B * Lp, RET_W)
    att = _window_attention(proj, sink).reshape(B * Lp, ATT_HEADS * ATT_HD)
    return _matmul([ret, att], w_out, 0, D_MODEL, 1024)


def _router_kernel(x_ref, wt_ref, bias_ref, eidx_ref, wts_ref, rank_ref, cnt_ref, carry_ref, *, TM):
    E = N_EXPERTS

    @pl.when(pl.program_id(0) == 0)
    def _():
        carry_ref[...] = jnp.zeros_like(carry_ref)

    xh, xm, xl = _split3(x_ref[...])
    wh, wm, wl = _split3(wt_ref[...])
    logits = (_dot_nt(wh, xh) + (_dot_nt(wh, xm) + _dot_nt(wm, xh))
              + (_dot_nt(wh, xl) + _dot_nt(wl, xh) + _dot_nt(wm, xm)))
    scores = jax.nn.sigmoid(logits)
    choice = scores + bias_ref[...]
    ninf = -jnp.inf
    io8 = lax.broadcasted_iota(jnp.int32, (GROUP_SIZE, TM), 0)
    gs_rows = []
    for g in range(N_GROUPS):
        cg = choice[g * GROUP_SIZE:(g + 1) * GROUP_SIZE, :]
        m1 = jnp.max(cg, axis=0, keepdims=True)
        i1 = jnp.min(jnp.where(cg == m1, io8, GROUP_SIZE), axis=0, keepdims=True)
        m2 = jnp.max(jnp.where(io8 == i1, ninf, cg), axis=0, keepdims=True)
        gs_rows.append(m1 + m2)
    gs = jnp.concatenate(gs_rows, axis=0)
    gsel = jnp.zeros((N_GROUPS, TM), jnp.int32)
    for _ in range(TOPK_GROUPS):
        m = jnp.max(gs, axis=0, keepdims=True)
        idx = jnp.min(jnp.where(gs == m, io8, N_GROUPS), axis=0, keepdims=True)
        hit = io8 == idx
        gsel = jnp.where(hit, 1, gsel)
        gs = jnp.where(hit, ninf, gs)
    masked = jnp.concatenate(
        [jnp.where(gsel[g:g + 1, :] > 0, choice[g * GROUP_SIZE:(g + 1) * GROUP_SIZE, :], ninf)
         for g in range(N_GROUPS)], axis=0)
    ioe = lax.broadcasted_iota(jnp.int32, (E, TM), 0)
    sel = jnp.zeros((E, TM), F32)
    idx_rows, w_rows = [], []
    for _ in range(TOP_K):
        m = jnp.max(masked, axis=0, keepdims=True)
        idx = jnp.min(jnp.where(masked == m, ioe, E), axis=0, keepdims=True)
        hit = ioe == idx
        idx_rows.append(idx)
        w_rows.append(jnp.sum(jnp.where(hit, scores, 0.0), axis=0, keepdims=True))
        sel = jnp.where(hit, 1.0, sel)
        masked = jnp.where(hit, ninf, masked)
    wsum = w_rows[0]
    for w in w_rows[1:]:
        wsum = wsum + w
    ti = lax.broadcasted_iota(jnp.int32, (TM, TM), 0)
    tj = lax.broadcasted_iota(jnp.int32, (TM, TM), 1)
    before = jnp.where(ti < tj, 1.0, 0.0).astype(BF16)
    rank = _dot(sel.astype(BF16), before) + carry_ref[:, 0:1]
    rank_rows = [jnp.sum(jnp.where(ioe == idx, rank, 0.0), axis=0, keepdims=True) for idx in idx_rows]
    eidx_ref[...] = jnp.concatenate(idx_rows, axis=0)
    wts_ref[...] = jnp.concatenate([w / wsum * ROUTE_SCALE for w in w_rows], axis=0)
    rank_ref[...] = jnp.concatenate(rank_rows, axis=0).astype(jnp.int32)
    carry_ref[...] = carry_ref[...] + jnp.sum(sel, axis=1, keepdims=True)
    cnt_ref[...] = carry_ref[...]


def _router(h, w_router, router_bias):
    T, Dm = h.shape
    TM = _pick_tile(T, 256)
    tok = pl.BlockSpec((TOP_K, TM), lambda i: (0, i))
    return pl.pallas_call(
        functools.partial(_router_kernel, TM=TM),
        grid=(T // TM,),
        in_specs=[pl.BlockSpec((TM, Dm), lambda i: (i, 0)),
                  pl.BlockSpec((N_EXPERTS, Dm), lambda i: (0, 0)),
                  pl.BlockSpec((N_EXPERTS, 1), lambda i: (0, 0))],
        out_specs=[tok, tok, tok, pl.BlockSpec((N_EXPERTS, 128), lambda i: (0, 0))],
        out_shape=[jax.ShapeDtypeStruct((TOP_K, T), jnp.int32), jax.ShapeDtypeStruct((TOP_K, T), F32),
                   jax.ShapeDtypeStruct((TOP_K, T), jnp.int32), jax.ShapeDtypeStruct((N_EXPERTS, 128), F32)],
        scratch_shapes=[pltpu.VMEM((N_EXPERTS, 128), F32)],
        compiler_params=_cp("arbitrary"),
        name="moe_router",
    )(h, w_router.T, router_bias.astype(F32).reshape(N_EXPERTS, 1))


def _dispatch_kernel(dest_hbm, fill_hbm, x_ref, xs_hbm, dest_smem, fill_smem, zrow_ref, sem, isem, fsem,
                     *, TM, nfill):
    i = pl.program_id(0)
    cp = pltpu.make_async_copy(dest_hbm.at[i], dest_smem, isem)
    cp.start()

    @pl.when(i == 0)
    def _():
        zrow_ref[...] = jnp.zeros_like(zrow_ref)
        fc = pltpu.make_async_copy(fill_hbm, fill_smem, fsem)
        fc.start()
        fc.wait()

        def body(r, carry):
            d = fill_smem[r]

            @pl.when(d >= 0)
            def _():
                pltpu.make_async_copy(zrow_ref, xs_hbm.at[pl.ds(d, 1)], sem).start()
            return carry

        lax.fori_loop(0, nfill, body, 0)

        def wbody(r, carry):
            @pl.when(fill_smem[r] >= 0)
            def _():
                pltpu.make_async_copy(zrow_ref, xs_hbm.at[pl.ds(0, 1)], sem).wait()
            return carry

        lax.fori_loop(0, nfill, wbody, 0)

    cp.wait()

    def issue(t, carry):
        for k in range(TOP_K):
            pltpu.make_async_copy(x_ref.at[pl.ds(t, 1)], xs_hbm.at[pl.ds(dest_smem[k, t], 1)], sem).start()
        return carry

    lax.fori_loop(0, TM, issue, 0)

    def drain(t, carry):
        for k in range(TOP_K):
            pltpu.make_async_copy(x_ref.at[pl.ds(0, 1)], xs_hbm.at[pl.ds(0, 1)], sem).wait()
        return carry

    lax.fori_loop(0, TM, drain, 0)


def _dispatch(h, dest_tiles, fill, rows):
    T, Dm = h.shape
    nt, _, TM = dest_tiles.shape
    nfill = fill.shape[0]
    return pl.pallas_call(
        functools.partial(_dispatch_kernel, TM=TM, nfill=nfill),
        grid=(nt,),
        in_specs=[pl.BlockSpec(memory_space=pl.ANY), pl.BlockSpec(memory_space=pl.ANY),
                  pl.BlockSpec((TM, Dm), lambda i: (i, 0))],
        out_specs=pl.BlockSpec(memory_space=pl.ANY),
        out_shape=jax.ShapeDtypeStruct((rows, Dm), F32),
        scratch_shapes=[pltpu.SMEM((TOP_K, TM), jnp.int32), pltpu.SMEM((nfill,), jnp.int32),
                        pltpu.VMEM((1, Dm), F32), pltpu.SemaphoreType.DMA, pltpu.SemaphoreType.DMA,
                        pltpu.SemaphoreType.DMA],
        compiler_params=_cp("arbitrary"),
        name="moe_dispatch",
    )(dest_tiles, fill, h)


def _expert_kernel(be_ref, na_ref, x_ref, wg_ref, wu_ref, wd_ref, o_ref):
    @pl.when(pl.program_id(0) < na_ref[0])
    def _():
        x = x_ref[...].astype(BF16)
        a = _dot(x, wg_ref[0, 0].astype(BF16))
        b = _dot(x, wu_ref[0, 0].astype(BF16))
        hmid = (a * jax.nn.sigmoid(a) * b).astype(BF16)
        o_ref[...] = _dot(hmid, wd_ref[0, 0].astype(BF16))


def _experts(xs, blk_e, nact, w_gate, w_up, w_down, layer):
    rows, Dm = xs.shape
    nblk = rows // MOE_BLOCK

    def row(i, be, na):
        return (jnp.minimum(i, na[0] - 1), 0)

    def wsel(i, be, na):
        return (layer, be[jnp.minimum(i, na[0] - 1)], 0, 0)

    return pl.pallas_call(
        _expert_kernel,
        grid_spec=pltpu.PrefetchScalarGridSpec(
            num_scalar_prefetch=2,
            grid=(nblk,),
            in_specs=[pl.BlockSpec((MOE_BLOCK, Dm), row),
                      pl.BlockSpec((1, 1, Dm, D_EXPERT), wsel),
                      pl.BlockSpec((1, 1, Dm, D_EXPERT), wsel),
                      pl.BlockSpec((1, 1, D_EXPERT, Dm), wsel)],
            out_specs=pl.BlockSpec((MOE_BLOCK, Dm), row),
        ),
        out_shape=jax.ShapeDtypeStruct((rows, Dm), F32),
        compiler_params=_cp("arbitrary"),
        name="moe_experts",
    )(blk_e, nact, xs, w_gate, w_up, w_down)


def _combine_kernel(dest_hbm, w_ref, yb_hbm, o_ref, dest_smem, buf_ref, sem, isem, *, TM):
    i = pl.program_id(0)
    cp = pltpu.make_async_copy(dest_hbm.at[i], dest_smem, isem)
    cp.start()
    cp.wait()

    def issue(t, carry):
        for k in range(TOP_K):
            pltpu.make_async_copy(yb_hbm.at[pl.ds(dest_smem[k, t], 1)], buf_ref.at[k, pl.ds(t, 1)], sem).start()
        return carry

    lax.fori_loop(0, TM, issue, 0)

    def drain(t, carry):
        for k in range(TOP_K):
            pltpu.make_async_copy(yb_hbm.at[pl.ds(0, 1)], buf_ref.at[k, pl.ds(0, 1)], sem).wait()
        return carry

    lax.fori_loop(0, TM, drain, 0)
    w = w_ref[...]
    acc = buf_ref[0] * w[:, 0:1]
    for k in range(1, TOP_K):
        acc = acc + buf_ref[k] * w[:, k:k + 1]
    o_ref[...] = acc


def _combine(yb, dest_tiles, wts_tok):
    nt, _, TM = dest_tiles.shape
    T = nt * TM
    Dm = yb.shape[1]
    return pl.pallas_call(
        functools.partial(_combine_kernel, TM=TM),
        grid=(nt,),
        in_specs=[pl.BlockSpec(memory_space=pl.ANY),
                  pl.BlockSpec((TM, TOP_K), lambda i: (i, 0)),
                  pl.BlockSpec(memory_space=pl.ANY)],
        out_specs=pl.BlockSpec((TM, Dm), lambda i: (i, 0)),
        out_shape=jax.ShapeDtypeStruct((T, Dm), F32),
        scratch_shapes=[pltpu.SMEM((TOP_K, TM), jnp.int32), pltpu.VMEM((TOP_K, TM, Dm), F32),
                        pltpu.SemaphoreType.DMA, pltpu.SemaphoreType.DMA],
        compiler_params=_cp("arbitrary"),
        name="moe_combine",
    )(dest_tiles, wts_tok, yb)


def _shared_kernel(x_ref, wg_ref, wu_ref, wd_ref, o_ref, wgb_ref, wub_ref, wdb_ref):
    @pl.when(pl.program_id(0) == 0)
    def _():
        wgb_ref[...] = wg_ref[...].astype(BF16)
        wub_ref[...] = wu_ref[...].astype(BF16)
        wdb_ref[...] = wd_ref[...].astype(BF16)

    x = x_ref[...]
    a = _dot(x, wgb_ref[...])
    b = _dot(x, wub_ref[...])
    o_ref[...] = _dot((a * jax.nn.sigmoid(a) * b).astype(BF16), wdb_ref[...])


def _shared_expert(hb, wg, wu, wd):
    T, Dm = hb.shape
    Ds = wg.shape[1]
    TM = _pick_tile(T, 512)
    full = lambda r, c: pl.BlockSpec((r, c), lambda i: (0, 0))
    return pl.pallas_call(
        _shared_kernel,
        grid=(T // TM,),
        in_specs=[pl.BlockSpec((TM, Dm), lambda i: (i, 0)), full(Dm, Ds), full(Dm, Ds), full(Ds, Dm)],
        out_specs=pl.BlockSpec((TM, Dm), lambda i: (i, 0)),
        out_shape=jax.ShapeDtypeStruct((T, Dm), F32),
        scratch_shapes=[pltpu.VMEM((Dm, Ds), BF16), pltpu.VMEM((Dm, Ds), BF16), pltpu.VMEM((Ds, Dm), BF16)],
        compiler_params=_cp("arbitrary"),
        name="moe_shared",
    )(hb, wg, wu, wd)


def _moe(h, hb, w_router, router_bias, w_gate, w_up, w_down, layer, ws_gate, ws_up, ws_down):
    T, Dm = h.shape
    eidx, wts, rank, cnt = _router(h, w_router, router_bias)
    counts = cnt[:, 0].astype(jnp.int32)
    pcounts = (counts + MOE_BLOCK - 1) // MOE_BLOCK * MOE_BLOCK
    pends = jnp.cumsum(pcounts)
    pstarts = pends - pcounts
    nblk = -(-T * TOP_K // MOE_BLOCK) + N_EXPERTS
    rows = nblk * MOE_BLOCK
    dest = pstarts[eidx] + rank
    TM = _pick_tile(T, 128)
    dest_tiles = dest.reshape(TOP_K, T // TM, TM).transpose(1, 0, 2)
    blk_e = jnp.minimum(jnp.searchsorted(pends, jnp.arange(nblk, dtype=jnp.int32) * MOE_BLOCK, side="right"),
                        N_EXPERTS - 1).astype(jnp.int32)
    nact = (pends[-1:] // MOE_BLOCK).astype(jnp.int32)
    pad_id = jnp.arange(MOE_BLOCK, dtype=jnp.int32)[None, :]
    fill = jnp.where(pad_id < (pcounts - counts)[:, None], (pstarts + counts)[:, None] + pad_id, -1).reshape(-1)
    xs = _dispatch(h, dest_tiles, fill, rows)
    yb = _experts(xs, blk_e, nact, w_gate, w_up, w_down, layer)
    routed = _combine(yb, dest_tiles, wts.T)
    shared = _shared_expert(hb, ws_gate, ws_up, ws_down)
    return routed, shared


def kernel(x, meta_tokens, ev_w_in, ev_w_out, ev_sink, od_w_in, od_conv_w, od_a_log, od_dt_bias, od_norm_w,
           od_w_out, ln_g, ln_b, w_router, router_bias, w_gate, w_up, w_down, ws_gate, ws_up, ws_down):
    B, S, Dm = x.shape
    Lp = S + FRONT
    meta = jnp.broadcast_to(meta_tokens.astype(x.dtype)[None], (B, N_META, Dm))
    h = jnp.concatenate([jnp.zeros((B, DEAD, Dm), x.dtype), meta, x], axis=1).reshape(B * Lp, Dm)
    hb = h.astype(BF16)
    for layer in range(DEPTH):
        i = layer // 2
        if layer % 2 == 0:
            mix = _even_mixer(hb, B, Lp, ev_w_in[i], ev_w_out[i], ev_sink[i])
        else:
            mix = _gdn_mixer(hb, B, Lp, od_w_in[i], od_conv_w[i], od_a_log[i], od_dt_bias[i],
                             od_norm_w[i], od_w_out[i])
        h, hb = _residual_ln(h, [mix], ln_g[layer, 0], ln_b[layer, 0])
        routed, shared = _moe(h, hb, w_router[layer], router_bias[layer], w_gate, w_up, w_down, layer,
                              ws_gate[layer], ws_up[layer], ws_down[layer])
        h, hb = _residual_ln(h, [routed, shared], ln_g[layer, 1], ln_b[layer, 1])
    return h.reshape(B, Lp, Dm)[:, FRONT:]
```

```python
import functools
import math

import numpy as np
import jax
import jax.numpy as jnp
from jax import lax
from jax.experimental import pallas as pl
from jax.experimental.pallas import tpu as pltpu

F32 = jnp.float32
BF16 = jnp.bfloat16

D_MODEL = 2048
DEPTH = 4
N_META = 16
FRONT = 128
DEAD = FRONT - N_META
RET_HEADS = 8
RET_HD = 128
RET_W = RET_HEADS * RET_HD
RET_CHUNK = 128
ATT_HEADS = 8
ATT_KV_HEADS = 2
ATT_GROUP = ATT_HEADS // ATT_KV_HEADS
ATT_HD = 128
ATT_BLOCK = 128
WINDOW = 128
EVEN_IN = 4 * RET_W + ATT_HEADS * ATT_HD + 2 * ATT_KV_HEADS * ATT_HD
GDN_QK_HEADS = 16
GDN_V_HEADS = 32
GDN_HD = 128
GDN_K_W = GDN_QK_HEADS * GDN_HD
GDN_V_W = GDN_V_HEADS * GDN_HD
GDN_CONV_CH = 2 * GDN_K_W + GDN_V_W
GDN_CHUNK = 64
GDN_HG = 8
CONV_WIDTH = 5
N_EXPERTS = 64
TOP_K = 8
N_GROUPS = 8
GROUP_SIZE = N_EXPERTS // N_GROUPS
TOPK_GROUPS = 4
D_EXPERT = 384
ROUTE_SCALE = 2.5
MOE_BLOCK = 256
DN_ALPHA = (2 * DEPTH) ** 0.25
LN_EPS = 1e-5
NORM_EPS = 1e-6
NEG = -1e30

VMEM_LIMIT = 56 * 2**20


def _cp(*sem, vmem=VMEM_LIMIT):
    return pltpu.CompilerParams(dimension_semantics=sem, vmem_limit_bytes=vmem)


def _dot(a, b):
    return jnp.dot(a, b, preferred_element_type=F32)


def _dot_nt(a, b):
    return lax.dot_general(a, b, (((1,), (1,)), ((), ())), preferred_element_type=F32)


def _dot_tn(a, b):
    return lax.dot_general(a, b, (((0,), (0,)), ((), ())), preferred_element_type=F32)


def _split3(a):
    hi = a.astype(BF16)
    r1 = a - hi.astype(F32)
    mid = r1.astype(BF16)
    lo = (r1 - mid.astype(F32)).astype(BF16)
    return hi, mid, lo


def _pick_tile(n, cap):
    for t in range(min(cap, n), 0, -1):
        if n % t == 0 and t % 8 == 0:
            return t
    return n


def _mm_kernel(*refs, ksplits):
    nx = len(ksplits)
    x_refs, w_ref, o_ref, wb_ref = refs[:nx], refs[nx], refs[nx + 1], refs[nx + 2]

    @pl.when(pl.program_id(1) == 0)
    def _():
        wb_ref[...] = w_ref[...].astype(BF16)

    acc = None
    k0 = 0
    for x_ref, kk in zip(x_refs, ksplits):
        part = _dot(x_ref[...].astype(BF16), wb_ref[k0:k0 + kk, :])
        acc = part if acc is None else acc + part
        k0 += kk
    o_ref[...] = acc.astype(o_ref.dtype)


def _matmul(xs, w, col0, ncols, tn, out_dtype=F32, tm_cap=512):
    M = xs[0].shape[0]
    ksplits = tuple(x.shape[1] for x in xs)
    K = sum(ksplits)
    assert w.shape[0] == K and ncols % tn == 0 and col0 % tn == 0
    tm = _pick_tile(M, tm_cap)
    in_specs = [pl.BlockSpec((tm, kk), lambda j, i: (i, 0)) for kk in ksplits]
    in_specs.append(pl.BlockSpec((K, tn), lambda j, i: (0, col0 // tn + j)))
    return pl.pallas_call(
        functools.partial(_mm_kernel, ksplits=ksplits),
        grid=(ncols // tn, M // tm),
        in_specs=in_specs,
        out_specs=pl.BlockSpec((tm, tn), lambda j, i: (i, j)),
        out_shape=jax.ShapeDtypeStruct((M, ncols), out_dtype),
        scratch_shapes=[pltpu.VMEM((K, tn), BF16)],
        compiler_params=_cp("arbitrary", "arbitrary"),
        name="matmul",
    )(*xs, w)


def _ln_kernel(*refs, n_add):
    h_ref = refs[0]
    add_refs = refs[1:1 + n_add]
    g_ref, b_ref, o_ref, ob_ref = refs[1 + n_add:]
    s = add_refs[0][...]
    for a in add_refs[1:]:
        s = s + a[...]
    y = DN_ALPHA * h_ref[...] + s
    mu = jnp.mean(y, axis=-1, keepdims=True)
    d = y - mu
    var = jnp.mean(d * d, axis=-1, keepdims=True)
    out = d * lax.rsqrt(var + LN_EPS) * g_ref[...] + b_ref[...]
    o_ref[...] = out
    ob_ref[...] = out.astype(BF16)


def _residual_ln(h, adds, g, b):
    M, Dm = h.shape
    tm = _pick_tile(M, 256)
    row = pl.BlockSpec((tm, Dm), lambda i: (i, 0))
    vec = pl.BlockSpec((1, Dm), lambda i: (0, 0))
    return pl.pallas_call(
        functools.partial(_ln_kernel, n_add=len(adds)),
        grid=(M // tm,),
        in_specs=[row] * (1 + len(adds)) + [vec, vec],
        out_specs=[row, row],
        out_shape=[jax.ShapeDtypeStruct((M, Dm), F32), jax.ShapeDtypeStruct((M, Dm), BF16)],
        compiler_params=_cp("arbitrary"),
        name="residual_ln",
    )(h, *adds, g.reshape(1, Dm), b.reshape(1, Dm))


def _ret_tables(C):
    hh = np.arange(RET_HEADS, dtype=np.float64)
    lg = np.log(1.0 - 2.0 ** (-5.0 - hh))[:, None]
    pos = np.arange(C, dtype=np.float64)[None, :]
    vecs = np.stack([np.exp(lg * (pos + 1.0)),
                     np.exp(lg * (C - pos)),
                     np.exp(lg * (C - 1.0 - pos)),
                     np.exp(lg * pos),
                     np.exp(lg * C) * np.ones_like(pos)], axis=1)
    tab = np.broadcast_to(vecs[..., None], (RET_HEADS, 5, C, RET_HD))
    rel = np.abs(pos.T - pos)
    dsym = np.exp(lg[:, :, None] * rel[None])
    return jnp.asarray(tab, F32), jnp.asarray(dsym, F32)


def _ret_kernel(q_ref, k_ref, v_ref, g_ref, tab_ref, d_ref, o_ref, fst_ref, run_ref, *, nc, C):
    s = pl.program_id(1)
    fwd = s < nc
    c = jnp.where(fwd, s, 2 * nc - 1 - s)
    row = c * C + lax.broadcasted_iota(jnp.int32, (C, 1), 0)
    live = row >= DEAD
    heads = range(RET_HEADS)
    HD = RET_HD
    ks = [jnp.where(live, k_ref[0, :, h * HD:(h + 1) * HD] * (RET_HD ** -0.5), 0.0) for h in heads]
    vbs = [jnp.where(live, v_ref[0, :, h * HD:(h + 1) * HD], 0.0).astype(BF16) for h in heads]

    @pl.when(jnp.logical_or(s == 0, s == nc))
    def _():
        run_ref[...] = jnp.zeros_like(run_ref)

    @pl.when(fwd)
    def _():
        for h in heads:
            fst_ref[c, h] = run_ref[h].astype(BF16)
        upd = [_dot_tn((ks[h] * tab_ref[h, 2]).astype(BF16), vbs[h]) for h in heads]
        for h in heads:
            run_ref[h] = run_ref[h] * tab_ref[h, 4] + upd[h]

    @pl.when(jnp.logical_not(fwd))
    def _():
        qs = [q_ref[0, :, h * HD:(h + 1) * HD] for h in heads]
        sc = [(_dot_nt(qs[h].astype(BF16), ks[h].astype(BF16)) * d_ref[h]).astype(BF16) for h in heads]
        left = [_dot((qs[h] * tab_ref[h, 0]).astype(BF16), fst_ref[c, h]) for h in heads]
        right = [_dot((qs[h] * tab_ref[h, 1]).astype(BF16), run_ref[h].astype(BF16)) for h in heads]
        intra = [_dot(sc[h], vbs[h]) for h in heads]
        upd = [_dot_tn((ks[h] * tab_ref[h, 3]).astype(BF16), vbs[h]) for h in heads]
        for h in heads:
            run_ref[h] = run_ref[h] * tab_ref[h, 4] + upd[h]
            out = intra[h] + left[h] + right[h]
            mu = jnp.mean(out, axis=-1, keepdims=True)
            dlt = out - mu
            var = jnp.mean(dlt * dlt, axis=-1, keepdims=True)
            normed = dlt * lax.rsqrt(var + NORM_EPS)
            g = g_ref[0, :, h * HD:(h + 1) * HD]
            o_ref[0, :, h * HD:(h + 1) * HD] = (g * jax.nn.sigmoid(g) * normed).astype(o_ref.dtype)


def _retention(proj):
    B, Lp, _ = proj.shape
    C = RET_CHUNK
    nc = Lp // C
    tab, dsym = _ret_tables(C)

    def cidx(s):
        return jnp.where(s < nc, s, 2 * nc - 1 - s)

    def cidx_out(s):
        return jnp.where(s < nc, nc - 1, 2 * nc - 1 - s)

    return pl.pallas_call(
        functools.partial(_ret_kernel, nc=nc, C=C),
        grid=(B, 2 * nc),
        in_specs=[
            pl.BlockSpec((1, C, RET_W), lambda b, s: (b, cidx_out(s), 0)),
            pl.BlockSpec((1, C, RET_W), lambda b, s: (b, cidx(s), 1)),
            pl.BlockSpec((1, C, RET_W), lambda b, s: (b, cidx(s), 2)),
            pl.BlockSpec((1, C, RET_W), lambda b, s: (b, cidx_out(s), 3)),
            pl.BlockSpec((RET_HEADS, 5, C, RET_HD), lambda b, s: (0, 0, 0, 0)),
            pl.BlockSpec((RET_HEADS, C, C), lambda b, s: (0, 0, 0)),
        ],
        out_specs=pl.BlockSpec((1, C, RET_W), lambda b, s: (b, cidx_out(s), 0)),
        out_shape=jax.ShapeDtypeStruct((B, Lp, RET_W), BF16),
        scratch_shapes=[pltpu.VMEM((nc, RET_HEADS, RET_HD, RET_HD), BF16),
                        pltpu.VMEM((RET_HEADS, RET_HD, RET_HD), F32)],
        compiler_params=_cp("arbitrary", "arbitrary"),
        name="retention",
    )(proj, proj, proj, proj, tab, dsym)


def _att_kernel(sink_ref, slope_ref, q_ref, km_ref, kp_ref, kc_ref, kn_ref,
                vm_ref, vp_ref, vc_ref, vn_ref, o_ref, *, nb):
    j = pl.program_id(1)
    qb = pl.program_id(2)
    T = ATT_BLOCK
    ii = lax.broadcasted_iota(jnp.int32, (T, T), 0)
    jj = lax.broadcasted_iota(jnp.int32, (T, T), 1)
    meta_ok = jj >= DEAD
    pieces = []
    for off, k_ref, v_ref, ok in ((-T, kp_ref, vp_ref, qb >= 2),
                                  (0, kc_ref, vc_ref, qb >= 1),
                                  (T, kn_ref, vn_ref, qb + 1 <= nb)):
        dist = jnp.abs(jj + off - ii)
        pieces.append((dist.astype(F32), dist <= WINDOW, k_ref[0].astype(BF16), v_ref[0].astype(BF16), ok))
    km = km_ref[0].astype(BF16)
    vm = vm_ref[0].astype(BF16)
    vals = [vm] + [p[3] for p in pieces]
    groups = range(ATT_GROUP)
    qs = [(q_ref[0, :, g * ATT_HD:(g + 1) * ATT_HD] * (ATT_HD ** -0.5)).astype(BF16) for g in groups]
    raw = [[_dot_nt(qs[g], km)] + [_dot_nt(qs[g], p[2]) for p in pieces] for g in groups]
    es, dens = [], []
    for g in groups:
        slope = slope_ref[j, g]
        sink = sink_ref[j, g]
        s_list = [jnp.where(meta_ok, raw[g][0], NEG)]
        for (dist, inwin, _, _, ok), r in zip(pieces, raw[g][1:]):
            s_list.append(jnp.where(ok, jnp.where(inwin, r - slope * dist, NEG), NEG))
        m = jnp.full((T, 1), sink, F32)
        for sb in s_list:
            m = jnp.maximum(m, jnp.max(sb, axis=-1, keepdims=True))
        e_list = [jnp.exp(sb - m) for sb in s_list]
        den = jnp.exp(sink - m)
        for e in e_list:
            den = den + jnp.sum(e, axis=-1, keepdims=True)
        es.append([e.astype(BF16) for e in e_list])
        dens.append(den)
    pv = [[_dot(e, vb) for e, vb in zip(es[g], vals)] for g in groups]
    for g in groups:
        acc = pv[g][0] + pv[g][1] + pv[g][2] + pv[g][3]
        o_ref[0, :, g * ATT_HD:(g + 1) * ATT_HD] = (acc / dens[g]).astype(o_ref.dtype)


def _window_attention(proj, sink):
    B, Lp, _ = proj.shape
    nb = Lp // ATT_BLOCK - 1
    gw = ATT_GROUP * ATT_HD
    q0 = 4 * RET_W // gw
    k0 = (4 * RET_W + ATT_HEADS * ATT_HD) // ATT_HD
    v0 = k0 + ATT_KV_HEADS
    slopes = np.asarray(2.0 ** (-8.0 * (np.arange(ATT_HEADS) + 1.0) / ATT_HEADS), np.float32)
    smem = pl.BlockSpec(memory_space=pltpu.SMEM)

    def kv(c0, f):
        return pl.BlockSpec((1, ATT_BLOCK, ATT_HD), lambda b, j, t: (b, f(t), c0 + j))

    rows = [lambda t: 0, lambda t: jnp.maximum(t - 1, 0), lambda t: t, lambda t: jnp.minimum(t + 1, nb)]
    return pl.pallas_call(
        functools.partial(_att_kernel, nb=nb),
        grid=(B, ATT_KV_HEADS, nb + 1),
        in_specs=[smem, smem, pl.BlockSpec((1, ATT_BLOCK, gw), lambda b, j, t: (b, t, q0 + j))]
        + [kv(k0, f) for f in rows] + [kv(v0, f) for f in rows],
        out_specs=pl.BlockSpec((1, ATT_BLOCK, gw), lambda b, j, t: (b, t, j)),
        out_shape=jax.ShapeDtypeStruct((B, Lp, ATT_HEADS * ATT_HD), BF16),
        compiler_params=_cp("arbitrary", "arbitrary", "arbitrary"),
        name="window_attention",
    )(sink.astype(F32).reshape(ATT_KV_HEADS, ATT_GROUP), jnp.asarray(slopes).reshape(ATT_KV_HEADS, ATT_GROUP),
      *([proj] * 9))


def _conv_kernel(xp_ref, x_ref, xn_ref, w_ref, o_ref, *, mode, TT, nt):
    t = pl.program_id(1)
    rows = t * TT + lax.broadcasted_iota(jnp.int32, (TT, 1), 0)
    live = rows >= DEAD
    x = jnp.where(live, x_ref[0], 0.0)
    prow = t * TT - 8 + lax.broadcasted_iota(jnp.int32, (8, 1), 0)
    prev = jnp.where(prow >= DEAD, xp_ref[0], 0.0)
    nrow = (t + 1) * TT + lax.broadcasted_iota(jnp.int32, (8, 1), 0)
    nxt = jnp.where(jnp.logical_and(nrow >= DEAD, t < nt - 1), xn_ref[0], 0.0)
    xx = jnp.concatenate([prev, x, nxt], axis=0)
    w = w_ref[...]
    half = CONV_WIDTH // 2
    acc = None
    for tap in range(CONV_WIDTH):
        term = xx[8 - half + tap:8 - half + tap + TT, :] * w[tap:tap + 1, :]
        acc = term if acc is None else acc + term
    y = acc * jax.nn.sigmoid(acc)
    if mode in ("q", "k"):
        scale = GDN_HD ** -0.5 if mode == "q" else 1.0
        for a in range(y.shape[1] // GDN_HD):
            ya = y[:, a * GDN_HD:(a + 1) * GDN_HD]
            ya = ya * lax.rsqrt(jnp.sum(ya * ya, axis=-1, keepdims=True) + NORM_EPS)
            if mode == "q":
                ya = ya * scale
            o_ref[0, :, a * GDN_HD:(a + 1) * GDN_HD] = jnp.where(live, ya, 0.0).astype(o_ref.dtype)
    else:
        o_ref[0] = jnp.where(live, y, 0.0).astype(o_ref.dtype)


def _short_conv(qkv, conv_w, mode, col0, ncols):
    B, Lp, _ = qkv.shape
    TT = _pick_tile(Lp, 256)
    TC = 512
    nt = Lp // TT
    c0 = col0 // TC
    return pl.pallas_call(
        functools.partial(_conv_kernel, mode=mode, TT=TT, nt=nt),
        grid=(B, nt, ncols // TC),
        in_specs=[
            pl.BlockSpec((1, 8, TC), lambda b, t, c: (b, jnp.maximum(t * (TT // 8) - 1, 0), c0 + c)),
            pl.BlockSpec((1, TT, TC), lambda b, t, c: (b, t, c0 + c)),
            pl.BlockSpec((1, 8, TC), lambda b, t, c: (b, jnp.minimum((t + 1) * (TT // 8), Lp // 8 - 1), c0 + c)),
            pl.BlockSpec((CONV_WIDTH, TC), lambda b, t, c: (0, c0 + c)),
        ],
        out_specs=pl.BlockSpec((1, TT, TC), lambda b, t, c: (b, t, c)),
        out_shape=jax.ShapeDtypeStruct((B, Lp, ncols), BF16),
        compiler_params=_cp("arbitrary", "arbitrary", "arbitrary"),
        name="short_conv_" + mode,
    )(qkv, qkv, qkv, conv_w)


def _gate_kernel(ba_ref, alog_ref, dtb_ref, o_ref, *, TT):
    t = pl.program_id(1)
    rows = t * TT + lax.broadcasted_iota(jnp.int32, (TT, 1), 0)
    lane = lax.broadcasted_iota(jnp.int32, (1, 4 * GDN_V_HEADS), 1)
    is_g = (lane // GDN_V_HEADS) % 2 == 1
    x = ba_ref[0]
    beta = jax.nn.sigmoid(x)
    xs = x + dtb_ref[...]
    softplus = jnp.maximum(xs, 0.0) + jnp.log(1.0 + jnp.exp(-jnp.abs(xs)))
    g = -jnp.exp(alog_ref[...]) * softplus
    o_ref[0] = jnp.where(rows >= DEAD, jnp.where(is_g, g, beta), 0.0)


def _gates(ba, a_log, dt_bias):
    B, Lp, W = ba.shape
    TT = _pick_tile(Lp, 1024)
    zeros = jnp.zeros((2, 1, GDN_V_HEADS), F32)
    alog = jnp.concatenate([zeros, a_log.astype(F32)[:, None, :]], axis=1).reshape(1, W)
    dtb = jnp.concatenate([zeros, dt_bias.astype(F32)[:, None, :]], axis=1).reshape(1, W)
    vec = pl.BlockSpec((1, W), lambda b, t: (0, 0))
    return pl.pallas_call(
        functools.partial(_gate_kernel, TT=TT),
        grid=(B, Lp // TT),
        in_specs=[pl.BlockSpec((1, TT, W), lambda b, t: (b, t, 0)), vec, vec],
        out_specs=pl.BlockSpec((1, TT, W), lambda b, t: (b, t, 0)),
        out_shape=jax.ShapeDtypeStruct((B, Lp, W), F32),
        compiler_params=_cp("arbitrary", "arbitrary"),
        name="gdn_gates",
    )(ba, alog, dtb)


def _gdn_kernel(*refs, rev, final):
    if final:
        q_ref, k_ref, v_ref, gc_ref, gr_ref, of_ref, z_ref, nw_ref, o_ref, s_ref = refs
    else:
        q_ref, k_ref, v_ref, gc_ref, gr_ref, o_ref, s_ref = refs
    C = GDN_CHUNK
    HG = GDN_HG

    @pl.when(pl.program_id(2) == 0)
    def _():
        s_ref[...] = jnp.zeros_like(s_ref)

    ii = lax.broadcasted_iota(jnp.int32, (C, C), 0)
    jj = lax.broadcasted_iota(jnp.int32, (C, C), 1)
    incl = (jj >= ii) if rev else (jj <= ii)
    strict = (jj > ii) if rev else (jj < ii)
    tri_col = jnp.where(incl, 1.0, 0.0).astype(BF16)
    tri_row = jnp.where((ii >= jj) if rev else (ii <= jj), 1.0, 0.0).astype(BF16)
    gcol = gc_ref[0, 0]
    grow = gr_ref[0, 0, 0]
    gc_col = sum(_dot(tri_col, p) for p in _split3(gcol))
    gc_row = sum(_dot(p, tri_row) for p in _split3(grow))
    last = 0 if rev else C - 1
    d0 = 2 * HG if rev else 0
    HD = GDN_HD
    heads = range(HG)
    pairs = range(HG // 2)
    qs = [q_ref[0, :, p * HD:(p + 1) * HD] for p in pairs]
    ks = [k_ref[0, :, p * HD:(p + 1) * HD] for p in pairs]
    kfs = [k.astype(F32) for k in ks]
    gram = [_dot_nt(ks[p], ks[p]) for p in pairs]
    qk = [_dot_nt(qs[p], ks[p]) for p in pairs]
    beta = [gcol[:, d0 + h:d0 + h + 1] for h in heads]
    gcc = [gc_col[:, d0 + HG + h:d0 + HG + h + 1] for h in heads]
    gcr = [gc_row[d0 + HG + h:d0 + HG + h + 1, :] for h in heads]
    gtot = [g[last:last + 1, :] for g in gcc]
    decay = [jnp.where(incl, jnp.exp(jnp.where(incl, gcc[h] - gcr[h], 0.0)), 0.0) for h in heads]
    eg = [jnp.exp(g) for g in gcc]
    ys = [-jnp.where(strict, beta[h] * gram[h // 2] * decay[h], 0.0) for h in heads]
    ms = list(ys)
    for _ in range(5):
        ybs = [y.astype(BF16) for y in ys]
        ys = [_dot(yb, yb) for yb in ybs]
        mys = [_dot(ms[h].astype(BF16), ys[h].astype(BF16)) for h in heads]
        ms = [ms[h] + ys[h] + mys[h] for h in heads]
    rhs = [jnp.concatenate([v_ref[0, :, h * HD:(h + 1) * HD].astype(F32) * beta[h],
                            kfs[h // 2] * (beta[h] * eg[h])], axis=1) for h in heads]
    sol = [rhs[h] + _dot(ms[h].astype(BF16), rhs[h].astype(BF16)) for h in heads]
    st = [s_ref[h] for h in heads]
    stb = [s.astype(BF16) for s in st]
    lhs = [jnp.concatenate([sol[h][:, HD:].astype(BF16), (qs[h // 2].astype(F32) * eg[h]).astype(BF16)], axis=0)
           for h in heads]
    ws = [_dot(lhs[h], stb[h]) for h in heads]
    vnb = [(sol[h][:, :HD] - ws[h][:C]).astype(BF16) for h in heads]
    outs = [ws[h][C:] + _dot((qk[h // 2] * decay[h]).astype(BF16), vnb[h]) for h in heads]
    upd = [_dot_tn((kfs[h // 2] * jnp.exp(gtot[h] - gcc[h])).astype(BF16), vnb[h]) for h in heads]
    for h in heads:
        s_ref[h] = st[h] * jnp.exp(gtot[h]) + upd[h]
        sl = slice(h * HD, (h + 1) * HD)
        if final:
            o = outs[h] + of_ref[0, :, sl]
            z = z_ref[0, :, sl]
            o = o * lax.rsqrt(jnp.mean(o * o, axis=-1, keepdims=True) + NORM_EPS) * nw_ref[...]
            o_ref[0, :, sl] = (o * (z * jax.nn.sigmoid(z))).astype(o_ref.dtype)
        else:
            o_ref[0, :, sl] = outs[h]


def _gdn_dir(q, k, v, gcol, grow, rev, o_fwd=None, z=None, norm_w=None):
    B, Lp, _ = v.shape
    C = GDN_CHUNK
    nc = Lp // C
    ng = GDN_V_HEADS // GDN_HG
    qw = GDN_HG // 2 * GDN_HD
    vw = GDN_HG * GDN_HD
    final = o_fwd is not None

    def cc(c):
        return nc - 1 - c if rev else c

    in_specs = [
        pl.BlockSpec((1, C, qw), lambda b, g, c: (b, cc(c), g)),
        pl.BlockSpec((1, C, qw), lambda b, g, c: (b, cc(c), g)),
        pl.BlockSpec((1, C, vw), lambda b, g, c: (b, cc(c), g)),
        pl.BlockSpec((1, 1, C, 4 * GDN_HG), lambda b, g, c: (b, g, cc(c), 0)),
        pl.BlockSpec((1, 1, 1, 4 * GDN_HG, C), lambda b, g, c: (b, g, cc(c), 0, 0)),
    ]
    args = [q, k, v, gcol, grow]
    if final:
        in_specs += [pl.BlockSpec((1, C, vw), lambda b, g, c: (b, cc(c), g)),
                     pl.BlockSpec((1, C, vw), lambda b, g, c: (b, cc(c), g)),
                     pl.BlockSpec((1, GDN_HD), lambda b, g, c: (0, 0))]
        args += [o_fwd, z, norm_w.astype(F32).reshape(1, GDN_HD)]
    return pl.pallas_call(
        functools.partial(_gdn_kernel, rev=rev, final=final),
        grid=(B, ng, nc),
        in_specs=in_specs,
        out_specs=pl.BlockSpec((1, C, vw), lambda b, g, c: (b, cc(c), g)),
        out_shape=jax.ShapeDtypeStruct((B, Lp, GDN_V_W), BF16 if final else F32),
        scratch_shapes=[pltpu.VMEM((GDN_HG, GDN_HD, GDN_HD), F32)],
        compiler_params=_cp("arbitrary", "arbitrary", "arbitrary"),
        name="gdn_bwd" if rev else "gdn_fwd",
    )(*args)


def _gdn_mixer(hb, B, Lp, w_in, conv_w, a_log, dt_bias, norm_w, w_out):
    qkv = _matmul([hb], w_in, 0, GDN_CONV_CH, 1024).reshape(B, Lp, GDN_CONV_CH)
    z = _matmul([hb], w_in, GDN_CONV_CH, GDN_V_W, 1024).reshape(B, Lp, GDN_V_W)
    ba = _matmul([hb], w_in, GDN_CONV_CH + GDN_V_W, 4 * GDN_V_HEADS, 128).reshape(B, Lp, 4 * GDN_V_HEADS)
    q = _short_conv(qkv, conv_w, "q", 0, GDN_K_W)
    k = _short_conv(qkv, conv_w, "k", GDN_K_W, GDN_K_W)
    v = _short_conv(qkv, conv_w, "v", 2 * GDN_K_W, GDN_V_W)
    gb = _gates(ba, a_log, dt_bias)
    ng = GDN_V_HEADS // GDN_HG
    nc = Lp // GDN_CHUNK
    gcol = gb.reshape(B, Lp, 4, ng, GDN_HG).transpose(0, 3, 1, 2, 4).reshape(B, ng, Lp, 4 * GDN_HG)
    grow = gcol.reshape(B, ng, nc, GDN_CHUNK, 4 * GDN_HG).transpose(0, 1, 2, 4, 3)
    o_f = _gdn_dir(q, k, v, gcol, grow, rev=False)
    o = _gdn_dir(q, k, v, gcol, grow, rev=True, o_fwd=o_f, z=z, norm_w=norm_w)
    return _matmul([o.reshape(B * Lp, GDN_V_W)], w_out, 0, D_MODEL, 512)


def _even_mixer(hb, B, Lp, w_in, w_out, sink):
    proj = _matmul([hb], w_in, 0, EVEN_IN, 512).reshape(B, Lp, EVEN_IN)
    ret = _retention(proj).reshape(B * Lp, RET_W)
    att = _window_attention(proj, sink).reshape(B * Lp, ATT_HEADS * ATT_HD)
    return _matmul([ret, att], w_out, 0, D_MODEL, 1024)


def _router_kernel(x_ref, wt_ref, bias_ref, eidx_ref, wts_ref, rank_ref, cnt_ref, carry_ref, *, TM):
    E = N_EXPERTS

    @pl.when(pl.program_id(0) == 0)
    def _():
        carry_ref[...] = jnp.zeros_like(carry_ref)

    xh, xm, xl = _split3(x_ref[...])
    wh, wm, wl = _split3(wt_ref[...])
    logits = (_dot_nt(wh, xh) + (_dot_nt(wh, xm) + _dot_nt(wm, xh))
              + (_dot_nt(wh, xl) + _dot_nt(wl, xh) + _dot_nt(wm, xm)))
    scores = jax.nn.sigmoid(logits)
    choice = scores + bias_ref[...]
    ninf = -jnp.inf
    io8 = lax.broadcasted_iota(jnp.int32, (GROUP_SIZE, TM), 0)
    gs_rows = []
    for g in range(N_GROUPS):
        cg = choice[g * GROUP_SIZE:(g + 1) * GROUP_SIZE, :]
        m1 = jnp.max(cg, axis=0, keepdims=True)
        i1 = jnp.min(jnp.where(cg == m1, io8, GROUP_SIZE), axis=0, keepdims=True)
        m2 = jnp.max(jnp.where(io8 == i1, ninf, cg), axis=0, keepdims=True)
        gs_rows.append(m1 + m2)
    gs = jnp.concatenate(gs_rows, axis=0)
    gsel = jnp.zeros((N_GROUPS, TM), jnp.int32)
    for _ in range(TOPK_GROUPS):
        m = jnp.max(gs, axis=0, keepdims=True)
        idx = jnp.min(jnp.where(gs == m, io8, N_GROUPS), axis=0, keepdims=True)
        hit = io8 == idx
        gsel = jnp.where(hit, 1, gsel)
        gs = jnp.where(hit, ninf, gs)
    masked = jnp.concatenate(
        [jnp.where(gsel[g:g + 1, :] > 0, choice[g * GROUP_SIZE:(g + 1) * GROUP_SIZE, :], ninf)
         for g in range(N_GROUPS)], axis=0)
    ioe = lax.broadcasted_iota(jnp.int32, (E, TM), 0)
    sel = jnp.zeros((E, TM), F32)
    idx_rows, w_rows = [], []
    for _ in range(TOP_K):
        m = jnp.max(masked, axis=0, keepdims=True)
        idx = jnp.min(jnp.where(masked == m, ioe, E), axis=0, keepdims=True)
        hit = ioe == idx
        idx_rows.append(idx)
        w_rows.append(jnp.sum(jnp.where(hit, scores, 0.0), axis=0, keepdims=True))
        sel = jnp.where(hit, 1.0, sel)
        masked = jnp.where(hit, ninf, masked)
    wsum = w_rows[0]
    for w in w_rows[1:]:
        wsum = wsum + w
    ti = lax.broadcasted_iota(jnp.int32, (TM, TM), 0)
    tj = lax.broadcasted_iota(jnp.int32, (TM, TM), 1)
    before = jnp.where(ti < tj, 1.0, 0.0).astype(BF16)
    rank = _dot(sel.astype(BF16), before) + carry_ref[:, 0:1]
    rank_rows = [jnp.sum(jnp.where(ioe == idx, rank, 0.0), axis=0, keepdims=True) for idx in idx_rows]
    eidx_ref[...] = jnp.concatenate(idx_rows, axis=0)
    wts_ref[...] = jnp.concatenate([w / wsum * ROUTE_SCALE for w in w_rows], axis=0)
    rank_ref[...] = jnp.concatenate(rank_rows, axis=0).astype(jnp.int32)
    carry_ref[...] = carry_ref[...] + jnp.sum(sel, axis=1, keepdims=True)
    cnt_ref[...] = carry_ref[...]


def _router(h, w_router, router_bias):
    T, Dm = h.shape
    TM = _pick_tile(T, 256)
    tok = pl.BlockSpec((TOP_K, TM), lambda i: (0, i))
    return pl.pallas_call(
        functools.partial(_router_kernel, TM=TM),
        grid=(T // TM,),
        in_specs=[pl.BlockSpec((TM, Dm), lambda i: (i, 0)),
                  pl.BlockSpec((N_EXPERTS, Dm), lambda i: (0, 0)),
                  pl.BlockSpec((N_EXPERTS, 1), lambda i: (0, 0))],
        out_specs=[tok, tok, tok, pl.BlockSpec((N_EXPERTS, 128), lambda i: (0, 0))],
        out_shape=[jax.ShapeDtypeStruct((TOP_K, T), jnp.int32), jax.ShapeDtypeStruct((TOP_K, T), F32),
                   jax.ShapeDtypeStruct((TOP_K, T), jnp.int32), jax.ShapeDtypeStruct((N_EXPERTS, 128), F32)],
        scratch_shapes=[pltpu.VMEM((N_EXPERTS, 128), F32)],
        compiler_params=_cp("arbitrary"),
        name="moe_router",
    )(h, w_router.T, router_bias.astype(F32).reshape(N_EXPERTS, 1))


def _dispatch_kernel(dest_hbm, fill_hbm, x_ref, xs_hbm, dest_smem, fill_smem, zrow_ref, sem, isem, fsem,
                     *, TM, nfill):
    i = pl.program_id(0)
    cp = pltpu.make_async_copy(dest_hbm.at[i], dest_smem, isem)
    cp.start()

    @pl.when(i == 0)
    def _():
        zrow_ref[...] = jnp.zeros_like(zrow_ref)
        fc = pltpu.make_async_copy(fill_hbm, fill_smem, fsem)
        fc.start()
        fc.wait()

        def body(r, carry):
            d = fill_smem[r]

            @pl.when(d >= 0)
            def _():
                pltpu.make_async_copy(zrow_ref, xs_hbm.at[pl.ds(d, 1)], sem).start()
            return carry

        lax.fori_loop(0, nfill, body, 0)

        def wbody(r, carry):
            @pl.when(fill_smem[r] >= 0)
            def _():
                pltpu.make_async_copy(zrow_ref, xs_hbm.at[pl.ds(0, 1)], sem).wait()
            return carry

        lax.fori_loop(0, nfill, wbody, 0)

    cp.wait()

    def issue(t, carry):
        for k in range(TOP_K):
            pltpu.make_async_copy(x_ref.at[pl.ds(t, 1)], xs_hbm.at[pl.ds(dest_smem[k, t], 1)],
                                  sem).start(priority=k % 2)
        return carry

    lax.fori_loop(0, TM, issue, 0)

    def drain(t, carry):
        for k in range(TOP_K):
            pltpu.make_async_copy(x_ref.at[pl.ds(0, 1)], xs_hbm.at[pl.ds(0, 1)], sem).wait()
        return carry

    lax.fori_loop(0, TM, drain, 0)


def _dispatch(h, dest_tiles, fill, rows):
    T, Dm = h.shape
    nt, _, TM = dest_tiles.shape
    nfill = fill.shape[0]
    return pl.pallas_call(
        functools.partial(_dispatch_kernel, TM=TM, nfill=nfill),
        grid=(nt,),
        in_specs=[pl.BlockSpec(memory_space=pl.ANY), pl.BlockSpec(memory_space=pl.ANY),
                  pl.BlockSpec((TM, Dm), lambda i: (i, 0))],
        out_specs=pl.BlockSpec(memory_space=pl.ANY),
        out_shape=jax.ShapeDtypeStruct((rows, Dm), F32),
        scratch_shapes=[pltpu.SMEM((TOP_K, TM), jnp.int32), pltpu.SMEM((nfill,), jnp.int32),
                        pltpu.VMEM((1, Dm), F32), pltpu.SemaphoreType.DMA, pltpu.SemaphoreType.DMA,
                        pltpu.SemaphoreType.DMA],
        compiler_params=_cp("arbitrary"),
        name="moe_dispatch",
    )(dest_tiles, fill, h)


def _expert_kernel(be_ref, na_ref, x_ref, wg_ref, wu_ref, wd_ref, o_ref, wgb_ref, wub_ref, wdb_ref):
    i = pl.program_id(0)
    active = i < na_ref[0]
    new_expert = jnp.logical_or(i == 0, be_ref[i] != be_ref[jnp.maximum(i - 1, 0)])

    @pl.when(jnp.logical_and(active, new_expert))
    def _():
        wgb_ref[...] = wg_ref[0, 0].astype(BF16)
        wub_ref[...] = wu_ref[0, 0].astype(BF16)
        wdb_ref[...] = wd_ref[0, 0].astype(BF16)

    @pl.when(active)
    def _():
        x = x_ref[...].astype(BF16)
        a = _dot(x, wgb_ref[...])
        b = _dot(x, wub_ref[...])
        hmid = (a * jax.nn.sigmoid(a) * b).astype(BF16)
        o_ref[...] = _dot(hmid, wdb_ref[...])


def _experts(xs, blk_e, nact, w_gate, w_up, w_down, layer):
    rows, Dm = xs.shape
    nblk = rows // MOE_BLOCK

    def row(i, be, na):
        return (jnp.minimum(i, na[0] - 1), 0)

    def wsel(i, be, na):
        return (layer, be[jnp.minimum(i, na[0] - 1)], 0, 0)

    return pl.pallas_call(
        _expert_kernel,
        grid_spec=pltpu.PrefetchScalarGridSpec(
            num_scalar_prefetch=2,
            grid=(nblk,),
            in_specs=[pl.BlockSpec((MOE_BLOCK, Dm), row),
                      pl.BlockSpec((1, 1, Dm, D_EXPERT), wsel),
                      pl.BlockSpec((1, 1, Dm, D_EXPERT), wsel),
                      pl.BlockSpec((1, 1, D_EXPERT, Dm), wsel)],
            out_specs=pl.BlockSpec((MOE_BLOCK, Dm), row),
            scratch_shapes=[pltpu.VMEM((Dm, D_EXPERT), BF16), pltpu.VMEM((Dm, D_EXPERT), BF16),
                            pltpu.VMEM((D_EXPERT, Dm), BF16)],
        ),
        out_shape=jax.ShapeDtypeStruct((rows, Dm), F32),
        compiler_params=_cp("arbitrary"),
        name="moe_experts",
    )(blk_e, nact, xs, w_gate, w_up, w_down)


def _combine_kernel(dest_hbm, w_ref, yb_hbm, o_ref, dest_smem, buf_ref, sem, isem, *, TM):
    i = pl.program_id(0)
    cp = pltpu.make_async_copy(dest_hbm.at[i], dest_smem, isem)
    cp.start()
    cp.wait()

    def issue(t, carry):
        for k in range(TOP_K):
            pltpu.make_async_copy(yb_hbm.at[pl.ds(dest_smem[k, t], 1)], buf_ref.at[k, pl.ds(t, 1)],
                                  sem).start(priority=k % 2)
        return carry

    lax.fori_loop(0, TM, issue, 0)

    def drain(t, carry):
        for k in range(TOP_K):
            pltpu.make_async_copy(yb_hbm.at[pl.ds(0, 1)], buf_ref.at[k, pl.ds(0, 1)], sem).wait()
        return carry

    lax.fori_loop(0, TM, drain, 0)
    w = w_ref[...]
    acc = buf_ref[0] * w[:, 0:1]
    for k in range(1, TOP_K):
        acc = acc + buf_ref[k] * w[:, k:k + 1]
    o_ref[...] = acc


def _combine(yb, dest_tiles, wts_tok):
    nt, _, TM = dest_tiles.shape
    T = nt * TM
    Dm = yb.shape[1]
    return pl.pallas_call(
        functools.partial(_combine_kernel, TM=TM),
        grid=(nt,),
        in_specs=[pl.BlockSpec(memory_space=pl.ANY),
                  pl.BlockSpec((TM, TOP_K), lambda i: (i, 0)),
                  pl.BlockSpec(memory_space=pl.ANY)],
        out_specs=pl.BlockSpec((TM, Dm), lambda i: (i, 0)),
        out_shape=jax.ShapeDtypeStruct((T, Dm), F32),
        scratch_shapes=[pltpu.SMEM((TOP_K, TM), jnp.int32), pltpu.VMEM((TOP_K, TM, Dm), F32),
                        pltpu.SemaphoreType.DMA, pltpu.SemaphoreType.DMA],
        compiler_params=_cp("arbitrary"),
        name="moe_combine",
    )(dest_tiles, wts_tok, yb)


def _shared_kernel(x_ref, wg_ref, wu_ref, wd_ref, o_ref, wgb_ref, wub_ref, wdb_ref):
    @pl.when(pl.program_id(0) == 0)
    def _():
        wgb_ref[...] = wg_ref[...].astype(BF16)
        wub_ref[...] = wu_ref[...].astype(BF16)
        wdb_ref[...] = wd_ref[...].astype(BF16)

    x = x_ref[...]
    a = _dot(x, wgb_ref[...])
    b = _dot(x, wub_ref[...])
    o_ref[...] = _dot((a * jax.nn.sigmoid(a) * b).astype(BF16), wdb_ref[...])


def _shared_expert(hb, wg, wu, wd):
    T, Dm = hb.shape
    Ds = wg.shape[1]
    TM = _pick_tile(T, 512)
    full = lambda r, c: pl.BlockSpec((r, c), lambda i: (0, 0))
    return pl.pallas_call(
        _shared_kernel,
        grid=(T // TM,),
        in_specs=[pl.BlockSpec((TM, Dm), lambda i: (i, 0)), full(Dm, Ds), full(Dm, Ds), full(Ds, Dm)],
        out_specs=pl.BlockSpec((TM, Dm), lambda i: (i, 0)),
        out_shape=jax.ShapeDtypeStruct((T, Dm), F32),
        scratch_shapes=[pltpu.VMEM((Dm, Ds), BF16), pltpu.VMEM((Dm, Ds), BF16), pltpu.VMEM((Ds, Dm), BF16)],
        compiler_params=_cp("arbitrary"),
        name="moe_shared",
    )(hb, wg, wu, wd)


def _moe(h, hb, w_router, router_bias, w_gate, w_up, w_down, layer, ws_gate, ws_up, ws_down):
    T, Dm = h.shape
    eidx, wts, rank, cnt = _router(h, w_router, router_bias)
    counts = cnt[:, 0].astype(jnp.int32)
    pcounts = (counts + MOE_BLOCK - 1) // MOE_BLOCK * MOE_BLOCK
    pends = jnp.cumsum(pcounts)
    pstarts = pends - pcounts
    nblk = -(-T * TOP_K // MOE_BLOCK) + N_EXPERTS
    rows = nblk * MOE_BLOCK
    eid = jnp.arange(N_EXPERTS, dtype=jnp.int32)
    dest = jnp.sum(jnp.where(eidx[..., None] == eid, pstarts.astype(jnp.int32), 0), axis=-1) + rank
    TM = _pick_tile(T, 128)
    dest_tiles = dest.reshape(TOP_K, T // TM, TM).transpose(1, 0, 2)
    blk_start = jnp.arange(nblk, dtype=jnp.int32) * MOE_BLOCK
    blk_e = jnp.minimum(jnp.sum((pends[None, :] <= blk_start[:, None]).astype(jnp.int32), axis=1),
                        N_EXPERTS - 1)
    nact = (pends[-1:] // MOE_BLOCK).astype(jnp.int32)
    pad_id = jnp.arange(MOE_BLOCK, dtype=jnp.int32)[None, :]
    fill = jnp.where(pad_id < (pcounts - counts)[:, None], (pstarts + counts)[:, None] + pad_id, -1).reshape(-1)
    xs = _dispatch(h, dest_tiles, fill, rows)
    yb = _experts(xs, blk_e, nact, w_gate, w_up, w_down, layer)
    routed = _combine(yb, dest_tiles, wts.T)
    shared = _shared_expert(hb, ws_gate, ws_up, ws_down)
    return routed, shared


def kernel(x, meta_tokens, ev_w_in, ev_w_out, ev_sink, od_w_in, od_conv_w, od_a_log, od_dt_bias, od_norm_w,
           od_w_out, ln_g, ln_b, w_router, router_bias, w_gate, w_up, w_down, ws_gate, ws_up, ws_down):
    B, S, Dm = x.shape
    Lp = S + FRONT
    meta = jnp.broadcast_to(meta_tokens.astype(x.dtype)[None], (B, N_META, Dm))
    h = jnp.concatenate([jnp.zeros((B, DEAD, Dm), x.dtype), meta, x], axis=1).reshape(B * Lp, Dm)
    hb = h.astype(BF16)
    for layer in range(DEPTH):
        i = layer // 2
        if layer % 2 == 0:
            mix = _even_mixer(hb, B, Lp, ev_w_in[i], ev_w_out[i], ev_sink[i])
        else:
            mix = _gdn_mixer(hb, B, Lp, od_w_in[i], od_conv_w[i], od_a_log[i], od_dt_bias[i],
                             od_norm_w[i], od_w_out[i])
        h, hb = _residual_ln(h, [mix], ln_g[layer, 0], ln_b[layer, 0])
        routed, shared = _moe(h, hb, w_router[layer], router_bias[layer], w_gate, w_up, w_down, layer,
                              ws_gate[layer], ws_up[layer], ws_down[layer])
        h, hb = _residual_ln(h, [routed, shared], ln_g[layer, 1], ln_b[layer, 1])
    return h.reshape(B, Lp, Dm)[:, FRONT:]
```

```python
import functools
import math

import numpy as np
import jax
import jax.numpy as jnp
from jax import lax
from jax.experimental import pallas as pl
from jax.experimental.pallas import tpu as pltpu

F32 = jnp.float32
BF16 = jnp.bfloat16

D_MODEL = 2048
DEPTH = 4
N_META = 16
FRONT = 128
DEAD = FRONT - N_META
RET_HEADS = 8
RET_HD = 128
RET_W = RET_HEADS * RET_HD
RET_CHUNK = 128
ATT_HEADS = 8
ATT_KV_HEADS = 2
ATT_GROUP = ATT_HEADS // ATT_KV_HEADS
ATT_HD = 128
ATT_BLOCK = 128
WINDOW = 128
EVEN_IN = 4 * RET_W + ATT_HEADS * ATT_HD + 2 * ATT_KV_HEADS * ATT_HD
GDN_QK_HEADS = 16
GDN_V_HEADS = 32
GDN_HD = 128
GDN_K_W = GDN_QK_HEADS * GDN_HD
GDN_V_W = GDN_V_HEADS * GDN_HD
GDN_CONV_CH = 2 * GDN_K_W + GDN_V_W
GDN_CHUNK = 64
GDN_HG = 32
CONV_WIDTH = 5
N_EXPERTS = 64
TOP_K = 8
N_GROUPS = 8
GROUP_SIZE = N_EXPERTS // N_GROUPS
TOPK_GROUPS = 4
D_EXPERT = 384
ROUTE_SCALE = 2.5
MOE_BLOCK = 256
ISSUE_GROUP = 4
DN_ALPHA = (2 * DEPTH) ** 0.25
LN_EPS = 1e-5
NORM_EPS = 1e-6
NEG = -1e30

VMEM_LIMIT = 56 * 2**20


def _cp(*sem, vmem=VMEM_LIMIT):
    return pltpu.CompilerParams(dimension_semantics=sem, vmem_limit_bytes=vmem)


def _dot(a, b):
    return jnp.dot(a, b, preferred_element_type=F32)


def _dot_nt(a, b):
    return lax.dot_general(a, b, (((1,), (1,)), ((), ())), preferred_element_type=F32)


def _dot_tn(a, b):
    return lax.dot_general(a, b, (((0,), (0,)), ((), ())), preferred_element_type=F32)


def _split3(a):
    hi = a.astype(BF16)
    r1 = a - hi.astype(F32)
    mid = r1.astype(BF16)
    lo = (r1 - mid.astype(F32)).astype(BF16)
    return hi, mid, lo


def _pick_tile(n, cap, mult=8):
    for t in range(min(cap, n), 0, -1):
        if n % t == 0 and t % mult == 0:
            return t
    return n


def _mm_kernel(*refs, ksplits):
    nx = len(ksplits)
    x_refs, w_ref, o_ref, wb_ref = refs[:nx], refs[nx], refs[nx + 1], refs[nx + 2]

    @pl.when(pl.program_id(1) == 0)
    def _():
        wb_ref[...] = w_ref[...].astype(BF16)

    acc = None
    k0 = 0
    for x_ref, kk in zip(x_refs, ksplits):
        part = _dot(x_ref[...].astype(BF16), wb_ref[k0:k0 + kk, :])
        acc = part if acc is None else acc + part
        k0 += kk
    o_ref[...] = acc.astype(o_ref.dtype)


def _matmul(xs, w, col0, ncols, tn, out_dtype=F32, tm_cap=512):
    M = xs[0].shape[0]
    ksplits = tuple(x.shape[1] for x in xs)
    K = sum(ksplits)
    assert w.shape[0] == K and ncols % tn == 0 and col0 % tn == 0
    tm = _pick_tile(M, tm_cap)
    in_specs = [pl.BlockSpec((tm, kk), lambda j, i: (i, 0)) for kk in ksplits]
    in_specs.append(pl.BlockSpec((K, tn), lambda j, i: (0, col0 // tn + j)))
    return pl.pallas_call(
        functools.partial(_mm_kernel, ksplits=ksplits),
        grid=(ncols // tn, M // tm),
        in_specs=in_specs,
        out_specs=pl.BlockSpec((tm, tn), lambda j, i: (i, j)),
        out_shape=jax.ShapeDtypeStruct((M, ncols), out_dtype),
        scratch_shapes=[pltpu.VMEM((K, tn), BF16)],
        compiler_params=_cp("arbitrary", "arbitrary"),
        name="matmul",
    )(*xs, w)


PACK_ROWS = D_MODEL // 256
HI_MASK = 0xFFFF0000


def _pack_rows(ref, x):
    n = x.shape[0]
    half = D_MODEL // 2
    for j in range(PACK_ROWS):
        lo = x[:, 128 * j:128 * (j + 1)].astype(BF16).astype(F32)
        hi = x[:, half + 128 * j:half + 128 * (j + 1)].astype(BF16).astype(F32)
        word = (pltpu.bitcast(lo, jnp.uint32) >> 16) | (pltpu.bitcast(hi, jnp.uint32) & jnp.uint32(HI_MASK))
        ref[pl.ds(j, n, stride=PACK_ROWS), :] = word


def _unpack_rows(ref, n):
    lo, hi = [], []
    for j in range(PACK_ROWS):
        word = ref[pl.ds(j, n, stride=PACK_ROWS), :]
        lo.append(pltpu.bitcast(word << 16, F32))
        hi.append(pltpu.bitcast(word & jnp.uint32(HI_MASK), F32))
    return lo + hi


def _layer_norm_rows(y, g, b):
    mu = jnp.mean(y, axis=-1, keepdims=True)
    d = y - mu
    var = jnp.mean(d * d, axis=-1, keepdims=True)
    return d * lax.rsqrt(var + LN_EPS) * g + b


def _ln_kernel(h_ref, a_ref, g_ref, b_ref, o_ref, ob_ref, op_ref):
    out = _layer_norm_rows(DN_ALPHA * h_ref[...] + a_ref[...], g_ref[...], b_ref[...])
    o_ref[...] = out
    ob_ref[...] = out.astype(BF16)
    _pack_rows(op_ref, out)


def _residual_ln(h, add, g, b):
    M, Dm = h.shape
    tm = _pick_tile(M, 256)
    row = pl.BlockSpec((tm, Dm), lambda i: (i, 0))
    vec = pl.BlockSpec((1, Dm), lambda i: (0, 0))
    return pl.pallas_call(
        _ln_kernel,
        grid=(M // tm,),
        in_specs=[row, row, vec, vec],
        out_specs=[row, row, pl.BlockSpec((tm * PACK_ROWS, 128), lambda i: (i, 0))],
        out_shape=[jax.ShapeDtypeStruct((M, Dm), F32), jax.ShapeDtypeStruct((M, Dm), BF16),
                   jax.ShapeDtypeStruct((M * PACK_ROWS, 128), jnp.uint32)],
        compiler_params=_cp("arbitrary"),
        name="residual_ln",
    )(h, add, g.reshape(1, Dm), b.reshape(1, Dm))


def _ret_tables(C):
    hh = np.arange(RET_HEADS, dtype=np.float64)
    lg = np.log(1.0 - 2.0 ** (-5.0 - hh))[:, None]
    pos = np.arange(C, dtype=np.float64)[None, :]
    vecs = np.stack([np.exp(lg * (pos + 1.0)),
                     np.exp(lg * (C - pos)),
                     np.exp(lg * (C - 1.0 - pos)),
                     np.exp(lg * pos),
                     np.exp(lg * C) * np.ones_like(pos)], axis=1)
    tab = np.broadcast_to(vecs[..., None], (RET_HEADS, 5, C, RET_HD))
    rel = np.abs(pos.T - pos)
    dsym = np.exp(lg[:, :, None] * rel[None])
    return jnp.asarray(tab, F32), jnp.asarray(dsym, F32)


def _ret_kernel(q_ref, k_ref, v_ref, g_ref, tab_ref, d_ref, o_ref, fst_ref, run_ref, *, nc, C):
    s = pl.program_id(1)
    fwd = s < nc
    c = jnp.where(fwd, s, 2 * nc - 1 - s)
    row = c * C + lax.broadcasted_iota(jnp.int32, (C, 1), 0)
    live = row >= DEAD
    heads = range(RET_HEADS)
    HD = RET_HD
    ks = [jnp.where(live, k_ref[0, :, h * HD:(h + 1) * HD] * (RET_HD ** -0.5), 0.0) for h in heads]
    vbs = [jnp.where(live, v_ref[0, :, h * HD:(h + 1) * HD], 0.0).astype(BF16) for h in heads]

    @pl.when(jnp.logical_or(s == 0, s == nc))
    def _():
        run_ref[...] = jnp.zeros_like(run_ref)

    @pl.when(fwd)
    def _():
        for h in heads:
            fst_ref[c, h] = run_ref[h].astype(BF16)
        upd = [_dot_tn((ks[h] * tab_ref[h, 2]).astype(BF16), vbs[h]) for h in heads]
        for h in heads:
            run_ref[h] = run_ref[h] * tab_ref[h, 4] + upd[h]

    @pl.when(jnp.logical_not(fwd))
    def _():
        qs = [q_ref[0, :, h * HD:(h + 1) * HD] for h in heads]
        sc = [(_dot_nt(qs[h].astype(BF16), ks[h].astype(BF16)) * d_ref[h]).astype(BF16) for h in heads]
        left = [_dot((qs[h] * tab_ref[h, 0]).astype(BF16), fst_ref[c, h]) for h in heads]
        right = [_dot((qs[h] * tab_ref[h, 1]).astype(BF16), run_ref[h].astype(BF16)) for h in heads]
        intra = [_dot(sc[h], vbs[h]) for h in heads]
        upd = [_dot_tn((ks[h] * tab_ref[h, 3]).astype(BF16), vbs[h]) for h in heads]
        for h in heads:
            run_ref[h] = run_ref[h] * tab_ref[h, 4] + upd[h]
            out = intra[h] + left[h] + right[h]
            mu = jnp.mean(out, axis=-1, keepdims=True)
            dlt = out - mu
            var = jnp.mean(dlt * dlt, axis=-1, keepdims=True)
            normed = dlt * lax.rsqrt(var + NORM_EPS)
            g = g_ref[0, :, h * HD:(h + 1) * HD]
            o_ref[0, :, h * HD:(h + 1) * HD] = (g * jax.nn.sigmoid(g) * normed).astype(o_ref.dtype)


def _retention(proj):
    B, Lp, _ = proj.shape
    C = RET_CHUNK
    nc = Lp // C
    tab, dsym = _ret_tables(C)

    def cidx(s):
        return jnp.where(s < nc, s, 2 * nc - 1 - s)

    def cidx_out(s):
        return jnp.where(s < nc, nc - 1, 2 * nc - 1 - s)

    return pl.pallas_call(
        functools.partial(_ret_kernel, nc=nc, C=C),
        grid=(B, 2 * nc),
        in_specs=[
            pl.BlockSpec((1, C, RET_W), lambda b, s: (b, cidx_out(s), 0)),
            pl.BlockSpec((1, C, RET_W), lambda b, s: (b, cidx(s), 1)),
            pl.BlockSpec((1, C, RET_W), lambda b, s: (b, cidx(s), 2)),
            pl.BlockSpec((1, C, RET_W), lambda b, s: (b, cidx_out(s), 3)),
            pl.BlockSpec((RET_HEADS, 5, C, RET_HD), lambda b, s: (0, 0, 0, 0)),
            pl.BlockSpec((RET_HEADS, C, C), lambda b, s: (0, 0, 0)),
        ],
        out_specs=pl.BlockSpec((1, C, RET_W), lambda b, s: (b, cidx_out(s), 0)),
        out_shape=jax.ShapeDtypeStruct((B, Lp, RET_W), BF16),
        scratch_shapes=[pltpu.VMEM((nc, RET_HEADS, RET_HD, RET_HD), BF16),
                        pltpu.VMEM((RET_HEADS, RET_HD, RET_HD), F32)],
        compiler_params=_cp("arbitrary", "arbitrary"),
        name="retention",
    )(proj, proj, proj, proj, tab, dsym)


def _att_kernel(sink_ref, slope_ref, q_ref, km_ref, kp_ref, kc_ref, kn_ref,
                vm_ref, vp_ref, vc_ref, vn_ref, o_ref, *, nb):
    j = pl.program_id(1)
    qb = pl.program_id(2)
    T = ATT_BLOCK
    ii = lax.broadcasted_iota(jnp.int32, (T, T), 0)
    jj = lax.broadcasted_iota(jnp.int32, (T, T), 1)
    meta_ok = jj >= DEAD
    pieces = []
    for off, k_ref, v_ref, ok in ((-T, kp_ref, vp_ref, qb >= 2),
                                  (0, kc_ref, vc_ref, qb >= 1),
                                  (T, kn_ref, vn_ref, qb + 1 <= nb)):
        dist = jnp.abs(jj + off - ii)
        pieces.append((dist.astype(F32), dist <= WINDOW, k_ref[0].astype(BF16), v_ref[0].astype(BF16), ok))
    km = km_ref[0].astype(BF16)
    vm = vm_ref[0].astype(BF16)
    vals = [vm] + [p[3] for p in pieces]
    groups = range(ATT_GROUP)
    qs = [(q_ref[0, :, g * ATT_HD:(g + 1) * ATT_HD] * (ATT_HD ** -0.5)).astype(BF16) for g in groups]
    raw = [[_dot_nt(qs[g], km)] + [_dot_nt(qs[g], p[2]) for p in pieces] for g in groups]
    es, dens = [], []
    for g in groups:
        slope = slope_ref[j, g]
        sink = sink_ref[j, g]
        s_list = [jnp.where(meta_ok, raw[g][0], NEG)]
        for (dist, inwin, _, _, ok), r in zip(pieces, raw[g][1:]):
            s_list.append(jnp.where(ok, jnp.where(inwin, r - slope * dist, NEG), NEG))
        m = jnp.full((T, 1), sink, F32)
        for sb in s_list:
            m = jnp.maximum(m, jnp.max(sb, axis=-1, keepdims=True))
        e_list = [jnp.exp(sb - m) for sb in s_list]
        den = jnp.exp(sink - m)
        for e in e_list:
            den = den + jnp.sum(e, axis=-1, keepdims=True)
        es.append([e.astype(BF16) for e in e_list])
        dens.append(den)
    pv = [[_dot(e, vb) for e, vb in zip(es[g], vals)] for g in groups]
    for g in groups:
        acc = pv[g][0] + pv[g][1] + pv[g][2] + pv[g][3]
        o_ref[0, :, g * ATT_HD:(g + 1) * ATT_HD] = (acc / dens[g]).astype(o_ref.dtype)


def _window_attention(proj, sink):
    B, Lp, _ = proj.shape
    nb = Lp // ATT_BLOCK - 1
    gw = ATT_GROUP * ATT_HD
    q0 = 4 * RET_W // gw
    k0 = (4 * RET_W + ATT_HEADS * ATT_HD) // ATT_HD
    v0 = k0 + ATT_KV_HEADS
    slopes = np.asarray(2.0 ** (-8.0 * (np.arange(ATT_HEADS) + 1.0) / ATT_HEADS), np.float32)
    smem = pl.BlockSpec(memory_space=pltpu.SMEM)

    def kv(c0, f):
        return pl.BlockSpec((1, ATT_BLOCK, ATT_HD), lambda b, j, t: (b, f(t), c0 + j))

    rows = [lambda t: 0, lambda t: jnp.maximum(t - 1, 0), lambda t: t, lambda t: jnp.minimum(t + 1, nb)]
    return pl.pallas_call(
        functools.partial(_att_kernel, nb=nb),
        grid=(B, ATT_KV_HEADS, nb + 1),
        in_specs=[smem, smem, pl.BlockSpec((1, ATT_BLOCK, gw), lambda b, j, t: (b, t, q0 + j))]
        + [kv(k0, f) for f in rows] + [kv(v0, f) for f in rows],
        out_specs=pl.BlockSpec((1, ATT_BLOCK, gw), lambda b, j, t: (b, t, j)),
        out_shape=jax.ShapeDtypeStruct((B, Lp, ATT_HEADS * ATT_HD), BF16),
        compiler_params=_cp("arbitrary", "arbitrary", "arbitrary"),
        name="window_attention",
    )(sink.astype(F32).reshape(ATT_KV_HEADS, ATT_GROUP), jnp.asarray(slopes).reshape(ATT_KV_HEADS, ATT_GROUP),
      *([proj] * 9))


CONV_HALO = 16


def _conv_shift_matrix(TT):
    half = CONV_WIDTH // 2
    t = np.arange(TT)[:, None]
    s = np.arange(TT + 2 * CONV_HALO)[None, :]
    blocks = [(s == t + CONV_HALO + tap - half) for tap in range(CONV_WIDTH) if tap != half]
    return jnp.asarray(np.concatenate(blocks, axis=0), BF16)


def _conv_kernel(xp_ref, x_ref, xn_ref, w_ref, sh_ref, o_ref, *, mode, TT, nt):
    t = pl.program_id(1)
    H = CONV_HALO
    rows = t * TT + lax.broadcasted_iota(jnp.int32, (TT, 1), 0)
    live = rows >= DEAD
    x = jnp.where(live, x_ref[0].astype(F32), 0.0)
    prow = t * TT - H + lax.broadcasted_iota(jnp.int32, (H, 1), 0)
    prev = jnp.where(prow >= DEAD, xp_ref[0].astype(F32), 0.0)
    nrow = (t + 1) * TT + lax.broadcasted_iota(jnp.int32, (H, 1), 0)
    nxt = jnp.where(jnp.logical_and(nrow >= DEAD, t < nt - 1), xn_ref[0].astype(F32), 0.0)
    staged = jnp.concatenate([prev.astype(BF16), x.astype(BF16), nxt.astype(BF16)], axis=0)
    shifted = _dot(sh_ref[...], staged)
    w = w_ref[...]
    half = CONV_WIDTH // 2
    acc = x * w[half:half + 1, :]
    blk = 0
    for tap in range(CONV_WIDTH):
        if tap == half:
            continue
        acc = acc + shifted[blk * TT:(blk + 1) * TT, :] * w[tap:tap + 1, :]
        blk += 1
    y = acc * jax.nn.sigmoid(acc)
    if mode in ("q", "k"):
        scale = GDN_HD ** -0.5 if mode == "q" else 1.0
        for a in range(y.shape[1] // GDN_HD):
            ya = y[:, a * GDN_HD:(a + 1) * GDN_HD]
            ya = ya * lax.rsqrt(jnp.sum(ya * ya, axis=-1, keepdims=True) + NORM_EPS)
            if mode == "q":
                ya = ya * scale
            o_ref[0, :, a * GDN_HD:(a + 1) * GDN_HD] = jnp.where(live, ya, 0.0).astype(o_ref.dtype)
    else:
        o_ref[0] = jnp.where(live, y, 0.0).astype(o_ref.dtype)


def _short_conv(qkv, conv_w, mode, col0, ncols):
    B, Lp, _ = qkv.shape
    H = CONV_HALO
    TT = _pick_tile(Lp, 256, mult=H)
    TC = 1024
    nt = Lp // TT
    c0 = col0 // TC
    shift = _conv_shift_matrix(TT)
    return pl.pallas_call(
        functools.partial(_conv_kernel, mode=mode, TT=TT, nt=nt),
        grid=(B, nt, ncols // TC),
        in_specs=[
            pl.BlockSpec((1, H, TC), lambda b, t, c: (b, jnp.maximum(t * (TT // H) - 1, 0), c0 + c)),
            pl.BlockSpec((1, TT, TC), lambda b, t, c: (b, t, c0 + c)),
            pl.BlockSpec((1, H, TC), lambda b, t, c: (b, jnp.minimum((t + 1) * (TT // H), Lp // H - 1), c0 + c)),
            pl.BlockSpec((CONV_WIDTH, TC), lambda b, t, c: (0, c0 + c)),
            pl.BlockSpec(shift.shape, lambda b, t, c: (0, 0)),
        ],
        out_specs=pl.BlockSpec((1, TT, TC), lambda b, t, c: (b, t, c)),
        out_shape=jax.ShapeDtypeStruct((B, Lp, ncols), BF16),
        compiler_params=_cp("arbitrary", "arbitrary", "arbitrary"),
        name="short_conv_" + mode,
    )(qkv, qkv, qkv, conv_w, shift)


def _gate_kernel(ba_ref, alog_ref, dtb_ref, o_ref, *, TT):
    t = pl.program_id(1)
    rows = t * TT + lax.broadcasted_iota(jnp.int32, (TT, 1), 0)
    lane = lax.broadcasted_iota(jnp.int32, (1, 4 * GDN_V_HEADS), 1)
    is_g = (lane // GDN_V_HEADS) % 2 == 1
    x = ba_ref[0]
    beta = jax.nn.sigmoid(x)
    xs = x + dtb_ref[...]
    softplus = jnp.maximum(xs, 0.0) + jnp.log(1.0 + jnp.exp(-jnp.abs(xs)))
    g = -jnp.exp(alog_ref[...]) * softplus
    o_ref[0] = jnp.where(rows >= DEAD, jnp.where(is_g, g, beta), 0.0)


def _gates(ba, a_log, dt_bias):
    B, Lp, W = ba.shape
    TT = _pick_tile(Lp, 1024)
    zeros = jnp.zeros((2, 1, GDN_V_HEADS), F32)
    alog = jnp.concatenate([zeros, a_log.astype(F32)[:, None, :]], axis=1).reshape(1, W)
    dtb = jnp.concatenate([zeros, dt_bias.astype(F32)[:, None, :]], axis=1).reshape(1, W)
    vec = pl.BlockSpec((1, W), lambda b, t: (0, 0))
    return pl.pallas_call(
        functools.partial(_gate_kernel, TT=TT),
        grid=(B, Lp // TT),
        in_specs=[pl.BlockSpec((1, TT, W), lambda b, t: (b, t, 0)), vec, vec],
        out_specs=pl.BlockSpec((1, TT, W), lambda b, t: (b, t, 0)),
        out_shape=jax.ShapeDtypeStruct((B, Lp, W), F32),
        compiler_params=_cp("arbitrary", "arbitrary"),
        name="gdn_gates",
    )(ba, alog, dtb)


def _gdn_kernel(*refs, rev, final):
    if final:
        q_ref, k_ref, v_ref, gc_ref, gr_ref, of_ref, z_ref, nw_ref, o_ref, s_ref = refs
    else:
        q_ref, k_ref, v_ref, gc_ref, gr_ref, o_ref, s_ref = refs
    C = GDN_CHUNK
    HG = GDN_HG

    @pl.when(pl.program_id(2) == 0)
    def _():
        s_ref[...] = jnp.zeros_like(s_ref)

    ii = lax.broadcasted_iota(jnp.int32, (C, C), 0)
    jj = lax.broadcasted_iota(jnp.int32, (C, C), 1)
    incl = (jj >= ii) if rev else (jj <= ii)
    strict = (jj > ii) if rev else (jj < ii)
    tri_col = jnp.where(incl, 1.0, 0.0).astype(BF16)
    tri_row = jnp.where((ii >= jj) if rev else (ii <= jj), 1.0, 0.0).astype(BF16)
    gcol = gc_ref[0, 0]
    grow = gr_ref[0, 0, 0]
    gc_col = sum(_dot(tri_col, p) for p in _split3(gcol))
    gc_row = sum(_dot(p, tri_row) for p in _split3(grow))
    last = 0 if rev else C - 1
    d0 = 2 * HG if rev else 0
    HD = GDN_HD
    heads = range(HG)
    pairs = range(HG // 2)
    qs = [q_ref[0, :, p * HD:(p + 1) * HD] for p in pairs]
    ks = [k_ref[0, :, p * HD:(p + 1) * HD] for p in pairs]
    kfs = [k.astype(F32) for k in ks]
    kq = [_dot_nt(jnp.concatenate([ks[p], qs[p]], axis=0), ks[p]) for p in pairs]
    gram = [r[:C] for r in kq]
    qk = [r[C:] for r in kq]
    beta = [gcol[:, d0 + h:d0 + h + 1] for h in heads]
    gcc = [gc_col[:, d0 + HG + h:d0 + HG + h + 1] for h in heads]
    gcr = [gc_row[d0 + HG + h:d0 + HG + h + 1, :] for h in heads]
    gtot = [g[last:last + 1, :] for g in gcc]
    decay = [jnp.where(incl, jnp.exp(jnp.where(incl, gcc[h] - gcr[h], 0.0)), 0.0) for h in heads]
    eg = [jnp.exp(g) for g in gcc]
    ms = [-jnp.where(strict, beta[h] * gram[h // 2] * decay[h], 0.0) for h in heads]
    pbs = [m.astype(BF16) for m in ms]
    ps = [_dot(pb, pb) for pb in pbs]
    pbs = [p.astype(BF16) for p in ps]
    for _ in range(4):
        res = [_dot(jnp.concatenate([ms[h].astype(BF16), pbs[h]], axis=0), pbs[h]) for h in heads]
        ms = [ms[h] + ps[h] + res[h][:C] for h in heads]
        ps = [r[C:] for r in res]
        pbs = [p.astype(BF16) for p in ps]
    ms = [ms[h] + ps[h] + _dot(ms[h].astype(BF16), pbs[h]) for h in heads]
    rhs = [jnp.concatenate([v_ref[0, :, h * HD:(h + 1) * HD].astype(F32) * beta[h],
                            kfs[h // 2] * (beta[h] * eg[h])], axis=1) for h in heads]
    sol = [rhs[h] + _dot(ms[h].astype(BF16), rhs[h].astype(BF16)) for h in heads]
    st = [s_ref[h] for h in heads]
    stb = [s.astype(BF16) for s in st]
    lhs = [jnp.concatenate([sol[h][:, HD:].astype(BF16), (qs[h // 2].astype(F32) * eg[h]).astype(BF16)], axis=0)
           for h in heads]
    ws = [_dot(lhs[h], stb[h]) for h in heads]
    vnb = [(sol[h][:, :HD] - ws[h][:C]).astype(BF16) for h in heads]
    outs = [ws[h][C:] + _dot((qk[h // 2] * decay[h]).astype(BF16), vnb[h]) for h in heads]
    upd = [_dot_tn((kfs[h // 2] * jnp.exp(gtot[h] - gcc[h])).astype(BF16), vnb[h]) for h in heads]
    for h in heads:
        s_ref[h] = st[h] * jnp.exp(gtot[h]) + upd[h]
        sl = slice(h * HD, (h + 1) * HD)
        if final:
            o = outs[h] + of_ref[0, :, sl]
            z = z_ref[0, :, sl]
            o = o * lax.rsqrt(jnp.mean(o * o, axis=-1, keepdims=True) + NORM_EPS) * nw_ref[...]
            o_ref[0, :, sl] = (o * (z * jax.nn.sigmoid(z))).astype(o_ref.dtype)
        else:
            o_ref[0, :, sl] = outs[h]


def _gdn_dir(q, k, v, gcol, grow, rev, o_fwd=None, z=None, norm_w=None):
    B, Lp, _ = v.shape
    C = GDN_CHUNK
    nc = Lp // C
    ng = GDN_V_HEADS // GDN_HG
    qw = GDN_HG // 2 * GDN_HD
    vw = GDN_HG * GDN_HD
    final = o_fwd is not None

    def cc(c):
        return nc - 1 - c if rev else c

    in_specs = [
        pl.BlockSpec((1, C, qw), lambda b, g, c: (b, cc(c), g)),
        pl.BlockSpec((1, C, qw), lambda b, g, c: (b, cc(c), g)),
        pl.BlockSpec((1, C, vw), lambda b, g, c: (b, cc(c), g)),
        pl.BlockSpec((1, 1, C, 4 * GDN_HG), lambda b, g, c: (b, g, cc(c), 0)),
        pl.BlockSpec((1, 1, 1, 4 * GDN_HG, C), lambda b, g, c: (b, g, cc(c), 0, 0)),
    ]
    args = [q, k, v, gcol, grow]
    if final:
        in_specs += [pl.BlockSpec((1, C, vw), lambda b, g, c: (b, cc(c), g)),
                     pl.BlockSpec((1, C, vw), lambda b, g, c: (b, cc(c), g)),
                     pl.BlockSpec((1, GDN_HD), lambda b, g, c: (0, 0))]
        args += [o_fwd, z, norm_w.astype(F32).reshape(1, GDN_HD)]
    return pl.pallas_call(
        functools.partial(_gdn_kernel, rev=rev, final=final),
        grid=(B, ng, nc),
        in_specs=in_specs,
        out_specs=pl.BlockSpec((1, C, vw), lambda b, g, c: (b, cc(c), g)),
        out_shape=jax.ShapeDtypeStruct((B, Lp, GDN_V_W), BF16 if final else F32),
        scratch_shapes=[pltpu.VMEM((GDN_HG, GDN_HD, GDN_HD), F32)],
        compiler_params=_cp("arbitrary", "arbitrary", "arbitrary"),
        name="gdn_bwd" if rev else "gdn_fwd",
    )(*args)


def _gdn_mixer(hb, B, Lp, w_in, conv_w, a_log, dt_bias, norm_w, w_out):
    qkv = _matmul([hb], w_in, 0, GDN_CONV_CH, 1024, out_dtype=BF16).reshape(B, Lp, GDN_CONV_CH)
    z = _matmul([hb], w_in, GDN_CONV_CH, GDN_V_W, 1024).reshape(B, Lp, GDN_V_W)
    ba = _matmul([hb], w_in, GDN_CONV_CH + GDN_V_W, 4 * GDN_V_HEADS, 128).reshape(B, Lp, 4 * GDN_V_HEADS)
    q = _short_conv(qkv, conv_w, "q", 0, GDN_K_W)
    k = _short_conv(qkv, conv_w, "k", GDN_K_W, GDN_K_W)
    v = _short_conv(qkv, conv_w, "v", 2 * GDN_K_W, GDN_V_W)
    gb = _gates(ba, a_log, dt_bias)
    ng = GDN_V_HEADS // GDN_HG
    nc = Lp // GDN_CHUNK
    gcol = gb.reshape(B, Lp, 4, ng, GDN_HG).transpose(0, 3, 1, 2, 4).reshape(B, ng, Lp, 4 * GDN_HG)
    grow = gcol.reshape(B, ng, nc, GDN_CHUNK, 4 * GDN_HG).transpose(0, 1, 2, 4, 3)
    o_f = _gdn_dir(q, k, v, gcol, grow, rev=False)
    o = _gdn_dir(q, k, v, gcol, grow, rev=True, o_fwd=o_f, z=z, norm_w=norm_w)
    return _matmul([o.reshape(B * Lp, GDN_V_W)], w_out, 0, D_MODEL, 512)


def _even_mixer(hb, B, Lp, w_in, w_out, sink):
    proj = _matmul([hb], w_in, 0, EVEN_IN, 512).reshape(B, Lp, EVEN_IN)
    ret = _retention(proj).reshape(B * Lp, RET_W)
    att = _window_attention(proj, sink).reshape(B * Lp, ATT_HEADS * ATT_HD)
    return _matmul([ret, att], w_out, 0, D_MODEL, 1024)


def _router_kernel(x_ref, wt_ref, bias_ref, eidx_ref, wts_ref, rank_ref, cnt_ref, carry_ref, *, TM):
    E = N_EXPERTS

    @pl.when(pl.program_id(0) == 0)
    def _():
        carry_ref[...] = jnp.zeros_like(carry_ref)

    xh, xm, xl = _split3(x_ref[...])
    wh, wm, wl = _split3(wt_ref[...])
    logits = (_dot_nt(wh, xh) + (_dot_nt(wh, xm) + _dot_nt(wm, xh))
              + (_dot_nt(wh, xl) + _dot_nt(wl, xh) + _dot_nt(wm, xm)))
    scores = jax.nn.sigmoid(logits)
    choice = scores + bias_ref[...]
    ninf = -jnp.inf
    io8 = lax.broadcasted_iota(jnp.int32, (GROUP_SIZE, TM), 0)
    gs_rows = []
    for g in range(N_GROUPS):
        cg = choice[g * GROUP_SIZE:(g + 1) * GROUP_SIZE, :]
        m1 = jnp.max(cg, axis=0, keepdims=True)
        i1 = jnp.min(jnp.where(cg == m1, io8, GROUP_SIZE), axis=0, keepdims=True)
        m2 = jnp.max(jnp.where(io8 == i1, ninf, cg), axis=0, keepdims=True)
        gs_rows.append(m1 + m2)
    gs = jnp.concatenate(gs_rows, axis=0)
    gsel = jnp.zeros((N_GROUPS, TM), jnp.int32)
    for _ in range(TOPK_GROUPS):
        m = jnp.max(gs, axis=0, keepdims=True)
        idx = jnp.min(jnp.where(gs == m, io8, N_GROUPS), axis=0, keepdims=True)
        hit = io8 == idx
        gsel = jnp.where(hit, 1, gsel)
        gs = jnp.where(hit, ninf, gs)
    masked = jnp.concatenate(
        [jnp.where(gsel[g:g + 1, :] > 0, choice[g * GROUP_SIZE:(g + 1) * GROUP_SIZE, :], ninf)
         for g in range(N_GROUPS)], axis=0)
    ioe = lax.broadcasted_iota(jnp.int32, (E, TM), 0)
    sel = jnp.zeros((E, TM), F32)
    idx_rows, w_rows = [], []
    for _ in range(TOP_K):
        m = jnp.max(masked, axis=0, keepdims=True)
        idx = jnp.min(jnp.where(masked == m, ioe, E), axis=0, keepdims=True)
        hit = ioe == idx
        idx_rows.append(idx)
        w_rows.append(jnp.sum(jnp.where(hit, scores, 0.0), axis=0, keepdims=True))
        sel = jnp.where(hit, 1.0, sel)
        masked = jnp.where(hit, ninf, masked)
    wsum = w_rows[0]
    for w in w_rows[1:]:
        wsum = wsum + w
    ti = lax.broadcasted_iota(jnp.int32, (TM, TM), 0)
    tj = lax.broadcasted_iota(jnp.int32, (TM, TM), 1)
    before = jnp.where(ti < tj, 1.0, 0.0).astype(BF16)
    rank = _dot(sel.astype(BF16), before) + carry_ref[:, 0:1]
    rank_rows = [jnp.sum(jnp.where(ioe == idx, rank, 0.0), axis=0, keepdims=True) for idx in idx_rows]
    eidx_ref[...] = jnp.concatenate(idx_rows, axis=0)
    wts_ref[...] = jnp.concatenate([w / wsum * ROUTE_SCALE for w in w_rows], axis=0)
    rank_ref[...] = jnp.concatenate(rank_rows, axis=0).astype(jnp.int32)
    carry_ref[...] = carry_ref[...] + jnp.sum(sel, axis=1, keepdims=True)
    cnt_ref[...] = carry_ref[...]


def _router(h, w_router, router_bias):
    T, Dm = h.shape
    TM = _pick_tile(T, 256)
    tok = pl.BlockSpec((TOP_K, TM), lambda i: (0, i))
    return pl.pallas_call(
        functools.partial(_router_kernel, TM=TM),
        grid=(T // TM,),
        in_specs=[pl.BlockSpec((TM, Dm), lambda i: (i, 0)),
                  pl.BlockSpec((N_EXPERTS, Dm), lambda i: (0, 0)),
                  pl.BlockSpec((N_EXPERTS, 1), lambda i: (0, 0))],
        out_specs=[tok, tok, tok, pl.BlockSpec((N_EXPERTS, 128), lambda i: (0, 0))],
        out_shape=[jax.ShapeDtypeStruct((TOP_K, T), jnp.int32), jax.ShapeDtypeStruct((TOP_K, T), F32),
                   jax.ShapeDtypeStruct((TOP_K, T), jnp.int32), jax.ShapeDtypeStruct((N_EXPERTS, 128), F32)],
        scratch_shapes=[pltpu.VMEM((N_EXPERTS, 128), F32)],
        compiler_params=_cp("arbitrary"),
        name="moe_router",
    )(h, w_router.T, router_bias.astype(F32).reshape(N_EXPERTS, 1))


def _dispatch_kernel(dest_hbm, fill_hbm, hp_hbm, xs_hbm, dest_smem, fill_smem, zrow_ref, sem, isem, fsem, zsem,
                     *, TM, nt, nfill):
    i = pl.program_id(0)
    cp = pltpu.make_async_copy(dest_hbm.at[i], dest_smem, isem)
    cp.start()

    R = PACK_ROWS

    def slot_rows(d):
        return xs_hbm.at[pl.ds(pl.multiple_of(d * R, R), R)]

    @pl.when(i == 0)
    def _():
        zrow_ref[...] = jnp.zeros_like(zrow_ref)
        fc = pltpu.make_async_copy(fill_hbm, fill_smem, fsem)
        fc.start()
        fc.wait()

        def body(r, carry):
            d = fill_smem[r]

            @pl.when(d >= 0)
            def _():
                pltpu.make_async_copy(zrow_ref, slot_rows(d), zsem).start()
            return carry

        lax.fori_loop(0, nfill, body, 0)

        def wbody(r, carry):
            @pl.when(fill_smem[r] >= 0)
            def _():
                pltpu.make_async_copy(zrow_ref, slot_rows(0), zsem).wait()
            return carry

        lax.fori_loop(0, nfill, wbody, 0)

    cp.wait()

    def issue(grp, carry):
        t0 = grp * ISSUE_GROUP
        slots = [[dest_smem[k, t0 + u] for k in range(TOP_K)] for u in range(ISSUE_GROUP)]
        for u in range(ISSUE_GROUP):
            src = hp_hbm.at[pl.ds(pl.multiple_of((i * TM + t0 + u) * R, R), R)]
            for k in range(TOP_K):
                pltpu.make_async_copy(src, slot_rows(slots[u][k]), sem).start(priority=k % 2)
        return carry

    lax.fori_loop(0, TM // ISSUE_GROUP, issue, 0)

    def drain(t, carry):
        for k in range(TOP_K):
            pltpu.make_async_copy(hp_hbm.at[pl.ds(0, R)], slot_rows(0), sem).wait()
        return carry

    @pl.when(i > 0)
    def _():
        lax.fori_loop(0, TM, drain, 0)

    @pl.when(i == nt - 1)
    def _():
        lax.fori_loop(0, TM, drain, 0)


def _dispatch(hp, dest_tiles, fill, rows):
    nt, _, TM = dest_tiles.shape
    nfill = fill.shape[0]
    return pl.pallas_call(
        functools.partial(_dispatch_kernel, TM=TM, nt=nt, nfill=nfill),
        grid=(nt,),
        in_specs=[pl.BlockSpec(memory_space=pl.ANY), pl.BlockSpec(memory_space=pl.ANY),
                  pl.BlockSpec(memory_space=pl.ANY)],
        out_specs=pl.BlockSpec(memory_space=pl.ANY),
        out_shape=jax.ShapeDtypeStruct((rows * PACK_ROWS, 128), jnp.uint32),
        scratch_shapes=[pltpu.SMEM((TOP_K, TM), jnp.int32), pltpu.SMEM((nfill,), jnp.int32),
                        pltpu.VMEM((PACK_ROWS, 128), jnp.uint32), pltpu.SemaphoreType.DMA,
                        pltpu.SemaphoreType.DMA, pltpu.SemaphoreType.DMA, pltpu.SemaphoreType.DMA],
        compiler_params=_cp("arbitrary"),
        name="moe_dispatch",
    )(dest_tiles, fill, hp)


def _expert_kernel(be_ref, na_ref, x_ref, wg_ref, wu_ref, wd_ref, o_ref, wgb_ref, wub_ref, wdb_ref):
    i = pl.program_id(0)
    active = i < na_ref[0]
    new_expert = jnp.logical_or(i == 0, be_ref[i] != be_ref[jnp.maximum(i - 1, 0)])

    @pl.when(jnp.logical_and(active, new_expert))
    def _():
        wgb_ref[...] = wg_ref[0, 0].astype(BF16)
        wub_ref[...] = wu_ref[0, 0].astype(BF16)
        wdb_ref[...] = wd_ref[0, 0].astype(BF16)

    @pl.when(active)
    def _():
        x = jnp.concatenate([c.astype(BF16) for c in _unpack_rows(x_ref, MOE_BLOCK)], axis=1)
        a = _dot(x, wgb_ref[...])
        b = _dot(x, wub_ref[...])
        hmid = (a * jax.nn.sigmoid(a) * b).astype(BF16)
        _pack_rows(o_ref, _dot(hmid, wdb_ref[...]))


def _experts(xs, blk_e, nact, w_gate, w_up, w_down, layer):
    Dm = D_MODEL
    rows = xs.shape[0] // PACK_ROWS
    nblk = rows // MOE_BLOCK
    pblock = MOE_BLOCK * PACK_ROWS

    def row(i, be, na):
        return (jnp.minimum(i, na[0] - 1), 0)

    def wsel(i, be, na):
        return (layer, be[jnp.minimum(i, na[0] - 1)], 0, 0)

    return pl.pallas_call(
        _expert_kernel,
        grid_spec=pltpu.PrefetchScalarGridSpec(
            num_scalar_prefetch=2,
            grid=(nblk,),
            in_specs=[pl.BlockSpec((pblock, 128), row),
                      pl.BlockSpec((1, 1, Dm, D_EXPERT), wsel),
                      pl.BlockSpec((1, 1, Dm, D_EXPERT), wsel),
                      pl.BlockSpec((1, 1, D_EXPERT, Dm), wsel)],
            out_specs=pl.BlockSpec((pblock, 128), row),
            scratch_shapes=[pltpu.VMEM((Dm, D_EXPERT), BF16), pltpu.VMEM((Dm, D_EXPERT), BF16),
                            pltpu.VMEM((D_EXPERT, Dm), BF16)],
        ),
        out_shape=jax.ShapeDtypeStruct((rows * PACK_ROWS, 128), jnp.uint32),
        compiler_params=_cp("arbitrary"),
        name="moe_experts",
    )(blk_e, nact, xs, w_gate, w_up, w_down)


def _combine_kernel(dest_hbm, yb_hbm, w_ref, h_ref, sh_ref, g_ref, b_ref, o_ref, ob_ref,
                    dest_smem, buf_ref, sem, isem, *, TM, nt):
    i = pl.program_id(0)
    R = PACK_ROWS

    def fetch(tile, slot):
        cp = pltpu.make_async_copy(dest_hbm.at[tile], dest_smem.at[slot], isem)
        cp.start()
        cp.wait()

        def issue(grp, carry):
            t0 = grp * ISSUE_GROUP
            slots = [[dest_smem[slot, k, t0 + u] for k in range(TOP_K)] for u in range(ISSUE_GROUP)]
            for u in range(ISSUE_GROUP):
                for k in range(TOP_K):
                    src = yb_hbm.at[pl.ds(pl.multiple_of(slots[u][k] * R, R), R)]
                    dst = buf_ref.at[slot, k, pl.ds(pl.multiple_of((t0 + u) * R, R), R)]
                    pltpu.make_async_copy(src, dst, sem.at[slot]).start(priority=k % 2)
            return carry

        lax.fori_loop(0, TM // ISSUE_GROUP, issue, 0)

    @pl.when(i == 0)
    def _():
        fetch(0, 0)

    @pl.when(i + 1 < nt)
    def _():
        fetch(i + 1, (i + 1) % 2)

    slot = i % 2

    def drain(t, carry):
        for k in range(TOP_K):
            pltpu.make_async_copy(yb_hbm.at[pl.ds(0, R)], buf_ref.at[slot, k, pl.ds(0, R)], sem.at[slot]).wait()
        return carry

    lax.fori_loop(0, TM, drain, 0)
    w = w_ref[...]
    acc = None
    for k in range(TOP_K):
        cols = _unpack_rows(buf_ref.at[slot, k], TM)
        wk = w[:, k:k + 1]
        acc = [c * wk for c in cols] if acc is None else [a + c * wk for a, c in zip(acc, cols)]
    routed = jnp.concatenate(acc, axis=1)
    out = _layer_norm_rows(DN_ALPHA * h_ref[...] + (routed + sh_ref[...]), g_ref[...], b_ref[...])
    o_ref[...] = out
    ob_ref[...] = out.astype(BF16)


def _combine_ln(yb, dest_tiles, wts_tok, h, shared, g, b):
    nt, _, TM = dest_tiles.shape
    T, Dm = h.shape
    row = pl.BlockSpec((TM, Dm), lambda i: (i, 0))
    vec = pl.BlockSpec((1, Dm), lambda i: (0, 0))
    hbm = pl.BlockSpec(memory_space=pl.ANY)
    return pl.pallas_call(
        functools.partial(_combine_kernel, TM=TM, nt=nt),
        grid=(nt,),
        in_specs=[hbm, hbm, pl.BlockSpec((TM, TOP_K), lambda i: (i, 0)), row, row, vec, vec],
        out_specs=[row, row],
        out_shape=[jax.ShapeDtypeStruct((T, Dm), F32), jax.ShapeDtypeStruct((T, Dm), BF16)],
        scratch_shapes=[pltpu.SMEM((2, TOP_K, TM), jnp.int32),
                        pltpu.VMEM((2, TOP_K, TM * PACK_ROWS, 128), jnp.uint32),
                        pltpu.SemaphoreType.DMA((2,)), pltpu.SemaphoreType.DMA],
        compiler_params=_cp("arbitrary"),
        name="moe_combine",
    )(dest_tiles, yb, wts_tok, h, shared, g.reshape(1, Dm), b.reshape(1, Dm))


def _shared_kernel(x_ref, wg_ref, wu_ref, wd_ref, o_ref, wgb_ref, wub_ref, wdb_ref):
    @pl.when(pl.program_id(0) == 0)
    def _():
        wgb_ref[...] = wg_ref[...].astype(BF16)
        wub_ref[...] = wu_ref[...].astype(BF16)
        wdb_ref[...] = wd_ref[...].astype(BF16)

    x = x_ref[...]
    a = _dot(x, wgb_ref[...])
    b = _dot(x, wub_ref[...])
    o_ref[...] = _dot((a * jax.nn.sigmoid(a) * b).astype(BF16), wdb_ref[...])


def _shared_expert(hb, wg, wu, wd):
    T, Dm = hb.shape
    Ds = wg.shape[1]
    TM = _pick_tile(T, 512)
    full = lambda r, c: pl.BlockSpec((r, c), lambda i: (0, 0))
    return pl.pallas_call(
        _shared_kernel,
        grid=(T // TM,),
        in_specs=[pl.BlockSpec((TM, Dm), lambda i: (i, 0)), full(Dm, Ds), full(Dm, Ds), full(Ds, Dm)],
        out_specs=pl.BlockSpec((TM, Dm), lambda i: (i, 0)),
        out_shape=jax.ShapeDtypeStruct((T, Dm), F32),
        scratch_shapes=[pltpu.VMEM((Dm, Ds), BF16), pltpu.VMEM((Dm, Ds), BF16), pltpu.VMEM((Ds, Dm), BF16)],
        compiler_params=_cp("arbitrary"),
        name="moe_shared",
    )(hb, wg, wu, wd)


def _moe_ln(h, hb, hp, w_router, router_bias, w_gate, w_up, w_down, layer, ws_gate, ws_up, ws_down, g, b):
    T, Dm = h.shape
    eidx, wts, rank, cnt = _router(h, w_router, router_bias)
    counts = cnt[:, 0].astype(jnp.int32)
    pcounts = (counts + MOE_BLOCK - 1) // MOE_BLOCK * MOE_BLOCK
    pends = jnp.cumsum(pcounts)
    pstarts = pends - pcounts
    nblk = -(-T * TOP_K // MOE_BLOCK) + N_EXPERTS
    rows = nblk * MOE_BLOCK
    eid = jnp.arange(N_EXPERTS, dtype=jnp.int32)
    dest = jnp.sum(jnp.where(eidx[..., None] == eid, pstarts.astype(jnp.int32), 0), axis=-1) + rank
    TM = _pick_tile(T, 128)
    dest_tiles = dest.reshape(TOP_K, T // TM, TM).transpose(1, 0, 2)
    blk_start = jnp.arange(nblk, dtype=jnp.int32) * MOE_BLOCK
    blk_e = jnp.minimum(jnp.sum((pends[None, :] <= blk_start[:, None]).astype(jnp.int32), axis=1),
                        N_EXPERTS - 1)
    nact = (pends[-1:] // MOE_BLOCK).astype(jnp.int32)
    pad_id = jnp.arange(MOE_BLOCK, dtype=jnp.int32)[None, :]
    fill = jnp.where(pad_id < (pcounts - counts)[:, None], (pstarts + counts)[:, None] + pad_id, -1).reshape(-1)
    xs = _dispatch(hp, dest_tiles, fill, rows)
    yb = _experts(xs, blk_e, nact, w_gate, w_up, w_down, layer)
    shared = _shared_expert(hb, ws_gate, ws_up, ws_down)
    return _combine_ln(yb, dest_tiles, wts.T, h, shared, g, b)


def kernel(x, meta_tokens, ev_w_in, ev_w_out, ev_sink, od_w_in, od_conv_w, od_a_log, od_dt_bias, od_norm_w,
           od_w_out, ln_g, ln_b, w_router, router_bias, w_gate, w_up, w_down, ws_gate, ws_up, ws_down):
    B, S, Dm = x.shape
    Lp = S + FRONT
    meta = jnp.broadcast_to(meta_tokens.astype(x.dtype)[None], (B, N_META, Dm))
    h = jnp.concatenate([jnp.zeros((B, DEAD, Dm), x.dtype), meta, x], axis=1).reshape(B * Lp, Dm)
    hb = h.astype(BF16)
    for layer in range(DEPTH):
        i = layer // 2
        if layer % 2 == 0:
            mix = _even_mixer(hb, B, Lp, ev_w_in[i], ev_w_out[i], ev_sink[i])
        else:
            mix = _gdn_mixer(hb, B, Lp, od_w_in[i], od_conv_w[i], od_a_log[i], od_dt_bias[i],
                             od_norm_w[i], od_w_out[i])
        h, hb, hp = _residual_ln(h, mix, ln_g[layer, 0], ln_b[layer, 0])
        h, hb = _moe_ln(h, hb, hp, w_router[layer], router_bias[layer], w_gate, w_up, w_down, layer,
                        ws_gate[layer], ws_up[layer], ws_down[layer], ln_g[layer, 1], ln_b[layer, 1])
    return h.reshape(B, Lp, Dm)[:, FRONT:]
```

```python
import functools
import math

import numpy as np
import jax
import jax.numpy as jnp
from jax import lax
from jax.experimental import pallas as pl
from jax.experimental.pallas import tpu as pltpu

F32 = jnp.float32
BF16 = jnp.bfloat16

D_MODEL = 2048
DEPTH = 4
N_META = 16
FRONT = 128
DEAD = FRONT - N_META
RET_HEADS = 8
RET_HD = 128
RET_W = RET_HEADS * RET_HD
RET_CHUNK = 128
ATT_HEADS = 8
ATT_KV_HEADS = 2
ATT_GROUP = ATT_HEADS // ATT_KV_HEADS
ATT_HD = 128
ATT_BLOCK = 128
WINDOW = 128
EVEN_IN = 4 * RET_W + ATT_HEADS * ATT_HD + 2 * ATT_KV_HEADS * ATT_HD
GDN_QK_HEADS = 16
GDN_V_HEADS = 32
GDN_HD = 128
GDN_K_W = GDN_QK_HEADS * GDN_HD
GDN_V_W = GDN_V_HEADS * GDN_HD
GDN_CONV_CH = 2 * GDN_K_W + GDN_V_W
GDN_CHUNK = 64
GDN_HG = 32
CONV_WIDTH = 5
N_EXPERTS = 64
TOP_K = 8
N_GROUPS = 8
GROUP_SIZE = N_EXPERTS // N_GROUPS
TOPK_GROUPS = 4
D_EXPERT = 384
ROUTE_SCALE = 2.5
MOE_BLOCK = 256
ISSUE_GROUP = 4
DN_ALPHA = (2 * DEPTH) ** 0.25
LN_EPS = 1e-5
NORM_EPS = 1e-6
NEG = -1e30

VMEM_LIMIT = 56 * 2**20


def _cp(*sem, vmem=VMEM_LIMIT):
    return pltpu.CompilerParams(dimension_semantics=sem, vmem_limit_bytes=vmem)


def _dot(a, b):
    return jnp.dot(a, b, preferred_element_type=F32)


def _dot_nt(a, b):
    return lax.dot_general(a, b, (((1,), (1,)), ((), ())), preferred_element_type=F32)


def _dot_tn(a, b):
    return lax.dot_general(a, b, (((0,), (0,)), ((), ())), preferred_element_type=F32)


def _split3(a):
    hi = a.astype(BF16)
    r1 = a - hi.astype(F32)
    mid = r1.astype(BF16)
    lo = (r1 - mid.astype(F32)).astype(BF16)
    return hi, mid, lo


def _pick_tile(n, cap, mult=8):
    for t in range(min(cap, n), 0, -1):
        if n % t == 0 and t % mult == 0:
            return t
    return n


def _mm_kernel(*refs, ksplits):
    nx = len(ksplits)
    x_refs, w_ref, o_ref, wb_ref = refs[:nx], refs[nx], refs[nx + 1], refs[nx + 2]

    @pl.when(pl.program_id(1) == 0)
    def _():
        wb_ref[...] = w_ref[...].astype(BF16)

    acc = None
    k0 = 0
    for x_ref, kk in zip(x_refs, ksplits):
        part = _dot(x_ref[...].astype(BF16), wb_ref[k0:k0 + kk, :])
        acc = part if acc is None else acc + part
        k0 += kk
    o_ref[...] = acc.astype(o_ref.dtype)


def _matmul(xs, w, col0, ncols, tn, out_dtype=F32, tm_cap=512):
    M = xs[0].shape[0]
    ksplits = tuple(x.shape[1] for x in xs)
    K = sum(ksplits)
    assert w.shape[0] == K and ncols % tn == 0 and col0 % tn == 0
    tm = _pick_tile(M, tm_cap)
    in_specs = [pl.BlockSpec((tm, kk), lambda j, i: (i, 0)) for kk in ksplits]
    in_specs.append(pl.BlockSpec((K, tn), lambda j, i: (0, col0 // tn + j)))
    return pl.pallas_call(
        functools.partial(_mm_kernel, ksplits=ksplits),
        grid=(ncols // tn, M // tm),
        in_specs=in_specs,
        out_specs=pl.BlockSpec((tm, tn), lambda j, i: (i, j)),
        out_shape=jax.ShapeDtypeStruct((M, ncols), out_dtype),
        scratch_shapes=[pltpu.VMEM((K, tn), BF16)],
        compiler_params=_cp("arbitrary", "arbitrary"),
        name="matmul",
    )(*xs, w)


PACK_ROWS = D_MODEL // 256
HI_MASK = 0xFFFF0000


def _pack_rows(ref, x):
    n = x.shape[0]
    half = D_MODEL // 2
    for j in range(PACK_ROWS):
        lo = x[:, 128 * j:128 * (j + 1)].astype(BF16).astype(F32)
        hi = x[:, half + 128 * j:half + 128 * (j + 1)].astype(BF16).astype(F32)
        word = (pltpu.bitcast(lo, jnp.uint32) >> 16) | (pltpu.bitcast(hi, jnp.uint32) & jnp.uint32(HI_MASK))
        ref[pl.ds(j, n, stride=PACK_ROWS), :] = word


def _unpack_rows(ref, n):
    lo, hi = [], []
    for j in range(PACK_ROWS):
        word = ref[pl.ds(j, n, stride=PACK_ROWS), :]
        lo.append(pltpu.bitcast(word << 16, F32))
        hi.append(pltpu.bitcast(word & jnp.uint32(HI_MASK), F32))
    return lo + hi


def _layer_norm_rows(y, g, b):
    mu = jnp.mean(y, axis=-1, keepdims=True)
    d = y - mu
    var = jnp.mean(d * d, axis=-1, keepdims=True)
    return d * lax.rsqrt(var + LN_EPS) * g + b


def _ln_kernel(h_ref, a_ref, g_ref, b_ref, o_ref, ob_ref, op_ref):
    out = _layer_norm_rows(DN_ALPHA * h_ref[...] + a_ref[...], g_ref[...], b_ref[...])
    o_ref[...] = out
    ob_ref[...] = out.astype(BF16)
    _pack_rows(op_ref, out)


def _residual_ln(h, add, g, b):
    M, Dm = h.shape
    tm = _pick_tile(M, 256)
    row = pl.BlockSpec((tm, Dm), lambda i: (i, 0))
    vec = pl.BlockSpec((1, Dm), lambda i: (0, 0))
    return pl.pallas_call(
        _ln_kernel,
        grid=(M // tm,),
        in_specs=[row, row, vec, vec],
        out_specs=[row, row, pl.BlockSpec((tm * PACK_ROWS, 128), lambda i: (i, 0))],
        out_shape=[jax.ShapeDtypeStruct((M, Dm), F32), jax.ShapeDtypeStruct((M, Dm), BF16),
                   jax.ShapeDtypeStruct((M * PACK_ROWS, 128), jnp.uint32)],
        compiler_params=_cp("arbitrary"),
        name="residual_ln",
    )(h, add, g.reshape(1, Dm), b.reshape(1, Dm))


def _ret_tables(C):
    hh = np.arange(RET_HEADS, dtype=np.float64)
    lg = np.log(1.0 - 2.0 ** (-5.0 - hh))[:, None]
    pos = np.arange(C, dtype=np.float64)[None, :]
    vecs = np.stack([np.exp(lg * (pos + 1.0)),
                     np.exp(lg * (C - pos)),
                     np.exp(lg * (C - 1.0 - pos)),
                     np.exp(lg * pos),
                     np.exp(lg * C) * np.ones_like(pos)], axis=1)
    tab = np.broadcast_to(vecs[..., None], (RET_HEADS, 5, C, RET_HD))
    rel = np.abs(pos.T - pos)
    dsym = np.exp(lg[:, :, None] * rel[None])
    return jnp.asarray(tab, F32), jnp.asarray(dsym, F32)


def _ret_kernel(q_ref, k_ref, v_ref, g_ref, tab_ref, d_ref, o_ref, fst_ref, run_ref, *, nc, C):
    s = pl.program_id(1)
    fwd = s < nc
    c = jnp.where(fwd, s, 2 * nc - 1 - s)
    row = c * C + lax.broadcasted_iota(jnp.int32, (C, 1), 0)
    live = row >= DEAD
    heads = range(RET_HEADS)
    HD = RET_HD
    ks = [jnp.where(live, k_ref[0, :, h * HD:(h + 1) * HD] * (RET_HD ** -0.5), 0.0) for h in heads]
    vbs = [jnp.where(live, v_ref[0, :, h * HD:(h + 1) * HD], 0.0).astype(BF16) for h in heads]

    @pl.when(jnp.logical_or(s == 0, s == nc))
    def _():
        run_ref[...] = jnp.zeros_like(run_ref)

    @pl.when(fwd)
    def _():
        for h in heads:
            fst_ref[c, h] = run_ref[h].astype(BF16)
        upd = [_dot_tn((ks[h] * tab_ref[h, 2]).astype(BF16), vbs[h]) for h in heads]
        for h in heads:
            run_ref[h] = run_ref[h] * tab_ref[h, 4] + upd[h]

    @pl.when(jnp.logical_not(fwd))
    def _():
        qs = [q_ref[0, :, h * HD:(h + 1) * HD] for h in heads]
        sc = [(_dot_nt(qs[h].astype(BF16), ks[h].astype(BF16)) * d_ref[h]).astype(BF16) for h in heads]
        left = [_dot((qs[h] * tab_ref[h, 0]).astype(BF16), fst_ref[c, h]) for h in heads]
        right = [_dot((qs[h] * tab_ref[h, 1]).astype(BF16), run_ref[h].astype(BF16)) for h in heads]
        intra = [_dot(sc[h], vbs[h]) for h in heads]
        upd = [_dot_tn((ks[h] * tab_ref[h, 3]).astype(BF16), vbs[h]) for h in heads]
        for h in heads:
            run_ref[h] = run_ref[h] * tab_ref[h, 4] + upd[h]
            out = intra[h] + left[h] + right[h]
            mu = jnp.mean(out, axis=-1, keepdims=True)
            dlt = out - mu
            var = jnp.mean(dlt * dlt, axis=-1, keepdims=True)
            normed = dlt * lax.rsqrt(var + NORM_EPS)
            g = g_ref[0, :, h * HD:(h + 1) * HD]
            o_ref[0, :, h * HD:(h + 1) * HD] = (g * jax.nn.sigmoid(g) * normed).astype(o_ref.dtype)


def _retention(proj):
    B, Lp, _ = proj.shape
    C = RET_CHUNK
    nc = Lp // C
    tab, dsym = _ret_tables(C)

    def cidx(s):
        return jnp.where(s < nc, s, 2 * nc - 1 - s)

    def cidx_out(s):
        return jnp.where(s < nc, nc - 1, 2 * nc - 1 - s)

    return pl.pallas_call(
        functools.partial(_ret_kernel, nc=nc, C=C),
        grid=(B, 2 * nc),
        in_specs=[
            pl.BlockSpec((1, C, RET_W), lambda b, s: (b, cidx_out(s), 0)),
            pl.BlockSpec((1, C, RET_W), lambda b, s: (b, cidx(s), 1)),
            pl.BlockSpec((1, C, RET_W), lambda b, s: (b, cidx(s), 2)),
            pl.BlockSpec((1, C, RET_W), lambda b, s: (b, cidx_out(s), 3)),
            pl.BlockSpec((RET_HEADS, 5, C, RET_HD), lambda b, s: (0, 0, 0, 0)),
            pl.BlockSpec((RET_HEADS, C, C), lambda b, s: (0, 0, 0)),
        ],
        out_specs=pl.BlockSpec((1, C, RET_W), lambda b, s: (b, cidx_out(s), 0)),
        out_shape=jax.ShapeDtypeStruct((B, Lp, RET_W), BF16),
        scratch_shapes=[pltpu.VMEM((nc, RET_HEADS, RET_HD, RET_HD), BF16),
                        pltpu.VMEM((RET_HEADS, RET_HD, RET_HD), F32)],
        compiler_params=_cp("arbitrary", "arbitrary"),
        name="retention",
    )(proj, proj, proj, proj, tab, dsym)


def _att_kernel(sink_ref, slope_ref, q_ref, km_ref, kp_ref, kc_ref, kn_ref,
                vm_ref, vp_ref, vc_ref, vn_ref, o_ref, *, nb):
    j = pl.program_id(1)
    qb = pl.program_id(2)
    T = ATT_BLOCK
    ii = lax.broadcasted_iota(jnp.int32, (T, T), 0)
    jj = lax.broadcasted_iota(jnp.int32, (T, T), 1)
    meta_ok = jj >= DEAD
    pieces = []
    for off, k_ref, v_ref, ok in ((-T, kp_ref, vp_ref, qb >= 2),
                                  (0, kc_ref, vc_ref, qb >= 1),
                                  (T, kn_ref, vn_ref, qb + 1 <= nb)):
        dist = jnp.abs(jj + off - ii)
        pieces.append((dist.astype(F32), dist <= WINDOW, k_ref[0].astype(BF16), v_ref[0].astype(BF16), ok))
    km = km_ref[0].astype(BF16)
    vm = vm_ref[0].astype(BF16)
    vals = [vm] + [p[3] for p in pieces]
    groups = range(ATT_GROUP)
    qs = [(q_ref[0, :, g * ATT_HD:(g + 1) * ATT_HD] * (ATT_HD ** -0.5)).astype(BF16) for g in groups]
    raw = [[_dot_nt(qs[g], km)] + [_dot_nt(qs[g], p[2]) for p in pieces] for g in groups]
    es, dens = [], []
    for g in groups:
        slope = slope_ref[j, g]
        sink = sink_ref[j, g]
        s_list = [jnp.where(meta_ok, raw[g][0], NEG)]
        for (dist, inwin, _, _, ok), r in zip(pieces, raw[g][1:]):
            s_list.append(jnp.where(ok, jnp.where(inwin, r - slope * dist, NEG), NEG))
        m = jnp.full((T, 1), sink, F32)
        for sb in s_list:
            m = jnp.maximum(m, jnp.max(sb, axis=-1, keepdims=True))
        e_list = [jnp.exp(sb - m) for sb in s_list]
        den = jnp.exp(sink - m)
        for e in e_list:
            den = den + jnp.sum(e, axis=-1, keepdims=True)
        es.append([e.astype(BF16) for e in e_list])
        dens.append(den)
    pv = [[_dot(e, vb) for e, vb in zip(es[g], vals)] for g in groups]
    for g in groups:
        acc = pv[g][0] + pv[g][1] + pv[g][2] + pv[g][3]
        o_ref[0, :, g * ATT_HD:(g + 1) * ATT_HD] = (acc / dens[g]).astype(o_ref.dtype)


def _window_attention(proj, sink):
    B, Lp, _ = proj.shape
    nb = Lp // ATT_BLOCK - 1
    gw = ATT_GROUP * ATT_HD
    q0 = 4 * RET_W // gw
    k0 = (4 * RET_W + ATT_HEADS * ATT_HD) // ATT_HD
    v0 = k0 + ATT_KV_HEADS
    slopes = np.asarray(2.0 ** (-8.0 * (np.arange(ATT_HEADS) + 1.0) / ATT_HEADS), np.float32)
    smem = pl.BlockSpec(memory_space=pltpu.SMEM)

    def kv(c0, f):
        return pl.BlockSpec((1, ATT_BLOCK, ATT_HD), lambda b, j, t: (b, f(t), c0 + j))

    rows = [lambda t: 0, lambda t: jnp.maximum(t - 1, 0), lambda t: t, lambda t: jnp.minimum(t + 1, nb)]
    return pl.pallas_call(
        functools.partial(_att_kernel, nb=nb),
        grid=(B, ATT_KV_HEADS, nb + 1),
        in_specs=[smem, smem, pl.BlockSpec((1, ATT_BLOCK, gw), lambda b, j, t: (b, t, q0 + j))]
        + [kv(k0, f) for f in rows] + [kv(v0, f) for f in rows],
        out_specs=pl.BlockSpec((1, ATT_BLOCK, gw), lambda b, j, t: (b, t, j)),
        out_shape=jax.ShapeDtypeStruct((B, Lp, ATT_HEADS * ATT_HD), BF16),
        compiler_params=_cp("arbitrary", "arbitrary", "arbitrary"),
        name="window_attention",
    )(sink.astype(F32).reshape(ATT_KV_HEADS, ATT_GROUP), jnp.asarray(slopes).reshape(ATT_KV_HEADS, ATT_GROUP),
      *([proj] * 9))


CONV_HALO = 16


def _conv_shift_matrix(TT):
    half = CONV_WIDTH // 2
    t = np.arange(TT)[:, None]
    s = np.arange(TT + 2 * CONV_HALO)[None, :]
    blocks = [(s == t + CONV_HALO + tap - half) for tap in range(CONV_WIDTH) if tap != half]
    return jnp.asarray(np.concatenate(blocks, axis=0), BF16)


def _conv_kernel(xp_ref, x_ref, xn_ref, w_ref, sh_ref, o_ref, *, mode, TT, nt):
    t = pl.program_id(1)
    H = CONV_HALO
    rows = t * TT + lax.broadcasted_iota(jnp.int32, (TT, 1), 0)
    live = rows >= DEAD
    x = jnp.where(live, x_ref[0].astype(F32), 0.0)
    prow = t * TT - H + lax.broadcasted_iota(jnp.int32, (H, 1), 0)
    prev = jnp.where(prow >= DEAD, xp_ref[0].astype(F32), 0.0)
    nrow = (t + 1) * TT + lax.broadcasted_iota(jnp.int32, (H, 1), 0)
    nxt = jnp.where(jnp.logical_and(nrow >= DEAD, t < nt - 1), xn_ref[0].astype(F32), 0.0)
    staged = jnp.concatenate([prev.astype(BF16), x.astype(BF16), nxt.astype(BF16)], axis=0)
    shifted = _dot(sh_ref[...], staged)
    w = w_ref[...]
    half = CONV_WIDTH // 2
    acc = x * w[half:half + 1, :]
    blk = 0
    for tap in range(CONV_WIDTH):
        if tap == half:
            continue
        acc = acc + shifted[blk * TT:(blk + 1) * TT, :] * w[tap:tap + 1, :]
        blk += 1
    y = acc * jax.nn.sigmoid(acc)
    if mode in ("q", "k"):
        scale = GDN_HD ** -0.5 if mode == "q" else 1.0
        for a in range(y.shape[1] // GDN_HD):
            ya = y[:, a * GDN_HD:(a + 1) * GDN_HD]
            ya = ya * lax.rsqrt(jnp.sum(ya * ya, axis=-1, keepdims=True) + NORM_EPS)
            if mode == "q":
                ya = ya * scale
            o_ref[0, :, a * GDN_HD:(a + 1) * GDN_HD] = jnp.where(live, ya, 0.0).astype(o_ref.dtype)
    else:
        o_ref[0] = jnp.where(live, y, 0.0).astype(o_ref.dtype)


def _short_conv(qkv, conv_w, mode, col0, ncols):
    B, Lp, _ = qkv.shape
    H = CONV_HALO
    TT = _pick_tile(Lp, 256, mult=H)
    TC = 1024
    nt = Lp // TT
    c0 = col0 // TC
    shift = _conv_shift_matrix(TT)
    return pl.pallas_call(
        functools.partial(_conv_kernel, mode=mode, TT=TT, nt=nt),
        grid=(B, nt, ncols // TC),
        in_specs=[
            pl.BlockSpec((1, H, TC), lambda b, t, c: (b, jnp.maximum(t * (TT // H) - 1, 0), c0 + c)),
            pl.BlockSpec((1, TT, TC), lambda b, t, c: (b, t, c0 + c)),
            pl.BlockSpec((1, H, TC), lambda b, t, c: (b, jnp.minimum((t + 1) * (TT // H), Lp // H - 1), c0 + c)),
            pl.BlockSpec((CONV_WIDTH, TC), lambda b, t, c: (0, c0 + c)),
            pl.BlockSpec(shift.shape, lambda b, t, c: (0, 0)),
        ],
        out_specs=pl.BlockSpec((1, TT, TC), lambda b, t, c: (b, t, c)),
        out_shape=jax.ShapeDtypeStruct((B, Lp, ncols), BF16),
        compiler_params=_cp("arbitrary", "arbitrary", "arbitrary"),
        name="short_conv_" + mode,
    )(qkv, qkv, qkv, conv_w, shift)


def _gate_kernel(ba_ref, alog_ref, dtb_ref, o_ref, *, TT):
    t = pl.program_id(1)
    rows = t * TT + lax.broadcasted_iota(jnp.int32, (TT, 1), 0)
    lane = lax.broadcasted_iota(jnp.int32, (1, 4 * GDN_V_HEADS), 1)
    is_g = (lane // GDN_V_HEADS) % 2 == 1
    x = ba_ref[0]
    beta = jax.nn.sigmoid(x)
    xs = x + dtb_ref[...]
    softplus = jnp.maximum(xs, 0.0) + jnp.log(1.0 + jnp.exp(-jnp.abs(xs)))
    g = -jnp.exp(alog_ref[...]) * softplus
    o_ref[0] = jnp.where(rows >= DEAD, jnp.where(is_g, g, beta), 0.0)


def _gates(ba, a_log, dt_bias):
    B, Lp, W = ba.shape
    TT = _pick_tile(Lp, 1024)
    zeros = jnp.zeros((2, 1, GDN_V_HEADS), F32)
    alog = jnp.concatenate([zeros, a_log.astype(F32)[:, None, :]], axis=1).reshape(1, W)
    dtb = jnp.concatenate([zeros, dt_bias.astype(F32)[:, None, :]], axis=1).reshape(1, W)
    vec = pl.BlockSpec((1, W), lambda b, t: (0, 0))
    return pl.pallas_call(
        functools.partial(_gate_kernel, TT=TT),
        grid=(B, Lp // TT),
        in_specs=[pl.BlockSpec((1, TT, W), lambda b, t: (b, t, 0)), vec, vec],
        out_specs=pl.BlockSpec((1, TT, W), lambda b, t: (b, t, 0)),
        out_shape=jax.ShapeDtypeStruct((B, Lp, W), F32),
        compiler_params=_cp("arbitrary", "arbitrary"),
        name="gdn_gates",
    )(ba, alog, dtb)


def _gdn_kernel(*refs, rev, final):
    if final:
        q_ref, k_ref, v_ref, gc_ref, gr_ref, of_ref, z_ref, nw_ref, o_ref, s_ref = refs
    else:
        q_ref, k_ref, v_ref, gc_ref, gr_ref, o_ref, s_ref = refs
    C = GDN_CHUNK
    HG = GDN_HG

    @pl.when(pl.program_id(2) == 0)
    def _():
        s_ref[...] = jnp.zeros_like(s_ref)

    ii = lax.broadcasted_iota(jnp.int32, (C, C), 0)
    jj = lax.broadcasted_iota(jnp.int32, (C, C), 1)
    incl = (jj >= ii) if rev else (jj <= ii)
    strict = (jj > ii) if rev else (jj < ii)
    tri_col = jnp.where(incl, 1.0, 0.0).astype(BF16)
    tri_row = jnp.where((ii >= jj) if rev else (ii <= jj), 1.0, 0.0).astype(BF16)
    gcol = gc_ref[0, 0]
    grow = gr_ref[0, 0, 0]
    gc_col = sum(_dot(tri_col, p) for p in _split3(gcol))
    gc_row = sum(_dot(p, tri_row) for p in _split3(grow))
    last = 0 if rev else C - 1
    d0 = 2 * HG if rev else 0
    HD = GDN_HD
    heads = range(HG)
    pairs = range(HG // 2)
    qs = [q_ref[0, :, p * HD:(p + 1) * HD] for p in pairs]
    ks = [k_ref[0, :, p * HD:(p + 1) * HD] for p in pairs]
    kfs = [k.astype(F32) for k in ks]
    kq = [_dot_nt(jnp.concatenate([ks[p], qs[p]], axis=0), ks[p]) for p in pairs]
    gram = [r[:C] for r in kq]
    qk = [r[C:] for r in kq]
    beta = [gcol[:, d0 + h:d0 + h + 1] for h in heads]
    gcc = [gc_col[:, d0 + HG + h:d0 + HG + h + 1] for h in heads]
    gcr = [gc_row[d0 + HG + h:d0 + HG + h + 1, :] for h in heads]
    gtot = [g[last:last + 1, :] for g in gcc]
    decay = [jnp.where(incl, jnp.exp(jnp.where(incl, gcc[h] - gcr[h], 0.0)), 0.0) for h in heads]
    eg = [jnp.exp(g) for g in gcc]
    ms = [-jnp.where(strict, beta[h] * gram[h // 2] * decay[h], 0.0) for h in heads]
    pbs = [m.astype(BF16) for m in ms]
    ps = [_dot(pb, pb) for pb in pbs]
    pbs = [p.astype(BF16) for p in ps]
    for _ in range(4):
        res = [_dot(jnp.concatenate([ms[h].astype(BF16), pbs[h]], axis=0), pbs[h]) for h in heads]
        ms = [ms[h] + ps[h] + res[h][:C] for h in heads]
        ps = [r[C:] for r in res]
        pbs = [p.astype(BF16) for p in ps]
    ms = [ms[h] + ps[h] + _dot(ms[h].astype(BF16), pbs[h]) for h in heads]
    rhs = [jnp.concatenate([v_ref[0, :, h * HD:(h + 1) * HD].astype(F32) * beta[h],
                            kfs[h // 2] * (beta[h] * eg[h])], axis=1) for h in heads]
    sol = [rhs[h] + _dot(ms[h].astype(BF16), rhs[h].astype(BF16)) for h in heads]
    st = [s_ref[h] for h in heads]
    stb = [s.astype(BF16) for s in st]
    lhs = [jnp.concatenate([sol[h][:, HD:].astype(BF16), (qs[h // 2].astype(F32) * eg[h]).astype(BF16)], axis=0)
           for h in heads]
    ws = [_dot(lhs[h], stb[h]) for h in heads]
    vnb = [(sol[h][:, :HD] - ws[h][:C]).astype(BF16) for h in heads]
    outs = [ws[h][C:] + _dot((qk[h // 2] * decay[h]).astype(BF16), vnb[h]) for h in heads]
    upd = [_dot_tn((kfs[h // 2] * jnp.exp(gtot[h] - gcc[h])).astype(BF16), vnb[h]) for h in heads]
    for h in heads:
        s_ref[h] = st[h] * jnp.exp(gtot[h]) + upd[h]
        sl = slice(h * HD, (h + 1) * HD)
        if final:
            o = outs[h] + of_ref[0, :, sl]
            z = z_ref[0, :, sl]
            o = o * lax.rsqrt(jnp.mean(o * o, axis=-1, keepdims=True) + NORM_EPS) * nw_ref[...]
            o_ref[0, :, sl] = (o * (z * jax.nn.sigmoid(z))).astype(o_ref.dtype)
        else:
            o_ref[0, :, sl] = outs[h]


def _gdn_dir(q, k, v, gcol, grow, rev, o_fwd=None, z=None, norm_w=None):
    B, Lp, _ = v.shape
    C = GDN_CHUNK
    nc = Lp // C
    ng = GDN_V_HEADS // GDN_HG
    qw = GDN_HG // 2 * GDN_HD
    vw = GDN_HG * GDN_HD
    final = o_fwd is not None

    def cc(c):
        return nc - 1 - c if rev else c

    in_specs = [
        pl.BlockSpec((1, C, qw), lambda b, g, c: (b, cc(c), g)),
        pl.BlockSpec((1, C, qw), lambda b, g, c: (b, cc(c), g)),
        pl.BlockSpec((1, C, vw), lambda b, g, c: (b, cc(c), g)),
        pl.BlockSpec((1, 1, C, 4 * GDN_HG), lambda b, g, c: (b, g, cc(c), 0)),
        pl.BlockSpec((1, 1, 1, 4 * GDN_HG, C), lambda b, g, c: (b, g, cc(c), 0, 0)),
    ]
    args = [q, k, v, gcol, grow]
    if final:
        in_specs += [pl.BlockSpec((1, C, vw), lambda b, g, c: (b, cc(c), g)),
                     pl.BlockSpec((1, C, vw), lambda b, g, c: (b, cc(c), g)),
                     pl.BlockSpec((1, GDN_HD), lambda b, g, c: (0, 0))]
        args += [o_fwd, z, norm_w.astype(F32).reshape(1, GDN_HD)]
    return pl.pallas_call(
        functools.partial(_gdn_kernel, rev=rev, final=final),
        grid=(B, ng, nc),
        in_specs=in_specs,
        out_specs=pl.BlockSpec((1, C, vw), lambda b, g, c: (b, cc(c), g)),
        out_shape=jax.ShapeDtypeStruct((B, Lp, GDN_V_W), BF16 if final else F32),
        scratch_shapes=[pltpu.VMEM((GDN_HG, GDN_HD, GDN_HD), F32)],
        compiler_params=_cp("arbitrary", "arbitrary", "arbitrary"),
        name="gdn_bwd" if rev else "gdn_fwd",
    )(*args)


def _gdn_mixer(hb, B, Lp, w_in, conv_w, a_log, dt_bias, norm_w, w_out):
    qkv = _matmul([hb], w_in, 0, GDN_CONV_CH, 1024, out_dtype=BF16).reshape(B, Lp, GDN_CONV_CH)
    z = _matmul([hb], w_in, GDN_CONV_CH, GDN_V_W, 1024).reshape(B, Lp, GDN_V_W)
    ba = _matmul([hb], w_in, GDN_CONV_CH + GDN_V_W, 4 * GDN_V_HEADS, 128).reshape(B, Lp, 4 * GDN_V_HEADS)
    q = _short_conv(qkv, conv_w, "q", 0, GDN_K_W)
    k = _short_conv(qkv, conv_w, "k", GDN_K_W, GDN_K_W)
    v = _short_conv(qkv, conv_w, "v", 2 * GDN_K_W, GDN_V_W)
    gb = _gates(ba, a_log, dt_bias)
    ng = GDN_V_HEADS // GDN_HG
    nc = Lp // GDN_CHUNK
    gcol = gb.reshape(B, Lp, 4, ng, GDN_HG).transpose(0, 3, 1, 2, 4).reshape(B, ng, Lp, 4 * GDN_HG)
    grow = gcol.reshape(B, ng, nc, GDN_CHUNK, 4 * GDN_HG).transpose(0, 1, 2, 4, 3)
    o_f = _gdn_dir(q, k, v, gcol, grow, rev=False)
    o = _gdn_dir(q, k, v, gcol, grow, rev=True, o_fwd=o_f, z=z, norm_w=norm_w)
    return _matmul([o.reshape(B * Lp, GDN_V_W)], w_out, 0, D_MODEL, 512)


def _even_mixer(hb, B, Lp, w_in, w_out, sink):
    proj = _matmul([hb], w_in, 0, EVEN_IN, 512).reshape(B, Lp, EVEN_IN)
    ret = _retention(proj).reshape(B * Lp, RET_W)
    att = _window_attention(proj, sink).reshape(B * Lp, ATT_HEADS * ATT_HD)
    return _matmul([ret, att], w_out, 0, D_MODEL, 1024)


def _router_kernel(x_ref, wt_ref, bias_ref, eidx_ref, wts_ref, rank_ref, cnt_ref, carry_ref, *, TM):
    E = N_EXPERTS

    @pl.when(pl.program_id(0) == 0)
    def _():
        carry_ref[...] = jnp.zeros_like(carry_ref)

    xh, xm, xl = _split3(x_ref[...])
    wh, wm, wl = _split3(wt_ref[...])
    logits = (_dot_nt(wh, xh) + (_dot_nt(wh, xm) + _dot_nt(wm, xh))
              + (_dot_nt(wh, xl) + _dot_nt(wl, xh) + _dot_nt(wm, xm)))
    scores = jax.nn.sigmoid(logits)
    choice = scores + bias_ref[...]
    ninf = -jnp.inf
    io8 = lax.broadcasted_iota(jnp.int32, (GROUP_SIZE, TM), 0)
    gs_rows = []
    for g in range(N_GROUPS):
        cg = choice[g * GROUP_SIZE:(g + 1) * GROUP_SIZE, :]
        m1 = jnp.max(cg, axis=0, keepdims=True)
        i1 = jnp.min(jnp.where(cg == m1, io8, GROUP_SIZE), axis=0, keepdims=True)
        m2 = jnp.max(jnp.where(io8 == i1, ninf, cg), axis=0, keepdims=True)
        gs_rows.append(m1 + m2)
    gs = jnp.concatenate(gs_rows, axis=0)
    gsel = jnp.zeros((N_GROUPS, TM), jnp.int32)
    for _ in range(TOPK_GROUPS):
        m = jnp.max(gs, axis=0, keepdims=True)
        idx = jnp.min(jnp.where(gs == m, io8, N_GROUPS), axis=0, keepdims=True)
        hit = io8 == idx
        gsel = jnp.where(hit, 1, gsel)
        gs = jnp.where(hit, ninf, gs)
    masked = jnp.concatenate(
        [jnp.where(gsel[g:g + 1, :] > 0, choice[g * GROUP_SIZE:(g + 1) * GROUP_SIZE, :], ninf)
         for g in range(N_GROUPS)], axis=0)
    ioe = lax.broadcasted_iota(jnp.int32, (E, TM), 0)
    sel = jnp.zeros((E, TM), F32)
    idx_rows, w_rows = [], []
    for _ in range(TOP_K):
        m = jnp.max(masked, axis=0, keepdims=True)
        idx = jnp.min(jnp.where(masked == m, ioe, E), axis=0, keepdims=True)
        hit = ioe == idx
        idx_rows.append(idx)
        w_rows.append(jnp.sum(jnp.where(hit, scores, 0.0), axis=0, keepdims=True))
        sel = jnp.where(hit, 1.0, sel)
        masked = jnp.where(hit, ninf, masked)
    wsum = w_rows[0]
    for w in w_rows[1:]:
        wsum = wsum + w
    ti = lax.broadcasted_iota(jnp.int32, (TM, TM), 0)
    tj = lax.broadcasted_iota(jnp.int32, (TM, TM), 1)
    before = jnp.where(ti < tj, 1.0, 0.0).astype(BF16)
    rank = _dot(sel.astype(BF16), before) + carry_ref[:, 0:1]
    rank_rows = [jnp.sum(jnp.where(ioe == idx, rank, 0.0), axis=0, keepdims=True) for idx in idx_rows]
    eidx_ref[...] = jnp.concatenate(idx_rows, axis=0)
    wts_ref[...] = jnp.concatenate([w / wsum * ROUTE_SCALE for w in w_rows], axis=0)
    rank_ref[...] = jnp.concatenate(rank_rows, axis=0).astype(jnp.int32)
    carry_ref[...] = carry_ref[...] + jnp.sum(sel, axis=1, keepdims=True)
    cnt_ref[...] = carry_ref[...]


def _router(h, w_router, router_bias):
    T, Dm = h.shape
    TM = _pick_tile(T, 256)
    tok = pl.BlockSpec((TOP_K, TM), lambda i: (0, i))
    return pl.pallas_call(
        functools.partial(_router_kernel, TM=TM),
        grid=(T // TM,),
        in_specs=[pl.BlockSpec((TM, Dm), lambda i: (i, 0)),
                  pl.BlockSpec((N_EXPERTS, Dm), lambda i: (0, 0)),
                  pl.BlockSpec((N_EXPERTS, 1), lambda i: (0, 0))],
        out_specs=[tok, tok, tok, pl.BlockSpec((N_EXPERTS, 128), lambda i: (0, 0))],
        out_shape=[jax.ShapeDtypeStruct((TOP_K, T), jnp.int32), jax.ShapeDtypeStruct((TOP_K, T), F32),
                   jax.ShapeDtypeStruct((TOP_K, T), jnp.int32), jax.ShapeDtypeStruct((N_EXPERTS, 128), F32)],
        scratch_shapes=[pltpu.VMEM((N_EXPERTS, 128), F32)],
        compiler_params=_cp("arbitrary"),
        name="moe_router",
    )(h, w_router.T, router_bias.astype(F32).reshape(N_EXPERTS, 1))


def _dispatch_kernel(dest_hbm, fill_hbm, hp_hbm, xs_hbm, dest_smem, fill_smem, zrow_ref, xbuf_ref,
                     sem, tsem, isem, fsem, zsem, *, TM, nt, nfill):
    i = pl.program_id(0)
    cp = pltpu.make_async_copy(dest_hbm.at[i], dest_smem, isem)
    cp.start()

    R = PACK_ROWS

    def slot_rows(d):
        return xs_hbm.at[pl.ds(pl.multiple_of(d * R, R), R)]

    @pl.when(i == 0)
    def _():
        zrow_ref[...] = jnp.zeros_like(zrow_ref)
        fc = pltpu.make_async_copy(fill_hbm, fill_smem, fsem)
        fc.start()
        fc.wait()

        def body(r, carry):
            d = fill_smem[r]

            @pl.when(d >= 0)
            def _():
                pltpu.make_async_copy(zrow_ref, slot_rows(d), zsem).start()
            return carry

        lax.fori_loop(0, nfill, body, 0)

        def wbody(r, carry):
            @pl.when(fill_smem[r] >= 0)
            def _():
                pltpu.make_async_copy(zrow_ref, slot_rows(0), zsem).wait()
            return carry

        lax.fori_loop(0, nfill, wbody, 0)

    def tile_copy(tile, b):
        rows = pl.ds(pl.multiple_of(tile * (TM * R), TM * R), TM * R)
        return pltpu.make_async_copy(hp_hbm.at[rows], xbuf_ref.at[b], tsem.at[b])

    @pl.when(i == 0)
    def _():
        tile_copy(0, 0).start()
        if nt > 1:
            tile_copy(1, 1).start()

    cp.wait()
    cur = i % 3
    tile_copy(i, cur).wait()

    def issue(grp, carry):
        t0 = grp * ISSUE_GROUP
        slots = [[dest_smem[k, t0 + u] for k in range(TOP_K)] for u in range(ISSUE_GROUP)]
        for u in range(ISSUE_GROUP):
            src = xbuf_ref.at[cur, pl.ds(pl.multiple_of((t0 + u) * R, R), R)]
            for k in range(TOP_K):
                pltpu.make_async_copy(src, slot_rows(slots[u][k]), sem.at[cur]).start(priority=k % 2)
        return carry

    lax.fori_loop(0, TM // ISSUE_GROUP, issue, 0)

    def retire(b):
        def drain(t, carry):
            for k in range(TOP_K):
                pltpu.make_async_copy(xbuf_ref.at[b, pl.ds(0, R)], slot_rows(0), sem.at[b]).wait()
            return carry

        lax.fori_loop(0, TM, drain, 0)

    @pl.when(i > 0)
    def _():
        retire((i + 2) % 3)

    @pl.when(i + 2 < nt)
    def _():
        tile_copy(i + 2, (i + 2) % 3).start()

    @pl.when(i == nt - 1)
    def _():
        retire(cur)


def _dispatch(hp, dest_tiles, fill, rows):
    nt, _, TM = dest_tiles.shape
    nfill = fill.shape[0]
    return pl.pallas_call(
        functools.partial(_dispatch_kernel, TM=TM, nt=nt, nfill=nfill),
        grid=(nt,),
        in_specs=[pl.BlockSpec(memory_space=pl.ANY), pl.BlockSpec(memory_space=pl.ANY),
                  pl.BlockSpec(memory_space=pl.ANY)],
        out_specs=pl.BlockSpec(memory_space=pl.ANY),
        out_shape=jax.ShapeDtypeStruct((rows * PACK_ROWS, 128), jnp.uint32),
        scratch_shapes=[pltpu.SMEM((TOP_K, TM), jnp.int32), pltpu.SMEM((nfill,), jnp.int32),
                        pltpu.VMEM((PACK_ROWS, 128), jnp.uint32),
                        pltpu.VMEM((3, TM * PACK_ROWS, 128), jnp.uint32),
                        pltpu.SemaphoreType.DMA((3,)), pltpu.SemaphoreType.DMA((3,)),
                        pltpu.SemaphoreType.DMA, pltpu.SemaphoreType.DMA, pltpu.SemaphoreType.DMA],
        compiler_params=_cp("arbitrary"),
        name="moe_dispatch",
    )(dest_tiles, fill, hp)


def _expert_kernel(be_ref, na_ref, x_ref, wg_ref, wu_ref, wd_ref, o_ref, wgb_ref, wub_ref, wdb_ref):
    i = pl.program_id(0)
    active = i < na_ref[0]
    new_expert = jnp.logical_or(i == 0, be_ref[i] != be_ref[jnp.maximum(i - 1, 0)])

    @pl.when(jnp.logical_and(active, new_expert))
    def _():
        wgb_ref[...] = wg_ref[0, 0].astype(BF16)
        wub_ref[...] = wu_ref[0, 0].astype(BF16)
        wdb_ref[...] = wd_ref[0, 0].astype(BF16)

    @pl.when(active)
    def _():
        x = jnp.concatenate([c.astype(BF16) for c in _unpack_rows(x_ref, MOE_BLOCK)], axis=1)
        a = _dot(x, wgb_ref[...])
        b = _dot(x, wub_ref[...])
        hmid = (a * jax.nn.sigmoid(a) * b).astype(BF16)
        _pack_rows(o_ref, _dot(hmid, wdb_ref[...]))


def _experts(xs, blk_e, nact, w_gate, w_up, w_down, layer):
    Dm = D_MODEL
    rows = xs.shape[0] // PACK_ROWS
    nblk = rows // MOE_BLOCK
    pblock = MOE_BLOCK * PACK_ROWS

    def row(i, be, na):
        return (jnp.minimum(i, na[0] - 1), 0)

    def wsel(i, be, na):
        return (layer, be[jnp.minimum(i, na[0] - 1)], 0, 0)

    return pl.pallas_call(
        _expert_kernel,
        grid_spec=pltpu.PrefetchScalarGridSpec(
            num_scalar_prefetch=2,
            grid=(nblk,),
            in_specs=[pl.BlockSpec((pblock, 128), row),
                      pl.BlockSpec((1, 1, Dm, D_EXPERT), wsel),
                      pl.BlockSpec((1, 1, Dm, D_EXPERT), wsel),
                      pl.BlockSpec((1, 1, D_EXPERT, Dm), wsel)],
            out_specs=pl.BlockSpec((pblock, 128), row),
            scratch_shapes=[pltpu.VMEM((Dm, D_EXPERT), BF16), pltpu.VMEM((Dm, D_EXPERT), BF16),
                            pltpu.VMEM((D_EXPERT, Dm), BF16)],
        ),
        out_shape=jax.ShapeDtypeStruct((rows * PACK_ROWS, 128), jnp.uint32),
        compiler_params=_cp("arbitrary"),
        name="moe_experts",
    )(blk_e, nact, xs, w_gate, w_up, w_down)


def _combine_kernel(dest_hbm, yb_hbm, w_ref, h_ref, sh_ref, g_ref, b_ref, o_ref, ob_ref,
                    dest_smem, buf_ref, sem, isem, *, TM, nt):
    i = pl.program_id(0)
    R = PACK_ROWS

    def fetch(tile, slot):
        cp = pltpu.make_async_copy(dest_hbm.at[tile], dest_smem.at[slot], isem)
        cp.start()
        cp.wait()

        def issue(grp, carry):
            t0 = grp * ISSUE_GROUP
            slots = [[dest_smem[slot, k, t0 + u] for k in range(TOP_K)] for u in range(ISSUE_GROUP)]
            for u in range(ISSUE_GROUP):
                for k in range(TOP_K):
                    src = yb_hbm.at[pl.ds(pl.multiple_of(slots[u][k] * R, R), R)]
                    dst = buf_ref.at[slot, k, pl.ds(pl.multiple_of((t0 + u) * R, R), R)]
                    pltpu.make_async_copy(src, dst, sem.at[slot]).start(priority=k % 2)
            return carry

        lax.fori_loop(0, TM // ISSUE_GROUP, issue, 0)

    @pl.when(i == 0)
    def _():
        fetch(0, 0)

    @pl.when(i + 1 < nt)
    def _():
        fetch(i + 1, (i + 1) % 2)

    slot = i % 2

    def drain(t, carry):
        for k in range(TOP_K):
            pltpu.make_async_copy(yb_hbm.at[pl.ds(0, R)], buf_ref.at[slot, k, pl.ds(0, R)], sem.at[slot]).wait()
        return carry

    lax.fori_loop(0, TM, drain, 0)
    w = w_ref[...]
    acc = None
    for k in range(TOP_K):
        cols = _unpack_rows(buf_ref.at[slot, k], TM)
        wk = w[:, k:k + 1]
        acc = [c * wk for c in cols] if acc is None else [a + c * wk for a, c in zip(acc, cols)]
    routed = jnp.concatenate(acc, axis=1)
    out = _layer_norm_rows(DN_ALPHA * h_ref[...] + (routed + sh_ref[...]), g_ref[...], b_ref[...])
    o_ref[...] = out
    ob_ref[...] = out.astype(BF16)


def _combine_ln(yb, dest_tiles, wts_tok, h, shared, g, b):
    nt, _, TM = dest_tiles.shape
    T, Dm = h.shape
    row = pl.BlockSpec((TM, Dm), lambda i: (i, 0))
    vec = pl.BlockSpec((1, Dm), lambda i: (0, 0))
    hbm = pl.BlockSpec(memory_space=pl.ANY)
    return pl.pallas_call(
        functools.partial(_combine_kernel, TM=TM, nt=nt),
        grid=(nt,),
        in_specs=[hbm, hbm, pl.BlockSpec((TM, TOP_K), lambda i: (i, 0)), row, row, vec, vec],
        out_specs=[row, row],
        out_shape=[jax.ShapeDtypeStruct((T, Dm), F32), jax.ShapeDtypeStruct((T, Dm), BF16)],
        scratch_shapes=[pltpu.SMEM((2, TOP_K, TM), jnp.int32),
                        pltpu.VMEM((2, TOP_K, TM * PACK_ROWS, 128), jnp.uint32),
                        pltpu.SemaphoreType.DMA((2,)), pltpu.SemaphoreType.DMA],
        compiler_params=_cp("arbitrary"),
        name="moe_combine",
    )(dest_tiles, yb, wts_tok, h, shared, g.reshape(1, Dm), b.reshape(1, Dm))


def _shared_kernel(x_ref, wg_ref, wu_ref, wd_ref, o_ref, wgb_ref, wub_ref, wdb_ref):
    @pl.when(pl.program_id(0) == 0)
    def _():
        wgb_ref[...] = wg_ref[...].astype(BF16)
        wub_ref[...] = wu_ref[...].astype(BF16)
        wdb_ref[...] = wd_ref[...].astype(BF16)

    x = x_ref[...]
    a = _dot(x, wgb_ref[...])
    b = _dot(x, wub_ref[...])
    o_ref[...] = _dot((a * jax.nn.sigmoid(a) * b).astype(BF16), wdb_ref[...])


def _shared_expert(hb, wg, wu, wd):
    T, Dm = hb.shape
    Ds = wg.shape[1]
    TM = _pick_tile(T, 512)
    full = lambda r, c: pl.BlockSpec((r, c), lambda i: (0, 0))
    return pl.pallas_call(
        _shared_kernel,
        grid=(T // TM,),
        in_specs=[pl.BlockSpec((TM, Dm), lambda i: (i, 0)), full(Dm, Ds), full(Dm, Ds), full(Ds, Dm)],
        out_specs=pl.BlockSpec((TM, Dm), lambda i: (i, 0)),
        out_shape=jax.ShapeDtypeStruct((T, Dm), F32),
        scratch_shapes=[pltpu.VMEM((Dm, Ds), BF16), pltpu.VMEM((Dm, Ds), BF16), pltpu.VMEM((Ds, Dm), BF16)],
        compiler_params=_cp("arbitrary"),
        name="moe_shared",
    )(hb, wg, wu, wd)


def _moe_ln(h, hb, hp, w_router, router_bias, w_gate, w_up, w_down, layer, ws_gate, ws_up, ws_down, g, b):
    T, Dm = h.shape
    eidx, wts, rank, cnt = _router(h, w_router, router_bias)
    counts = cnt[:, 0].astype(jnp.int32)
    pcounts = (counts + MOE_BLOCK - 1) // MOE_BLOCK * MOE_BLOCK
    pends = jnp.cumsum(pcounts)
    pstarts = pends - pcounts
    nblk = -(-T * TOP_K // MOE_BLOCK) + N_EXPERTS
    rows = nblk * MOE_BLOCK
    eid = jnp.arange(N_EXPERTS, dtype=jnp.int32)
    dest = jnp.sum(jnp.where(eidx[..., None] == eid, pstarts.astype(jnp.int32), 0), axis=-1) + rank
    TM = _pick_tile(T, 128)
    dest_tiles = dest.reshape(TOP_K, T // TM, TM).transpose(1, 0, 2)
    blk_start = jnp.arange(nblk, dtype=jnp.int32) * MOE_BLOCK
    blk_e = jnp.minimum(jnp.sum((pends[None, :] <= blk_start[:, None]).astype(jnp.int32), axis=1),
                        N_EXPERTS - 1)
    nact = (pends[-1:] // MOE_BLOCK).astype(jnp.int32)
    pad_id = jnp.arange(MOE_BLOCK, dtype=jnp.int32)[None, :]
    fill = jnp.where(pad_id < (pcounts - counts)[:, None], (pstarts + counts)[:, None] + pad_id, -1).reshape(-1)
    xs = _dispatch(hp, dest_tiles, fill, rows)
    yb = _experts(xs, blk_e, nact, w_gate, w_up, w_down, layer)
    shared = _shared_expert(hb, ws_gate, ws_up, ws_down)
    return _combine_ln(yb, dest_tiles, wts.T, h, shared, g, b)


def kernel(x, meta_tokens, ev_w_in, ev_w_out, ev_sink, od_w_in, od_conv_w, od_a_log, od_dt_bias, od_norm_w,
           od_w_out, ln_g, ln_b, w_router, router_bias, w_gate, w_up, w_down, ws_gate, ws_up, ws_down):
    B, S, Dm = x.shape
    Lp = S + FRONT
    meta = jnp.broadcast_to(meta_tokens.astype(x.dtype)[None], (B, N_META, Dm))
    h = jnp.concatenate([jnp.zeros((B, DEAD, Dm), x.dtype), meta, x], axis=1).reshape(B * Lp, Dm)
    hb = h.astype(BF16)
    for layer in range(DEPTH):
        i = layer // 2
        if layer % 2 == 0:
            mix = _even_mixer(hb, B, Lp, ev_w_in[i], ev_w_out[i], ev_sink[i])
        else:
            mix = _gdn_mixer(hb, B, Lp, od_w_in[i], od_conv_w[i], od_a_log[i], od_dt_bias[i],
                             od_norm_w[i], od_w_out[i])
        h, hb, hp = _residual_ln(h, mix, ln_g[layer, 0], ln_b[layer, 0])
        h, hb = _moe_ln(h, hb, hp, w_router[layer], router_bias[layer], w_gate, w_up, w_down, layer,
                        ws_gate[layer], ws_up[layer], ws_down[layer], ln_g[layer, 1], ln_b[layer, 1])
    return h.reshape(B, Lp, Dm)[:, FRONT:]
```

```python
import functools
import math

import numpy as np
import jax
import jax.numpy as jnp
from jax import lax
from jax.experimental import pallas as pl
from jax.experimental.pallas import tpu as pltpu

F32 = jnp.float32
BF16 = jnp.bfloat16

D_MODEL = 2048
DEPTH = 4
N_META = 16
FRONT = 128
DEAD = FRONT - N_META
RET_HEADS = 8
RET_HD = 128
RET_W = RET_HEADS * RET_HD
RET_CHUNK = 128
ATT_HEADS = 8
ATT_KV_HEADS = 2
ATT_GROUP = ATT_HEADS // ATT_KV_HEADS
ATT_HD = 128
ATT_BLOCK = 128
WINDOW = 128
EVEN_IN = 4 * RET_W + ATT_HEADS * ATT_HD + 2 * ATT_KV_HEADS * ATT_HD
GDN_QK_HEADS = 16
GDN_V_HEADS = 32
GDN_HD = 128
GDN_K_W = GDN_QK_HEADS * GDN_HD
GDN_V_W = GDN_V_HEADS * GDN_HD
GDN_CONV_CH = 2 * GDN_K_W + GDN_V_W
GDN_CHUNK = 64
GDN_HG = 32
CONV_WIDTH = 5
N_EXPERTS = 64
TOP_K = 8
N_GROUPS = 8
GROUP_SIZE = N_EXPERTS // N_GROUPS
TOPK_GROUPS = 4
D_EXPERT = 384
ROUTE_SCALE = 2.5
MOE_BLOCK = 512
DENSE_TM = 1280
ISSUE_GROUP = 4
DN_ALPHA = (2 * DEPTH) ** 0.25
LN_EPS = 1e-5
NORM_EPS = 1e-6
NEG = -1e30

VMEM_LIMIT = 56 * 2**20


def _cp(*sem, vmem=VMEM_LIMIT):
    return pltpu.CompilerParams(dimension_semantics=sem, vmem_limit_bytes=vmem)


def _dot(a, b):
    return jnp.dot(a, b, preferred_element_type=F32)


def _dot_nt(a, b):
    return lax.dot_general(a, b, (((1,), (1,)), ((), ())), preferred_element_type=F32)


def _dot_tn(a, b):
    return lax.dot_general(a, b, (((0,), (0,)), ((), ())), preferred_element_type=F32)


def _split3(a):
    hi = a.astype(BF16)
    r1 = a - hi.astype(F32)
    mid = r1.astype(BF16)
    lo = (r1 - mid.astype(F32)).astype(BF16)
    return hi, mid, lo


def _pick_tile(n, cap, mult=8):
    for t in range(min(cap, n), 0, -1):
        if n % t == 0 and t % mult == 0:
            return t
    return n


def _mm_kernel(*refs, ksplits):
    nx = len(ksplits)
    x_refs, w_ref, o_ref, wb_ref = refs[:nx], refs[nx], refs[nx + 1], refs[nx + 2]

    @pl.when(pl.program_id(1) == 0)
    def _():
        wb_ref[...] = w_ref[...].astype(BF16)

    acc = None
    k0 = 0
    for x_ref, kk in zip(x_refs, ksplits):
        part = _dot(x_ref[...].astype(BF16), wb_ref[k0:k0 + kk, :])
        acc = part if acc is None else acc + part
        k0 += kk
    o_ref[...] = acc.astype(o_ref.dtype)


def _matmul(xs, w, col0, ncols, tn, out_dtype=F32, tm_cap=DENSE_TM):
    M = xs[0].shape[0]
    ksplits = tuple(x.shape[1] for x in xs)
    K = sum(ksplits)
    assert w.shape[0] == K and ncols % tn == 0 and col0 % tn == 0
    tm = _pick_tile(M, tm_cap)
    in_specs = [pl.BlockSpec((tm, kk), lambda j, i: (i, 0)) for kk in ksplits]
    in_specs.append(pl.BlockSpec((K, tn), lambda j, i: (0, col0 // tn + j)))
    return pl.pallas_call(
        functools.partial(_mm_kernel, ksplits=ksplits),
        grid=(ncols // tn, M // tm),
        in_specs=in_specs,
        out_specs=pl.BlockSpec((tm, tn), lambda j, i: (i, j)),
        out_shape=jax.ShapeDtypeStruct((M, ncols), out_dtype),
        scratch_shapes=[pltpu.VMEM((K, tn), BF16)],
        compiler_params=_cp("arbitrary", "arbitrary"),
        name="matmul",
    )(*xs, w)


PACK_ROWS = D_MODEL // 256
HI_MASK = 0xFFFF0000


def _pack_rows(ref, x):
    n = x.shape[0]
    half = D_MODEL // 2
    for j in range(PACK_ROWS):
        lo = x[:, 128 * j:128 * (j + 1)].astype(BF16).astype(F32)
        hi = x[:, half + 128 * j:half + 128 * (j + 1)].astype(BF16).astype(F32)
        word = (pltpu.bitcast(lo, jnp.uint32) >> 16) | (pltpu.bitcast(hi, jnp.uint32) & jnp.uint32(HI_MASK))
        ref[pl.ds(j, n, stride=PACK_ROWS), :] = word


def _unpack_rows(ref, n):
    lo, hi = [], []
    for j in range(PACK_ROWS):
        word = ref[pl.ds(j, n, stride=PACK_ROWS), :]
        lo.append(pltpu.bitcast(word << 16, F32))
        hi.append(pltpu.bitcast(word & jnp.uint32(HI_MASK), F32))
    return lo + hi


def _layer_norm_rows(y, g, b):
    mu = jnp.mean(y, axis=-1, keepdims=True)
    d = y - mu
    var = jnp.mean(d * d, axis=-1, keepdims=True)
    return d * lax.rsqrt(var + LN_EPS) * g + b


def _ln_kernel(h_ref, a_ref, g_ref, b_ref, o_ref, ob_ref, op_ref):
    out = _layer_norm_rows(DN_ALPHA * h_ref[...] + a_ref[...], g_ref[...], b_ref[...])
    o_ref[...] = out
    ob_ref[...] = out.astype(BF16)
    _pack_rows(op_ref, out)


def _residual_ln(h, add, g, b):
    M, Dm = h.shape
    tm = _pick_tile(M, 256)
    row = pl.BlockSpec((tm, Dm), lambda i: (i, 0))
    vec = pl.BlockSpec((1, Dm), lambda i: (0, 0))
    return pl.pallas_call(
        _ln_kernel,
        grid=(M // tm,),
        in_specs=[row, row, vec, vec],
        out_specs=[row, row, pl.BlockSpec((tm * PACK_ROWS, 128), lambda i: (i, 0))],
        out_shape=[jax.ShapeDtypeStruct((M, Dm), F32), jax.ShapeDtypeStruct((M, Dm), BF16),
                   jax.ShapeDtypeStruct((M * PACK_ROWS, 128), jnp.uint32)],
        compiler_params=_cp("arbitrary"),
        name="residual_ln",
    )(h, add, g.reshape(1, Dm), b.reshape(1, Dm))


def _ret_tables(C):
    hh = np.arange(RET_HEADS, dtype=np.float64)
    lg = np.log(1.0 - 2.0 ** (-5.0 - hh))[:, None]
    pos = np.arange(C, dtype=np.float64)[None, :]
    vecs = np.stack([np.exp(lg * (pos + 1.0)),
                     np.exp(lg * (C - pos)),
                     np.exp(lg * (C - 1.0 - pos)),
                     np.exp(lg * pos),
                     np.exp(lg * C) * np.ones_like(pos)], axis=1)
    tab = np.broadcast_to(vecs[..., None], (RET_HEADS, 5, C, RET_HD))
    rel = np.abs(pos.T - pos)
    dsym = np.exp(lg[:, :, None] * rel[None])
    return jnp.asarray(tab, F32), jnp.asarray(dsym, F32)


def _ret_kernel(q_ref, k_ref, v_ref, g_ref, tab_ref, d_ref, o_ref, fst_ref, run_ref, *, nc, C):
    s = pl.program_id(1)
    fwd = s < nc
    c = jnp.where(fwd, s, 2 * nc - 1 - s)
    row = c * C + lax.broadcasted_iota(jnp.int32, (C, 1), 0)
    live = row >= DEAD
    heads = range(RET_HEADS)
    HD = RET_HD
    ks = [jnp.where(live, k_ref[0, :, h * HD:(h + 1) * HD] * (RET_HD ** -0.5), 0.0) for h in heads]
    vbs = [jnp.where(live, v_ref[0, :, h * HD:(h + 1) * HD], 0.0).astype(BF16) for h in heads]

    @pl.when(jnp.logical_or(s == 0, s == nc))
    def _():
        run_ref[...] = jnp.zeros_like(run_ref)

    @pl.when(fwd)
    def _():
        for h in heads:
            fst_ref[c, h] = run_ref[h].astype(BF16)
        upd = [_dot_tn((ks[h] * tab_ref[h, 2]).astype(BF16), vbs[h]) for h in heads]
        for h in heads:
            run_ref[h] = run_ref[h] * tab_ref[h, 4] + upd[h]

    @pl.when(jnp.logical_not(fwd))
    def _():
        qs = [q_ref[0, :, h * HD:(h + 1) * HD] for h in heads]
        sc = [(_dot_nt(qs[h].astype(BF16), ks[h].astype(BF16)) * d_ref[h]).astype(BF16) for h in heads]
        left = [_dot((qs[h] * tab_ref[h, 0]).astype(BF16), fst_ref[c, h]) for h in heads]
        right = [_dot((qs[h] * tab_ref[h, 1]).astype(BF16), run_ref[h].astype(BF16)) for h in heads]
        intra = [_dot(sc[h], vbs[h]) for h in heads]
        upd = [_dot_tn((ks[h] * tab_ref[h, 3]).astype(BF16), vbs[h]) for h in heads]
        for h in heads:
            run_ref[h] = run_ref[h] * tab_ref[h, 4] + upd[h]
            out = intra[h] + left[h] + right[h]
            mu = jnp.mean(out, axis=-1, keepdims=True)
            dlt = out - mu
            var = jnp.mean(dlt * dlt, axis=-1, keepdims=True)
            normed = dlt * lax.rsqrt(var + NORM_EPS)
            g = g_ref[0, :, h * HD:(h + 1) * HD]
            o_ref[0, :, h * HD:(h + 1) * HD] = (g * jax.nn.sigmoid(g) * normed).astype(o_ref.dtype)


def _retention(proj):
    B, Lp, _ = proj.shape
    C = RET_CHUNK
    nc = Lp // C
    tab, dsym = _ret_tables(C)

    def cidx(s):
        return jnp.where(s < nc, s, 2 * nc - 1 - s)

    def cidx_out(s):
        return jnp.where(s < nc, nc - 1, 2 * nc - 1 - s)

    return pl.pallas_call(
        functools.partial(_ret_kernel, nc=nc, C=C),
        grid=(B, 2 * nc),
        in_specs=[
            pl.BlockSpec((1, C, RET_W), lambda b, s: (b, cidx_out(s), 0)),
            pl.BlockSpec((1, C, RET_W), lambda b, s: (b, cidx(s), 1)),
            pl.BlockSpec((1, C, RET_W), lambda b, s: (b, cidx(s), 2)),
            pl.BlockSpec((1, C, RET_W), lambda b, s: (b, cidx_out(s), 3)),
            pl.BlockSpec((RET_HEADS, 5, C, RET_HD), lambda b, s: (0, 0, 0, 0)),
            pl.BlockSpec((RET_HEADS, C, C), lambda b, s: (0, 0, 0)),
        ],
        out_specs=pl.BlockSpec((1, C, RET_W), lambda b, s: (b, cidx_out(s), 0)),
        out_shape=jax.ShapeDtypeStruct((B, Lp, RET_W), BF16),
        scratch_shapes=[pltpu.VMEM((nc, RET_HEADS, RET_HD, RET_HD), BF16),
                        pltpu.VMEM((RET_HEADS, RET_HD, RET_HD), F32)],
        compiler_params=_cp("arbitrary", "arbitrary"),
        name="retention",
    )(proj, proj, proj, proj, tab, dsym)


def _att_kernel(sink_ref, slope_ref, q_ref, km_ref, kp_ref, kc_ref, kn_ref,
                vm_ref, vp_ref, vc_ref, vn_ref, o_ref, *, nb):
    j = pl.program_id(1)
    qb = pl.program_id(2)
    T = ATT_BLOCK
    ii = lax.broadcasted_iota(jnp.int32, (T, T), 0)
    jj = lax.broadcasted_iota(jnp.int32, (T, T), 1)
    meta_ok = jj >= DEAD
    pieces = []
    for off, k_ref, v_ref, ok in ((-T, kp_ref, vp_ref, qb >= 2),
                                  (0, kc_ref, vc_ref, qb >= 1),
                                  (T, kn_ref, vn_ref, qb + 1 <= nb)):
        dist = jnp.abs(jj + off - ii)
        pieces.append((dist.astype(F32), dist <= WINDOW, k_ref[0].astype(BF16), v_ref[0].astype(BF16), ok))
    km = km_ref[0].astype(BF16)
    vm = vm_ref[0].astype(BF16)
    vals = [vm] + [p[3] for p in pieces]
    groups = range(ATT_GROUP)
    qs = [(q_ref[0, :, g * ATT_HD:(g + 1) * ATT_HD] * (ATT_HD ** -0.5)).astype(BF16) for g in groups]
    raw = [[_dot_nt(qs[g], km)] + [_dot_nt(qs[g], p[2]) for p in pieces] for g in groups]
    es, dens = [], []
    for g in groups:
        slope = slope_ref[j, g]
        sink = sink_ref[j, g]
        s_list = [jnp.where(meta_ok, raw[g][0], NEG)]
        for (dist, inwin, _, _, ok), r in zip(pieces, raw[g][1:]):
            s_list.append(jnp.where(ok, jnp.where(inwin, r - slope * dist, NEG), NEG))
        m = jnp.full((T, 1), sink, F32)
        for sb in s_list:
            m = jnp.maximum(m, jnp.max(sb, axis=-1, keepdims=True))
        e_list = [jnp.exp(sb - m) for sb in s_list]
        den = jnp.exp(sink - m)
        for e in e_list:
            den = den + jnp.sum(e, axis=-1, keepdims=True)
        es.append([e.astype(BF16) for e in e_list])
        dens.append(den)
    pv = [[_dot(e, vb) for e, vb in zip(es[g], vals)] for g in groups]
    for g in groups:
        acc = pv[g][0] + pv[g][1] + pv[g][2] + pv[g][3]
        o_ref[0, :, g * ATT_HD:(g + 1) * ATT_HD] = (acc / dens[g]).astype(o_ref.dtype)


def _window_attention(proj, sink):
    B, Lp, _ = proj.shape
    nb = Lp // ATT_BLOCK - 1
    gw = ATT_GROUP * ATT_HD
    q0 = 4 * RET_W // gw
    k0 = (4 * RET_W + ATT_HEADS * ATT_HD) // ATT_HD
    v0 = k0 + ATT_KV_HEADS
    slopes = np.asarray(2.0 ** (-8.0 * (np.arange(ATT_HEADS) + 1.0) / ATT_HEADS), np.float32)
    smem = pl.BlockSpec(memory_space=pltpu.SMEM)

    def kv(c0, f):
        return pl.BlockSpec((1, ATT_BLOCK, ATT_HD), lambda b, j, t: (b, f(t), c0 + j))

    rows = [lambda t: 0, lambda t: jnp.maximum(t - 1, 0), lambda t: t, lambda t: jnp.minimum(t + 1, nb)]
    return pl.pallas_call(
        functools.partial(_att_kernel, nb=nb),
        grid=(B, ATT_KV_HEADS, nb + 1),
        in_specs=[smem, smem, pl.BlockSpec((1, ATT_BLOCK, gw), lambda b, j, t: (b, t, q0 + j))]
        + [kv(k0, f) for f in rows] + [kv(v0, f) for f in rows],
        out_specs=pl.BlockSpec((1, ATT_BLOCK, gw), lambda b, j, t: (b, t, j)),
        out_shape=jax.ShapeDtypeStruct((B, Lp, ATT_HEADS * ATT_HD), BF16),
        compiler_params=_cp("arbitrary", "arbitrary", "arbitrary"),
        name="window_attention",
    )(sink.astype(F32).reshape(ATT_KV_HEADS, ATT_GROUP), jnp.asarray(slopes).reshape(ATT_KV_HEADS, ATT_GROUP),
      *([proj] * 9))


CONV_HALO = 16


def _conv_shift_matrix(TT):
    half = CONV_WIDTH // 2
    t = np.arange(TT)[:, None]
    s = np.arange(TT + 2 * CONV_HALO)[None, :]
    blocks = [(s == t + CONV_HALO + tap - half) for tap in range(CONV_WIDTH) if tap != half]
    return jnp.asarray(np.concatenate(blocks, axis=0), BF16)


def _conv_kernel(xp_ref, x_ref, xn_ref, w_ref, sh_ref, o_ref, *, mode, TT, nt):
    t = pl.program_id(1)
    H = CONV_HALO
    rows = t * TT + lax.broadcasted_iota(jnp.int32, (TT, 1), 0)
    live = rows >= DEAD
    x = jnp.where(live, x_ref[0].astype(F32), 0.0)
    prow = t * TT - H + lax.broadcasted_iota(jnp.int32, (H, 1), 0)
    prev = jnp.where(prow >= DEAD, xp_ref[0].astype(F32), 0.0)
    nrow = (t + 1) * TT + lax.broadcasted_iota(jnp.int32, (H, 1), 0)
    nxt = jnp.where(jnp.logical_and(nrow >= DEAD, t < nt - 1), xn_ref[0].astype(F32), 0.0)
    staged = jnp.concatenate([prev.astype(BF16), x.astype(BF16), nxt.astype(BF16)], axis=0)
    shifted = _dot(sh_ref[...], staged)
    w = w_ref[...]
    half = CONV_WIDTH // 2
    acc = x * w[half:half + 1, :]
    blk = 0
    for tap in range(CONV_WIDTH):
        if tap == half:
            continue
        acc = acc + shifted[blk * TT:(blk + 1) * TT, :] * w[tap:tap + 1, :]
        blk += 1
    y = acc * jax.nn.sigmoid(acc)
    if mode in ("q", "k"):
        scale = GDN_HD ** -0.5 if mode == "q" else 1.0
        for a in range(y.shape[1] // GDN_HD):
            ya = y[:, a * GDN_HD:(a + 1) * GDN_HD]
            ya = ya * lax.rsqrt(jnp.sum(ya * ya, axis=-1, keepdims=True) + NORM_EPS)
            if mode == "q":
                ya = ya * scale
            o_ref[0, :, a * GDN_HD:(a + 1) * GDN_HD] = jnp.where(live, ya, 0.0).astype(o_ref.dtype)
    else:
        o_ref[0] = jnp.where(live, y, 0.0).astype(o_ref.dtype)


def _short_conv(qkv, conv_w, mode, col0, ncols):
    B, Lp, _ = qkv.shape
    H = CONV_HALO
    TT = _pick_tile(Lp, 256, mult=H)
    TC = 1024
    nt = Lp // TT
    c0 = col0 // TC
    shift = _conv_shift_matrix(TT)
    return pl.pallas_call(
        functools.partial(_conv_kernel, mode=mode, TT=TT, nt=nt),
        grid=(B, nt, ncols // TC),
        in_specs=[
            pl.BlockSpec((1, H, TC), lambda b, t, c: (b, jnp.maximum(t * (TT // H) - 1, 0), c0 + c)),
            pl.BlockSpec((1, TT, TC), lambda b, t, c: (b, t, c0 + c)),
            pl.BlockSpec((1, H, TC), lambda b, t, c: (b, jnp.minimum((t + 1) * (TT // H), Lp // H - 1), c0 + c)),
            pl.BlockSpec((CONV_WIDTH, TC), lambda b, t, c: (0, c0 + c)),
            pl.BlockSpec(shift.shape, lambda b, t, c: (0, 0)),
        ],
        out_specs=pl.BlockSpec((1, TT, TC), lambda b, t, c: (b, t, c)),
        out_shape=jax.ShapeDtypeStruct((B, Lp, ncols), BF16),
        compiler_params=_cp("arbitrary", "arbitrary", "arbitrary"),
        name="short_conv_" + mode,
    )(qkv, qkv, qkv, conv_w, shift)


def _gate_kernel(ba_ref, alog_ref, dtb_ref, o_ref, *, TT):
    t = pl.program_id(1)
    rows = t * TT + lax.broadcasted_iota(jnp.int32, (TT, 1), 0)
    lane = lax.broadcasted_iota(jnp.int32, (1, 4 * GDN_V_HEADS), 1)
    is_g = (lane // GDN_V_HEADS) % 2 == 1
    x = ba_ref[0]
    beta = jax.nn.sigmoid(x)
    xs = x + dtb_ref[...]
    softplus = jnp.maximum(xs, 0.0) + jnp.log(1.0 + jnp.exp(-jnp.abs(xs)))
    g = -jnp.exp(alog_ref[...]) * softplus
    o_ref[0] = jnp.where(rows >= DEAD, jnp.where(is_g, g, beta), 0.0)


def _gates(ba, a_log, dt_bias):
    B, Lp, W = ba.shape
    TT = _pick_tile(Lp, 1024)
    zeros = jnp.zeros((2, 1, GDN_V_HEADS), F32)
    alog = jnp.concatenate([zeros, a_log.astype(F32)[:, None, :]], axis=1).reshape(1, W)
    dtb = jnp.concatenate([zeros, dt_bias.astype(F32)[:, None, :]], axis=1).reshape(1, W)
    vec = pl.BlockSpec((1, W), lambda b, t: (0, 0))
    return pl.pallas_call(
        functools.partial(_gate_kernel, TT=TT),
        grid=(B, Lp // TT),
        in_specs=[pl.BlockSpec((1, TT, W), lambda b, t: (b, t, 0)), vec, vec],
        out_specs=pl.BlockSpec((1, TT, W), lambda b, t: (b, t, 0)),
        out_shape=jax.ShapeDtypeStruct((B, Lp, W), F32),
        compiler_params=_cp("arbitrary", "arbitrary"),
        name="gdn_gates",
    )(ba, alog, dtb)


def _gdn_kernel(*refs, rev, final):
    if final:
        q_ref, k_ref, v_ref, gc_ref, gr_ref, of_ref, z_ref, nw_ref, o_ref, s_ref = refs
    else:
        q_ref, k_ref, v_ref, gc_ref, gr_ref, o_ref, s_ref = refs
    C = GDN_CHUNK
    HG = GDN_HG

    @pl.when(pl.program_id(2) == 0)
    def _():
        s_ref[...] = jnp.zeros_like(s_ref)

    ii = lax.broadcasted_iota(jnp.int32, (C, C), 0)
    jj = lax.broadcasted_iota(jnp.int32, (C, C), 1)
    incl = (jj >= ii) if rev else (jj <= ii)
    strict = (jj > ii) if rev else (jj < ii)
    tri_col = jnp.where(incl, 1.0, 0.0).astype(BF16)
    tri_row = jnp.where((ii >= jj) if rev else (ii <= jj), 1.0, 0.0).astype(BF16)
    gcol = gc_ref[0, 0]
    grow = gr_ref[0, 0, 0]
    gc_col = sum(_dot(tri_col, p) for p in _split3(gcol))
    gc_row = sum(_dot(p, tri_row) for p in _split3(grow))
    last = 0 if rev else C - 1
    d0 = 2 * HG if rev else 0
    HD = GDN_HD
    heads = range(HG)
    pairs = range(HG // 2)
    qs = [q_ref[0, :, p * HD:(p + 1) * HD] for p in pairs]
    ks = [k_ref[0, :, p * HD:(p + 1) * HD] for p in pairs]
    kfs = [k.astype(F32) for k in ks]
    kq = [_dot_nt(jnp.concatenate([ks[p], qs[p]], axis=0), ks[p]) for p in pairs]
    gram = [r[:C] for r in kq]
    qk = [r[C:] for r in kq]
    beta = [gcol[:, d0 + h:d0 + h + 1] for h in heads]
    gcc = [gc_col[:, d0 + HG + h:d0 + HG + h + 1] for h in heads]
    gcr = [gc_row[d0 + HG + h:d0 + HG + h + 1, :] for h in heads]
    gtot = [g[last:last + 1, :] for g in gcc]
    decay = [jnp.where(incl, jnp.exp(jnp.where(incl, gcc[h] - gcr[h], 0.0)), 0.0) for h in heads]
    eg = [jnp.exp(g) for g in gcc]
    ms = [-jnp.where(strict, beta[h] * gram[h // 2] * decay[h], 0.0) for h in heads]
    pbs = [m.astype(BF16) for m in ms]
    ps = [_dot(pb, pb) for pb in pbs]
    pbs = [p.astype(BF16) for p in ps]
    for _ in range(4):
        res = [_dot(jnp.concatenate([ms[h].astype(BF16), pbs[h]], axis=0), pbs[h]) for h in heads]
        ms = [ms[h] + ps[h] + res[h][:C] for h in heads]
        ps = [r[C:] for r in res]
        pbs = [p.astype(BF16) for p in ps]
    ms = [ms[h] + ps[h] + _dot(ms[h].astype(BF16), pbs[h]) for h in heads]
    rhs = [jnp.concatenate([v_ref[0, :, h * HD:(h + 1) * HD].astype(F32) * beta[h],
                            kfs[h // 2] * (beta[h] * eg[h])], axis=1) for h in heads]
    sol = [rhs[h] + _dot(ms[h].astype(BF16), rhs[h].astype(BF16)) for h in heads]
    st = [s_ref[h] for h in heads]
    stb = [s.astype(BF16) for s in st]
    lhs = [jnp.concatenate([sol[h][:, HD:].astype(BF16), (qs[h // 2].astype(F32) * eg[h]).astype(BF16)], axis=0)
           for h in heads]
    ws = [_dot(lhs[h], stb[h]) for h in heads]
    vnb = [(sol[h][:, :HD] - ws[h][:C]).astype(BF16) for h in heads]
    outs = [ws[h][C:] + _dot((qk[h // 2] * decay[h]).astype(BF16), vnb[h]) for h in heads]
    upd = [_dot_tn((kfs[h // 2] * jnp.exp(gtot[h] - gcc[h])).astype(BF16), vnb[h]) for h in heads]
    for h in heads:
        s_ref[h] = st[h] * jnp.exp(gtot[h]) + upd[h]
        sl = slice(h * HD, (h + 1) * HD)
        if final:
            o = outs[h] + of_ref[0, :, sl]
            z = z_ref[0, :, sl]
            o = o * lax.rsqrt(jnp.mean(o * o, axis=-1, keepdims=True) + NORM_EPS) * nw_ref[...]
            o_ref[0, :, sl] = (o * (z * jax.nn.sigmoid(z))).astype(o_ref.dtype)
        else:
            o_ref[0, :, sl] = outs[h]


def _gdn_dir(q, k, v, gcol, grow, rev, o_fwd=None, z=None, norm_w=None):
    B, Lp, _ = v.shape
    C = GDN_CHUNK
    nc = Lp // C
    ng = GDN_V_HEADS // GDN_HG
    qw = GDN_HG // 2 * GDN_HD
    vw = GDN_HG * GDN_HD
    final = o_fwd is not None

    def cc(c):
        return nc - 1 - c if rev else c

    in_specs = [
        pl.BlockSpec((1, C, qw), lambda b, g, c: (b, cc(c), g)),
        pl.BlockSpec((1, C, qw), lambda b, g, c: (b, cc(c), g)),
        pl.BlockSpec((1, C, vw), lambda b, g, c: (b, cc(c), g)),
        pl.BlockSpec((1, 1, C, 4 * GDN_HG), lambda b, g, c: (b, g, cc(c), 0)),
        pl.BlockSpec((1, 1, 1, 4 * GDN_HG, C), lambda b, g, c: (b, g, cc(c), 0, 0)),
    ]
    args = [q, k, v, gcol, grow]
    if final:
        in_specs += [pl.BlockSpec((1, C, vw), lambda b, g, c: (b, cc(c), g)),
                     pl.BlockSpec((1, C, vw), lambda b, g, c: (b, cc(c), g)),
                     pl.BlockSpec((1, GDN_HD), lambda b, g, c: (0, 0))]
        args += [o_fwd, z, norm_w.astype(F32).reshape(1, GDN_HD)]
    return pl.pallas_call(
        functools.partial(_gdn_kernel, rev=rev, final=final),
        grid=(B, ng, nc),
        in_specs=in_specs,
        out_specs=pl.BlockSpec((1, C, vw), lambda b, g, c: (b, cc(c), g)),
        out_shape=jax.ShapeDtypeStruct((B, Lp, GDN_V_W), BF16 if final else F32),
        scratch_shapes=[pltpu.VMEM((GDN_HG, GDN_HD, GDN_HD), F32)],
        compiler_params=_cp("arbitrary", "arbitrary", "arbitrary"),
        name="gdn_bwd" if rev else "gdn_fwd",
    )(*args)


def _gdn_mixer(hb, B, Lp, w_in, conv_w, a_log, dt_bias, norm_w, w_out):
    qkv = _matmul([hb], w_in, 0, GDN_CONV_CH, 1024, out_dtype=BF16).reshape(B, Lp, GDN_CONV_CH)
    z = _matmul([hb], w_in, GDN_CONV_CH, GDN_V_W, 1024).reshape(B, Lp, GDN_V_W)
    ba = _matmul([hb], w_in, GDN_CONV_CH + GDN_V_W, 4 * GDN_V_HEADS, 128).reshape(B, Lp, 4 * GDN_V_HEADS)
    q = _short_conv(qkv, conv_w, "q", 0, GDN_K_W)
    k = _short_conv(qkv, conv_w, "k", GDN_K_W, GDN_K_W)
    v = _short_conv(qkv, conv_w, "v", 2 * GDN_K_W, GDN_V_W)
    gb = _gates(ba, a_log, dt_bias)
    ng = GDN_V_HEADS // GDN_HG
    nc = Lp // GDN_CHUNK
    gcol = gb.reshape(B, Lp, 4, ng, GDN_HG).transpose(0, 3, 1, 2, 4).reshape(B, ng, Lp, 4 * GDN_HG)
    grow = gcol.reshape(B, ng, nc, GDN_CHUNK, 4 * GDN_HG).transpose(0, 1, 2, 4, 3)
    o_f = _gdn_dir(q, k, v, gcol, grow, rev=False)
    o = _gdn_dir(q, k, v, gcol, grow, rev=True, o_fwd=o_f, z=z, norm_w=norm_w)
    return _matmul([o.reshape(B * Lp, GDN_V_W)], w_out, 0, D_MODEL, 512)


def _even_mixer(hb, B, Lp, w_in, w_out, sink):
    proj = _matmul([hb], w_in, 0, EVEN_IN, 512).reshape(B, Lp, EVEN_IN)
    ret = _retention(proj).reshape(B * Lp, RET_W)
    att = _window_attention(proj, sink).reshape(B * Lp, ATT_HEADS * ATT_HD)
    return _matmul([ret, att], w_out, 0, D_MODEL, 1024)


def _router_kernel(x_ref, wt_ref, bias_ref, eidx_ref, wts_ref, rank_ref, cnt_ref, carry_ref, *, TM):
    E = N_EXPERTS

    @pl.when(pl.program_id(0) == 0)
    def _():
        carry_ref[...] = jnp.zeros_like(carry_ref)

    xh, xm, xl = _split3(x_ref[...])
    wh, wm, wl = _split3(wt_ref[...])
    logits = (_dot_nt(wh, xh) + (_dot_nt(wh, xm) + _dot_nt(wm, xh))
              + (_dot_nt(wh, xl) + _dot_nt(wl, xh) + _dot_nt(wm, xm)))
    scores = jax.nn.sigmoid(logits)
    choice = scores + bias_ref[...]
    ninf = -jnp.inf
    io8 = lax.broadcasted_iota(jnp.int32, (GROUP_SIZE, TM), 0)
    gs_rows = []
    for g in range(N_GROUPS):
        cg = choice[g * GROUP_SIZE:(g + 1) * GROUP_SIZE, :]
        m1 = jnp.max(cg, axis=0, keepdims=True)
        i1 = jnp.min(jnp.where(cg == m1, io8, GROUP_SIZE), axis=0, keepdims=True)
        m2 = jnp.max(jnp.where(io8 == i1, ninf, cg), axis=0, keepdims=True)
        gs_rows.append(m1 + m2)
    gs = jnp.concatenate(gs_rows, axis=0)
    gsel = jnp.zeros((N_GROUPS, TM), jnp.int32)
    for _ in range(TOPK_GROUPS):
        m = jnp.max(gs, axis=0, keepdims=True)
        idx = jnp.min(jnp.where(gs == m, io8, N_GROUPS), axis=0, keepdims=True)
        hit = io8 == idx
        gsel = jnp.where(hit, 1, gsel)
        gs = jnp.where(hit, ninf, gs)
    masked = jnp.concatenate(
        [jnp.where(gsel[g:g + 1, :] > 0, choice[g * GROUP_SIZE:(g + 1) * GROUP_SIZE, :], ninf)
         for g in range(N_GROUPS)], axis=0)
    ioe = lax.broadcasted_iota(jnp.int32, (E, TM), 0)
    sel = jnp.zeros((E, TM), F32)
    idx_rows, w_rows = [], []
    for _ in range(TOP_K):
        m = jnp.max(masked, axis=0, keepdims=True)
        idx = jnp.min(jnp.where(masked == m, ioe, E), axis=0, keepdims=True)
        hit = ioe == idx
        idx_rows.append(idx)
        w_rows.append(jnp.sum(jnp.where(hit, scores, 0.0), axis=0, keepdims=True))
        sel = jnp.where(hit, 1.0, sel)
        masked = jnp.where(hit, ninf, masked)
    wsum = w_rows[0]
    for w in w_rows[1:]:
        wsum = wsum + w
    ti = lax.broadcasted_iota(jnp.int32, (TM, TM), 0)
    tj = lax.broadcasted_iota(jnp.int32, (TM, TM), 1)
    before = jnp.where(ti < tj, 1.0, 0.0).astype(BF16)
    rank = _dot(sel.astype(BF16), before) + carry_ref[:, 0:1]
    rank_rows = [jnp.sum(jnp.where(ioe == idx, rank, 0.0), axis=0, keepdims=True) for idx in idx_rows]
    eidx_ref[...] = jnp.concatenate(idx_rows, axis=0)
    wts_ref[...] = jnp.concatenate([w / wsum * ROUTE_SCALE for w in w_rows], axis=0)
    rank_ref[...] = jnp.concatenate(rank_rows, axis=0).astype(jnp.int32)
    carry_ref[...] = carry_ref[...] + jnp.sum(sel, axis=1, keepdims=True)
    cnt_ref[...] = carry_ref[...]


def _router(h, w_router, router_bias):
    T, Dm = h.shape
    TM = _pick_tile(T, 256)
    tok = pl.BlockSpec((TOP_K, TM), lambda i: (0, i))
    return pl.pallas_call(
        functools.partial(_router_kernel, TM=TM),
        grid=(T // TM,),
        in_specs=[pl.BlockSpec((TM, Dm), lambda i: (i, 0)),
                  pl.BlockSpec((N_EXPERTS, Dm), lambda i: (0, 0)),
                  pl.BlockSpec((N_EXPERTS, 1), lambda i: (0, 0))],
        out_specs=[tok, tok, tok, pl.BlockSpec((N_EXPERTS, 128), lambda i: (0, 0))],
        out_shape=[jax.ShapeDtypeStruct((TOP_K, T), jnp.int32), jax.ShapeDtypeStruct((TOP_K, T), F32),
                   jax.ShapeDtypeStruct((TOP_K, T), jnp.int32), jax.ShapeDtypeStruct((N_EXPERTS, 128), F32)],
        scratch_shapes=[pltpu.VMEM((N_EXPERTS, 128), F32)],
        compiler_params=_cp("arbitrary"),
        name="moe_router",
    )(h, w_router.T, router_bias.astype(F32).reshape(N_EXPERTS, 1))


def _dispatch_kernel(dest_hbm, fill_hbm, hp_hbm, xs_hbm, dest_smem, fill_smem, zrow_ref, xbuf_ref,
                     sem, tsem, isem, fsem, zsem, *, TM, nt, nfill):
    i = pl.program_id(0)
    cp = pltpu.make_async_copy(dest_hbm.at[i], dest_smem, isem)
    cp.start()

    R = PACK_ROWS

    def slot_rows(d):
        return xs_hbm.at[pl.ds(pl.multiple_of(d * R, R), R)]

    @pl.when(i == 0)
    def _():
        zrow_ref[...] = jnp.zeros_like(zrow_ref)
        fc = pltpu.make_async_copy(fill_hbm, fill_smem, fsem)
        fc.start()
        fc.wait()

        def per_expert(e, carry):
            first = fill_smem[e]

            def body(r, c):
                pltpu.make_async_copy(zrow_ref, slot_rows(first + r), zsem).start()
                return c

            return lax.fori_loop(0, fill_smem[N_EXPERTS + e], body, carry)

        lax.fori_loop(0, N_EXPERTS, per_expert, 0)

        def per_expert_wait(e, carry):
            def wbody(r, c):
                pltpu.make_async_copy(zrow_ref, slot_rows(0), zsem).wait()
                return c

            return lax.fori_loop(0, fill_smem[N_EXPERTS + e], wbody, carry)

        lax.fori_loop(0, N_EXPERTS, per_expert_wait, 0)

    def tile_copy(tile, b):
        rows = pl.ds(pl.multiple_of(tile * (TM * R), TM * R), TM * R)
        return pltpu.make_async_copy(hp_hbm.at[rows], xbuf_ref.at[b], tsem.at[b])

    @pl.when(i == 0)
    def _():
        tile_copy(0, 0).start()
        if nt > 1:
            tile_copy(1, 1).start()

    cp.wait()
    cur = i % 3
    tile_copy(i, cur).wait()

    def issue(grp, carry):
        t0 = grp * ISSUE_GROUP
        slots = [[dest_smem[k, t0 + u] for k in range(TOP_K)] for u in range(ISSUE_GROUP)]
        for u in range(ISSUE_GROUP):
            src = xbuf_ref.at[cur, pl.ds(pl.multiple_of((t0 + u) * R, R), R)]
            for k in range(TOP_K):
                pltpu.make_async_copy(src, slot_rows(slots[u][k]), sem.at[cur]).start(priority=k % 2)
        return carry

    lax.fori_loop(0, TM // ISSUE_GROUP, issue, 0)

    def retire(b):
        def drain(t, carry):
            for k in range(TOP_K):
                pltpu.make_async_copy(xbuf_ref.at[b, pl.ds(0, R)], slot_rows(0), sem.at[b]).wait()
            return carry

        lax.fori_loop(0, TM, drain, 0)

    @pl.when(i > 0)
    def _():
        retire((i + 2) % 3)

    @pl.when(i + 2 < nt)
    def _():
        tile_copy(i + 2, (i + 2) % 3).start()

    @pl.when(i == nt - 1)
    def _():
        retire(cur)


def _dispatch(hp, dest_tiles, fill, rows):
    nt, _, TM = dest_tiles.shape
    nfill = fill.shape[0]
    return pl.pallas_call(
        functools.partial(_dispatch_kernel, TM=TM, nt=nt, nfill=nfill),
        grid=(nt,),
        in_specs=[pl.BlockSpec(memory_space=pl.ANY), pl.BlockSpec(memory_space=pl.ANY),
                  pl.BlockSpec(memory_space=pl.ANY)],
        out_specs=pl.BlockSpec(memory_space=pl.ANY),
        out_shape=jax.ShapeDtypeStruct((rows * PACK_ROWS, 128), jnp.uint32),
        scratch_shapes=[pltpu.SMEM((TOP_K, TM), jnp.int32), pltpu.SMEM((nfill,), jnp.int32),
                        pltpu.VMEM((PACK_ROWS, 128), jnp.uint32),
                        pltpu.VMEM((3, TM * PACK_ROWS, 128), jnp.uint32),
                        pltpu.SemaphoreType.DMA((3,)), pltpu.SemaphoreType.DMA((3,)),
                        pltpu.SemaphoreType.DMA, pltpu.SemaphoreType.DMA, pltpu.SemaphoreType.DMA],
        compiler_params=_cp("arbitrary"),
        name="moe_dispatch",
    )(dest_tiles, fill, hp)


def _expert_kernel(be_ref, na_ref, x_ref, wg_ref, wu_ref, wd_ref, o_ref, wgb_ref, wub_ref, wdb_ref):
    i = pl.program_id(0)
    active = i < na_ref[0]
    new_expert = jnp.logical_or(i == 0, be_ref[i] != be_ref[jnp.maximum(i - 1, 0)])

    @pl.when(jnp.logical_and(active, new_expert))
    def _():
        wgb_ref[...] = wg_ref[0, 0].astype(BF16)
        wub_ref[...] = wu_ref[0, 0].astype(BF16)
        wdb_ref[...] = wd_ref[0, 0].astype(BF16)

    @pl.when(active)
    def _():
        x = jnp.concatenate([c.astype(BF16) for c in _unpack_rows(x_ref, MOE_BLOCK)], axis=1)
        a = _dot(x, wgb_ref[...])
        b = _dot(x, wub_ref[...])
        hmid = (a * jax.nn.sigmoid(a) * b).astype(BF16)
        _pack_rows(o_ref, _dot(hmid, wdb_ref[...]))


def _experts(xs, blk_e, nact, w_gate, w_up, w_down, layer):
    Dm = D_MODEL
    rows = xs.shape[0] // PACK_ROWS
    nblk = rows // MOE_BLOCK
    pblock = MOE_BLOCK * PACK_ROWS

    def row(i, be, na):
        return (jnp.minimum(i, na[0] - 1), 0)

    def wsel(i, be, na):
        return (layer, be[jnp.minimum(i, na[0] - 1)], 0, 0)

    return pl.pallas_call(
        _expert_kernel,
        grid_spec=pltpu.PrefetchScalarGridSpec(
            num_scalar_prefetch=2,
            grid=(nblk,),
            in_specs=[pl.BlockSpec((pblock, 128), row),
                      pl.BlockSpec((1, 1, Dm, D_EXPERT), wsel),
                      pl.BlockSpec((1, 1, Dm, D_EXPERT), wsel),
                      pl.BlockSpec((1, 1, D_EXPERT, Dm), wsel)],
            out_specs=pl.BlockSpec((pblock, 128), row),
            scratch_shapes=[pltpu.VMEM((Dm, D_EXPERT), BF16), pltpu.VMEM((Dm, D_EXPERT), BF16),
                            pltpu.VMEM((D_EXPERT, Dm), BF16)],
        ),
        out_shape=jax.ShapeDtypeStruct((rows * PACK_ROWS, 128), jnp.uint32),
        compiler_params=_cp("arbitrary"),
        name="moe_experts",
    )(blk_e, nact, xs, w_gate, w_up, w_down)


def _combine_kernel(dest_hbm, yb_hbm, w_ref, h_ref, sh_ref, g_ref, b_ref, o_ref, ob_ref,
                    dest_smem, buf_ref, routed_ref, sem, isem, *, TM, nt):
    i = pl.program_id(0)
    R = PACK_ROWS
    G = 8
    half = D_MODEL // 2

    def load_slots(tile, slot):
        cp = pltpu.make_async_copy(dest_hbm.at[tile], dest_smem.at[slot], isem)
        cp.start()
        cp.wait()

    def issue_group(slot, grp):
        t0 = grp * G
        slots = [[dest_smem[slot, k, t0 + u] for k in range(TOP_K)] for u in range(G)]
        for u in range(G):
            for k in range(TOP_K):
                src = yb_hbm.at[pl.ds(pl.multiple_of(slots[u][k] * R, R), R)]
                dst = buf_ref.at[slot, k, pl.ds(pl.multiple_of((t0 + u) * R, R), R)]
                pltpu.make_async_copy(src, dst, sem.at[slot]).start(priority=k % 2)

    @pl.when(i == 0)
    def _():
        load_slots(0, 0)

        def first(grp, carry):
            issue_group(0, grp)
            return carry

        lax.fori_loop(0, TM // G, first, 0)

    slot = i % 2
    nslot = (i + 1) % 2
    load_slots(jnp.minimum(i + 1, nt - 1), nslot)

    def retire(b):
        def drain(t, carry):
            for k in range(TOP_K):
                pltpu.make_async_copy(yb_hbm.at[pl.ds(0, R)], buf_ref.at[b, k, pl.ds(0, R)], sem.at[b]).wait()
            return carry

        lax.fori_loop(0, TM, drain, 0)

    retire(slot)

    def step(grp, carry):
        t0 = grp * G
        slots = [[dest_smem[nslot, k, t0 + u] for k in range(TOP_K)] for u in range(G)]
        r0 = pl.multiple_of(grp * G, G)
        w = w_ref[pl.ds(r0, G), :]
        lo = [None] * R
        hi = [None] * R
        for k in range(TOP_K):
            wk = w[:, k:k + 1]
            words = [buf_ref[slot, k, pl.ds(grp * (G * R) + j, G, stride=R), :] for j in range(R)]
            u = k
            for kk in range(TOP_K):
                src = yb_hbm.at[pl.ds(pl.multiple_of(slots[u][kk] * R, R), R)]
                dst = buf_ref.at[nslot, kk, pl.ds(pl.multiple_of((t0 + u) * R, R), R)]
                pltpu.make_async_copy(src, dst, sem.at[nslot]).start(priority=kk % 2)
            for j in range(R):
                a = pltpu.bitcast(words[j] << 16, F32) * wk
                b = pltpu.bitcast(words[j] & jnp.uint32(HI_MASK), F32) * wk
                lo[j] = a if lo[j] is None else lo[j] + a
                hi[j] = b if hi[j] is None else hi[j] + b
        for j in range(R):
            routed_ref[pl.ds(r0, G), 128 * j:128 * (j + 1)] = lo[j]
            routed_ref[pl.ds(r0, G), half + 128 * j:half + 128 * (j + 1)] = hi[j]
        return carry

    lax.fori_loop(0, TM // G, step, 0)

    @pl.when(i == nt - 1)
    def _():
        retire(nslot)

    out = _layer_norm_rows(DN_ALPHA * h_ref[...] + (routed_ref[...] + sh_ref[...]), g_ref[...], b_ref[...])
    o_ref[...] = out
    ob_ref[...] = out.astype(BF16)


def _combine_ln(yb, dest_tiles, wts_tok, h, shared, g, b):
    nt, _, TM = dest_tiles.shape
    T, Dm = h.shape
    row = pl.BlockSpec((TM, Dm), lambda i: (i, 0))
    vec = pl.BlockSpec((1, Dm), lambda i: (0, 0))
    hbm = pl.BlockSpec(memory_space=pl.ANY)
    return pl.pallas_call(
        functools.partial(_combine_kernel, TM=TM, nt=nt),
        grid=(nt,),
        in_specs=[hbm, hbm, pl.BlockSpec((TM, TOP_K), lambda i: (i, 0)), row, row, vec, vec],
        out_specs=[row, row],
        out_shape=[jax.ShapeDtypeStruct((T, Dm), F32), jax.ShapeDtypeStruct((T, Dm), BF16)],
        scratch_shapes=[pltpu.SMEM((2, TOP_K, TM), jnp.int32),
                        pltpu.VMEM((2, TOP_K, TM * PACK_ROWS, 128), jnp.uint32),
                        pltpu.VMEM((TM, Dm), F32),
                        pltpu.SemaphoreType.DMA((2,)), pltpu.SemaphoreType.DMA],
        compiler_params=_cp("arbitrary"),
        name="moe_combine",
    )(dest_tiles, yb, wts_tok, h, shared, g.reshape(1, Dm), b.reshape(1, Dm))


def _shared_kernel(x_ref, wg_ref, wu_ref, wd_ref, o_ref, wgb_ref, wub_ref, wdb_ref):
    @pl.when(pl.program_id(0) == 0)
    def _():
        wgb_ref[...] = wg_ref[...].astype(BF16)
        wub_ref[...] = wu_ref[...].astype(BF16)
        wdb_ref[...] = wd_ref[...].astype(BF16)

    x = x_ref[...]
    a = _dot(x, wgb_ref[...])
    b = _dot(x, wub_ref[...])
    o_ref[...] = _dot((a * jax.nn.sigmoid(a) * b).astype(BF16), wdb_ref[...])


def _shared_expert(hb, wg, wu, wd):
    T, Dm = hb.shape
    Ds = wg.shape[1]
    TM = _pick_tile(T, DENSE_TM)
    full = lambda r, c: pl.BlockSpec((r, c), lambda i: (0, 0))
    return pl.pallas_call(
        _shared_kernel,
        grid=(T // TM,),
        in_specs=[pl.BlockSpec((TM, Dm), lambda i: (i, 0)), full(Dm, Ds), full(Dm, Ds), full(Ds, Dm)],
        out_specs=pl.BlockSpec((TM, Dm), lambda i: (i, 0)),
        out_shape=jax.ShapeDtypeStruct((T, Dm), F32),
        scratch_shapes=[pltpu.VMEM((Dm, Ds), BF16), pltpu.VMEM((Dm, Ds), BF16), pltpu.VMEM((Ds, Dm), BF16)],
        compiler_params=_cp("arbitrary"),
        name="moe_shared",
    )(hb, wg, wu, wd)


def _moe_ln(h, hb, hp, w_router, router_bias, w_gate, w_up, w_down, layer, ws_gate, ws_up, ws_down, g, b):
    T, Dm = h.shape
    eidx, wts, rank, cnt = _router(h, w_router, router_bias)
    counts = cnt[:, 0].astype(jnp.int32)
    pcounts = (counts + MOE_BLOCK - 1) // MOE_BLOCK * MOE_BLOCK
    pends = jnp.cumsum(pcounts)
    pstarts = pends - pcounts
    nblk = -(-T * TOP_K // MOE_BLOCK) + N_EXPERTS
    rows = nblk * MOE_BLOCK
    eid = jnp.arange(N_EXPERTS, dtype=jnp.int32)
    dest = jnp.sum(jnp.where(eidx[..., None] == eid, pstarts.astype(jnp.int32), 0), axis=-1) + rank
    TM = _pick_tile(T, 128)
    dest_tiles = dest.reshape(TOP_K, T // TM, TM).transpose(1, 0, 2)
    blk_start = jnp.arange(nblk, dtype=jnp.int32) * MOE_BLOCK
    blk_e = jnp.minimum(jnp.sum((pends[None, :] <= blk_start[:, None]).astype(jnp.int32), axis=1),
                        N_EXPERTS - 1)
    nact = (pends[-1:] // MOE_BLOCK).astype(jnp.int32)
    fill = jnp.concatenate([pstarts + counts, pcounts - counts]).astype(jnp.int32)
    xs = _dispatch(hp, dest_tiles, fill, rows)
    yb = _experts(xs, blk_e, nact, w_gate, w_up, w_down, layer)
    shared = _shared_expert(hb, ws_gate, ws_up, ws_down)
    return _combine_ln(yb, dest_tiles, wts.T, h, shared, g, b)


def kernel(x, meta_tokens, ev_w_in, ev_w_out, ev_sink, od_w_in, od_conv_w, od_a_log, od_dt_bias, od_norm_w,
           od_w_out, ln_g, ln_b, w_router, router_bias, w_gate, w_up, w_down, ws_gate, ws_up, ws_down):
    B, S, Dm = x.shape
    Lp = S + FRONT
    meta = jnp.broadcast_to(meta_tokens.astype(x.dtype)[None], (B, N_META, Dm))
    h = jnp.concatenate([jnp.zeros((B, DEAD, Dm), x.dtype), meta, x], axis=1).reshape(B * Lp, Dm)
    hb = h.astype(BF16)
    for layer in range(DEPTH):
        i = layer // 2
        if layer % 2 == 0:
            mix = _even_mixer(hb, B, Lp, ev_w_in[i], ev_w_out[i], ev_sink[i])
        else:
            mix = _gdn_mixer(hb, B, Lp, od_w_in[i], od_conv_w[i], od_a_log[i], od_dt_bias[i],
                             od_norm_w[i], od_w_out[i])
        h, hb, hp = _residual_ln(h, mix, ln_g[layer, 0], ln_b[layer, 0])
        h, hb = _moe_ln(h, hb, hp, w_router[layer], router_bias[layer], w_gate, w_up, w_down, layer,
                        ws_gate[layer], ws_up[layer], ws_down[layer], ln_g[layer, 1], ln_b[layer, 1])
    return h.reshape(B, Lp, Dm)[:, FRONT:]
```

```python
import functools
import math

import numpy as np
import jax
import jax.numpy as jnp
from jax import lax
from jax.experimental import pallas as pl
from jax.experimental.pallas import tpu as pltpu

F32 = jnp.float32
BF16 = jnp.bfloat16

D_MODEL = 2048
DEPTH = 4
N_META = 16
FRONT = 128
DEAD = FRONT - N_META
RET_HEADS = 8
RET_HD = 128
RET_W = RET_HEADS * RET_HD
RET_CHUNK = 128
ATT_HEADS = 8
ATT_KV_HEADS = 2
ATT_GROUP = ATT_HEADS // ATT_KV_HEADS
ATT_HD = 128
ATT_BLOCK = 128
WINDOW = 128
EVEN_IN = 4 * RET_W + ATT_HEADS * ATT_HD + 2 * ATT_KV_HEADS * ATT_HD
GDN_QK_HEADS = 16
GDN_V_HEADS = 32
GDN_HD = 128
GDN_K_W = GDN_QK_HEADS * GDN_HD
GDN_V_W = GDN_V_HEADS * GDN_HD
GDN_CONV_CH = 2 * GDN_K_W + GDN_V_W
GDN_CHUNK = 64
GDN_HG = 32
CONV_WIDTH = 5
N_EXPERTS = 64
TOP_K = 8
N_GROUPS = 8
GROUP_SIZE = N_EXPERTS // N_GROUPS
TOPK_GROUPS = 4
D_EXPERT = 384
ROUTE_SCALE = 2.5
MOE_BLOCK = 512
DENSE_TM = 1280
ISSUE_GROUP = 4
DN_ALPHA = (2 * DEPTH) ** 0.25
LN_EPS = 1e-5
NORM_EPS = 1e-6
NEG = -1e30

VMEM_LIMIT = 56 * 2**20


def _cp(*sem, vmem=VMEM_LIMIT):
    return pltpu.CompilerParams(dimension_semantics=sem, vmem_limit_bytes=vmem)


def _dot(a, b):
    return jnp.dot(a, b, preferred_element_type=F32)


def _dot_nt(a, b):
    return lax.dot_general(a, b, (((1,), (1,)), ((), ())), preferred_element_type=F32)


def _dot_tn(a, b):
    return lax.dot_general(a, b, (((0,), (0,)), ((), ())), preferred_element_type=F32)


def _split3(a):
    hi = a.astype(BF16)
    r1 = a - hi.astype(F32)
    mid = r1.astype(BF16)
    lo = (r1 - mid.astype(F32)).astype(BF16)
    return hi, mid, lo


def _pick_tile(n, cap, mult=8):
    for t in range(min(cap, n), 0, -1):
        if n % t == 0 and t % mult == 0:
            return t
    return n


def _mm_kernel(*refs, ksplits):
    nx = len(ksplits)
    x_refs, w_ref, o_ref, wb_ref = refs[:nx], refs[nx], refs[nx + 1], refs[nx + 2]

    @pl.when(pl.program_id(1) == 0)
    def _():
        wb_ref[...] = w_ref[...].astype(BF16)

    acc = None
    k0 = 0
    for x_ref, kk in zip(x_refs, ksplits):
        part = _dot(x_ref[...].astype(BF16), wb_ref[k0:k0 + kk, :])
        acc = part if acc is None else acc + part
        k0 += kk
    o_ref[...] = acc.astype(o_ref.dtype)


def _matmul(xs, w, col0, ncols, tn, out_dtype=F32, tm_cap=DENSE_TM):
    M = xs[0].shape[0]
    ksplits = tuple(x.shape[1] for x in xs)
    K = sum(ksplits)
    assert w.shape[0] == K and ncols % tn == 0 and col0 % tn == 0
    tm = _pick_tile(M, tm_cap)
    in_specs = [pl.BlockSpec((tm, kk), lambda j, i: (i, 0)) for kk in ksplits]
    in_specs.append(pl.BlockSpec((K, tn), lambda j, i: (0, col0 // tn + j)))
    return pl.pallas_call(
        functools.partial(_mm_kernel, ksplits=ksplits),
        grid=(ncols // tn, M // tm),
        in_specs=in_specs,
        out_specs=pl.BlockSpec((tm, tn), lambda j, i: (i, j)),
        out_shape=jax.ShapeDtypeStruct((M, ncols), out_dtype),
        scratch_shapes=[pltpu.VMEM((K, tn), BF16)],
        compiler_params=_cp("arbitrary", "arbitrary"),
        name="matmul",
    )(*xs, w)


PACK_ROWS = D_MODEL // 256
HI_MASK = 0xFFFF0000


def _pack_rows(ref, x):
    n = x.shape[0]
    half = D_MODEL // 2
    for j in range(PACK_ROWS):
        lo = x[:, 128 * j:128 * (j + 1)].astype(BF16).astype(F32)
        hi = x[:, half + 128 * j:half + 128 * (j + 1)].astype(BF16).astype(F32)
        word = (pltpu.bitcast(lo, jnp.uint32) >> 16) | (pltpu.bitcast(hi, jnp.uint32) & jnp.uint32(HI_MASK))
        ref[pl.ds(j, n, stride=PACK_ROWS), :] = word


def _unpack_rows(ref, n):
    lo, hi = [], []
    for j in range(PACK_ROWS):
        word = ref[pl.ds(j, n, stride=PACK_ROWS), :]
        lo.append(pltpu.bitcast(word << 16, F32))
        hi.append(pltpu.bitcast(word & jnp.uint32(HI_MASK), F32))
    return lo + hi


def _layer_norm_rows(y, g, b):
    mu = jnp.mean(y, axis=-1, keepdims=True)
    d = y - mu
    var = jnp.mean(d * d, axis=-1, keepdims=True)
    return d * lax.rsqrt(var + LN_EPS) * g + b


def _ln_kernel(h_ref, a_ref, g_ref, b_ref, o_ref, ob_ref, op_ref):
    out = _layer_norm_rows(DN_ALPHA * h_ref[...] + a_ref[...], g_ref[...], b_ref[...])
    o_ref[...] = out
    ob_ref[...] = out.astype(BF16)
    _pack_rows(op_ref, out)


def _residual_ln(h, add, g, b):
    M, Dm = h.shape
    tm = _pick_tile(M, 256)
    row = pl.BlockSpec((tm, Dm), lambda i: (i, 0))
    vec = pl.BlockSpec((1, Dm), lambda i: (0, 0))
    return pl.pallas_call(
        _ln_kernel,
        grid=(M // tm,),
        in_specs=[row, row, vec, vec],
        out_specs=[row, row, pl.BlockSpec((tm * PACK_ROWS, 128), lambda i: (i, 0))],
        out_shape=[jax.ShapeDtypeStruct((M, Dm), F32), jax.ShapeDtypeStruct((M, Dm), BF16),
                   jax.ShapeDtypeStruct((M * PACK_ROWS, 128), jnp.uint32)],
        compiler_params=_cp("arbitrary"),
        name="residual_ln",
    )(h, add, g.reshape(1, Dm), b.reshape(1, Dm))


def _ret_tables(C):
    hh = np.arange(RET_HEADS, dtype=np.float64)
    lg = np.log(1.0 - 2.0 ** (-5.0 - hh))[:, None]
    pos = np.arange(C, dtype=np.float64)[None, :]
    vecs = np.stack([np.exp(lg * (pos + 1.0)),
                     np.exp(lg * (C - pos)),
                     np.exp(lg * (C - 1.0 - pos)),
                     np.exp(lg * pos),
                     np.exp(lg * C) * np.ones_like(pos)], axis=1)
    tab = np.broadcast_to(vecs[..., None], (RET_HEADS, 5, C, RET_HD))
    rel = np.abs(pos.T - pos)
    dsym = np.exp(lg[:, :, None] * rel[None])
    return jnp.asarray(tab, F32), jnp.asarray(dsym, F32)


def _ret_kernel(q_ref, k_ref, v_ref, g_ref, tab_ref, d_ref, o_ref, fst_ref, run_ref, *, nc, C):
    s = pl.program_id(1)
    fwd = s < nc
    c = jnp.where(fwd, s, 2 * nc - 1 - s)
    row = c * C + lax.broadcasted_iota(jnp.int32, (C, 1), 0)
    live = row >= DEAD
    heads = range(RET_HEADS)
    HD = RET_HD
    ks = [jnp.where(live, k_ref[0, :, h * HD:(h + 1) * HD] * (RET_HD ** -0.5), 0.0) for h in heads]
    vbs = [jnp.where(live, v_ref[0, :, h * HD:(h + 1) * HD], 0.0).astype(BF16) for h in heads]

    @pl.when(jnp.logical_or(s == 0, s == nc))
    def _():
        run_ref[...] = jnp.zeros_like(run_ref)

    @pl.when(fwd)
    def _():
        for h in heads:
            fst_ref[c, h] = run_ref[h].astype(BF16)
        upd = [_dot_tn((ks[h] * tab_ref[h, 2]).astype(BF16), vbs[h]) for h in heads]
        for h in heads:
            run_ref[h] = run_ref[h] * tab_ref[h, 4] + upd[h]

    @pl.when(jnp.logical_not(fwd))
    def _():
        qs = [q_ref[0, :, h * HD:(h + 1) * HD] for h in heads]
        sc = [(_dot_nt(qs[h].astype(BF16), ks[h].astype(BF16)) * d_ref[h]).astype(BF16) for h in heads]
        left = [_dot((qs[h] * tab_ref[h, 0]).astype(BF16), fst_ref[c, h]) for h in heads]
        right = [_dot((qs[h] * tab_ref[h, 1]).astype(BF16), run_ref[h].astype(BF16)) for h in heads]
        intra = [_dot(sc[h], vbs[h]) for h in heads]
        upd = [_dot_tn((ks[h] * tab_ref[h, 3]).astype(BF16), vbs[h]) for h in heads]
        for h in heads:
            run_ref[h] = run_ref[h] * tab_ref[h, 4] + upd[h]
            out = intra[h] + left[h] + right[h]
            mu = jnp.mean(out, axis=-1, keepdims=True)
            dlt = out - mu
            var = jnp.mean(dlt * dlt, axis=-1, keepdims=True)
            normed = dlt * lax.rsqrt(var + NORM_EPS)
            g = g_ref[0, :, h * HD:(h + 1) * HD]
            o_ref[0, :, h * HD:(h + 1) * HD] = (g * jax.nn.sigmoid(g) * normed).astype(o_ref.dtype)


def _retention(proj):
    B, Lp, _ = proj.shape
    C = RET_CHUNK
    nc = Lp // C
    tab, dsym = _ret_tables(C)

    def cidx(s):
        return jnp.where(s < nc, s, 2 * nc - 1 - s)

    def cidx_out(s):
        return jnp.where(s < nc, nc - 1, 2 * nc - 1 - s)

    return pl.pallas_call(
        functools.partial(_ret_kernel, nc=nc, C=C),
        grid=(B, 2 * nc),
        in_specs=[
            pl.BlockSpec((1, C, RET_W), lambda b, s: (b, cidx_out(s), 0)),
            pl.BlockSpec((1, C, RET_W), lambda b, s: (b, cidx(s), 1)),
            pl.BlockSpec((1, C, RET_W), lambda b, s: (b, cidx(s), 2)),
            pl.BlockSpec((1, C, RET_W), lambda b, s: (b, cidx_out(s), 3)),
            pl.BlockSpec((RET_HEADS, 5, C, RET_HD), lambda b, s: (0, 0, 0, 0)),
            pl.BlockSpec((RET_HEADS, C, C), lambda b, s: (0, 0, 0)),
        ],
        out_specs=pl.BlockSpec((1, C, RET_W), lambda b, s: (b, cidx_out(s), 0)),
        out_shape=jax.ShapeDtypeStruct((B, Lp, RET_W), BF16),
        scratch_shapes=[pltpu.VMEM((nc, RET_HEADS, RET_HD, RET_HD), BF16),
                        pltpu.VMEM((RET_HEADS, RET_HD, RET_HD), F32)],
        compiler_params=_cp("arbitrary", "arbitrary"),
        name="retention",
    )(proj, proj, proj, proj, tab, dsym)


def _att_kernel(sink_ref, slope_ref, q_ref, km_ref, kp_ref, kc_ref, kn_ref,
                vm_ref, vp_ref, vc_ref, vn_ref, o_ref, *, nb):
    j = pl.program_id(1)
    qb = pl.program_id(2)
    T = ATT_BLOCK
    ii = lax.broadcasted_iota(jnp.int32, (T, T), 0)
    jj = lax.broadcasted_iota(jnp.int32, (T, T), 1)
    meta_ok = jj >= DEAD
    pieces = []
    for off, k_ref, v_ref, ok in ((-T, kp_ref, vp_ref, qb >= 2),
                                  (0, kc_ref, vc_ref, qb >= 1),
                                  (T, kn_ref, vn_ref, qb + 1 <= nb)):
        dist = jnp.abs(jj + off - ii)
        pieces.append((dist.astype(F32), dist <= WINDOW, k_ref[0].astype(BF16), v_ref[0].astype(BF16), ok))
    km = km_ref[0].astype(BF16)
    vm = vm_ref[0].astype(BF16)
    vals = [vm] + [p[3] for p in pieces]
    groups = range(ATT_GROUP)
    qs = [(q_ref[0, :, g * ATT_HD:(g + 1) * ATT_HD] * (ATT_HD ** -0.5)).astype(BF16) for g in groups]
    raw = [[_dot_nt(qs[g], km)] + [_dot_nt(qs[g], p[2]) for p in pieces] for g in groups]
    es, dens = [], []
    for g in groups:
        slope = slope_ref[j, g]
        sink = sink_ref[j, g]
        s_list = [jnp.where(meta_ok, raw[g][0], NEG)]
        for (dist, inwin, _, _, ok), r in zip(pieces, raw[g][1:]):
            s_list.append(jnp.where(ok, jnp.where(inwin, r - slope * dist, NEG), NEG))
        m = jnp.full((T, 1), sink, F32)
        for sb in s_list:
            m = jnp.maximum(m, jnp.max(sb, axis=-1, keepdims=True))
        e_list = [jnp.exp(sb - m) for sb in s_list]
        den = jnp.exp(sink - m)
        for e in e_list:
            den = den + jnp.sum(e, axis=-1, keepdims=True)
        es.append([e.astype(BF16) for e in e_list])
        dens.append(den)
    pv = [[_dot(e, vb) for e, vb in zip(es[g], vals)] for g in groups]
    for g in groups:
        acc = pv[g][0] + pv[g][1] + pv[g][2] + pv[g][3]
        o_ref[0, :, g * ATT_HD:(g + 1) * ATT_HD] = (acc / dens[g]).astype(o_ref.dtype)


def _window_attention(proj, sink):
    B, Lp, _ = proj.shape
    nb = Lp // ATT_BLOCK - 1
    gw = ATT_GROUP * ATT_HD
    q0 = 4 * RET_W // gw
    k0 = (4 * RET_W + ATT_HEADS * ATT_HD) // ATT_HD
    v0 = k0 + ATT_KV_HEADS
    slopes = np.asarray(2.0 ** (-8.0 * (np.arange(ATT_HEADS) + 1.0) / ATT_HEADS), np.float32)
    smem = pl.BlockSpec(memory_space=pltpu.SMEM)

    def kv(c0, f):
        return pl.BlockSpec((1, ATT_BLOCK, ATT_HD), lambda b, j, t: (b, f(t), c0 + j))

    rows = [lambda t: 0, lambda t: jnp.maximum(t - 1, 0), lambda t: t, lambda t: jnp.minimum(t + 1, nb)]
    return pl.pallas_call(
        functools.partial(_att_kernel, nb=nb),
        grid=(B, ATT_KV_HEADS, nb + 1),
        in_specs=[smem, smem, pl.BlockSpec((1, ATT_BLOCK, gw), lambda b, j, t: (b, t, q0 + j))]
        + [kv(k0, f) for f in rows] + [kv(v0, f) for f in rows],
        out_specs=pl.BlockSpec((1, ATT_BLOCK, gw), lambda b, j, t: (b, t, j)),
        out_shape=jax.ShapeDtypeStruct((B, Lp, ATT_HEADS * ATT_HD), BF16),
        compiler_params=_cp("arbitrary", "arbitrary", "arbitrary"),
        name="window_attention",
    )(sink.astype(F32).reshape(ATT_KV_HEADS, ATT_GROUP), jnp.asarray(slopes).reshape(ATT_KV_HEADS, ATT_GROUP),
      *([proj] * 9))


CONV_HALO = 16


def _conv_shift_matrix(TT):
    half = CONV_WIDTH // 2
    t = np.arange(TT)[:, None]
    s = np.arange(TT + 2 * CONV_HALO)[None, :]
    blocks = [(s == t + CONV_HALO + tap - half) for tap in range(CONV_WIDTH) if tap != half]
    return jnp.asarray(np.concatenate(blocks, axis=0), BF16)


def _conv_kernel(xp_ref, x_ref, xn_ref, w_ref, sh_ref, o_ref, *, mode, TT, nt):
    t = pl.program_id(1)
    H = CONV_HALO
    rows = t * TT + lax.broadcasted_iota(jnp.int32, (TT, 1), 0)
    live = rows >= DEAD
    x = jnp.where(live, x_ref[0].astype(F32), 0.0)
    prow = t * TT - H + lax.broadcasted_iota(jnp.int32, (H, 1), 0)
    prev = jnp.where(prow >= DEAD, xp_ref[0].astype(F32), 0.0)
    nrow = (t + 1) * TT + lax.broadcasted_iota(jnp.int32, (H, 1), 0)
    nxt = jnp.where(jnp.logical_and(nrow >= DEAD, t < nt - 1), xn_ref[0].astype(F32), 0.0)
    staged = jnp.concatenate([prev.astype(BF16), x.astype(BF16), nxt.astype(BF16)], axis=0)
    shifted = _dot(sh_ref[...], staged)
    w = w_ref[...]
    half = CONV_WIDTH // 2
    acc = x * w[half:half + 1, :]
    blk = 0
    for tap in range(CONV_WIDTH):
        if tap == half:
            continue
        acc = acc + shifted[blk * TT:(blk + 1) * TT, :] * w[tap:tap + 1, :]
        blk += 1
    y = acc * jax.nn.sigmoid(acc)
    if mode in ("q", "k"):
        scale = GDN_HD ** -0.5 if mode == "q" else 1.0
        for a in range(y.shape[1] // GDN_HD):
            ya = y[:, a * GDN_HD:(a + 1) * GDN_HD]
            ya = ya * lax.rsqrt(jnp.sum(ya * ya, axis=-1, keepdims=True) + NORM_EPS)
            if mode == "q":
                ya = ya * scale
            o_ref[0, :, a * GDN_HD:(a + 1) * GDN_HD] = jnp.where(live, ya, 0.0).astype(o_ref.dtype)
    else:
        o_ref[0] = jnp.where(live, y, 0.0).astype(o_ref.dtype)


def _short_conv(qkv, conv_w, mode, col0, ncols):
    B, Lp, _ = qkv.shape
    H = CONV_HALO
    TT = _pick_tile(Lp, 256, mult=H)
    TC = 1024
    nt = Lp // TT
    c0 = col0 // TC
    shift = _conv_shift_matrix(TT)
    return pl.pallas_call(
        functools.partial(_conv_kernel, mode=mode, TT=TT, nt=nt),
        grid=(B, nt, ncols // TC),
        in_specs=[
            pl.BlockSpec((1, H, TC), lambda b, t, c: (b, jnp.maximum(t * (TT // H) - 1, 0), c0 + c)),
            pl.BlockSpec((1, TT, TC), lambda b, t, c: (b, t, c0 + c)),
            pl.BlockSpec((1, H, TC), lambda b, t, c: (b, jnp.minimum((t + 1) * (TT // H), Lp // H - 1), c0 + c)),
            pl.BlockSpec((CONV_WIDTH, TC), lambda b, t, c: (0, c0 + c)),
            pl.BlockSpec(shift.shape, lambda b, t, c: (0, 0)),
        ],
        out_specs=pl.BlockSpec((1, TT, TC), lambda b, t, c: (b, t, c)),
        out_shape=jax.ShapeDtypeStruct((B, Lp, ncols), BF16),
        compiler_params=_cp("arbitrary", "arbitrary", "arbitrary"),
        name="short_conv_" + mode,
    )(qkv, qkv, qkv, conv_w, shift)


def _gate_kernel(ba_ref, alog_ref, dtb_ref, o_ref, *, TT):
    t = pl.program_id(1)
    rows = t * TT + lax.broadcasted_iota(jnp.int32, (TT, 1), 0)
    lane = lax.broadcasted_iota(jnp.int32, (1, 4 * GDN_V_HEADS), 1)
    is_g = (lane // GDN_V_HEADS) % 2 == 1
    x = ba_ref[0]
    beta = jax.nn.sigmoid(x)
    xs = x + dtb_ref[...]
    softplus = jnp.maximum(xs, 0.0) + jnp.log(1.0 + jnp.exp(-jnp.abs(xs)))
    g = -jnp.exp(alog_ref[...]) * softplus
    o_ref[0] = jnp.where(rows >= DEAD, jnp.where(is_g, g, beta), 0.0)


def _gates(ba, a_log, dt_bias):
    B, Lp, W = ba.shape
    TT = _pick_tile(Lp, 1024)
    zeros = jnp.zeros((2, 1, GDN_V_HEADS), F32)
    alog = jnp.concatenate([zeros, a_log.astype(F32)[:, None, :]], axis=1).reshape(1, W)
    dtb = jnp.concatenate([zeros, dt_bias.astype(F32)[:, None, :]], axis=1).reshape(1, W)
    vec = pl.BlockSpec((1, W), lambda b, t: (0, 0))
    return pl.pallas_call(
        functools.partial(_gate_kernel, TT=TT),
        grid=(B, Lp // TT),
        in_specs=[pl.BlockSpec((1, TT, W), lambda b, t: (b, t, 0)), vec, vec],
        out_specs=pl.BlockSpec((1, TT, W), lambda b, t: (b, t, 0)),
        out_shape=jax.ShapeDtypeStruct((B, Lp, W), F32),
        compiler_params=_cp("arbitrary", "arbitrary"),
        name="gdn_gates",
    )(ba, alog, dtb)


def _gdn_kernel(*refs, rev, final):
    if final:
        q_ref, k_ref, v_ref, gc_ref, gr_ref, of_ref, z_ref, nw_ref, o_ref, s_ref = refs
    else:
        q_ref, k_ref, v_ref, gc_ref, gr_ref, o_ref, s_ref = refs
    C = GDN_CHUNK
    HG = GDN_HG

    @pl.when(pl.program_id(2) == 0)
    def _():
        s_ref[...] = jnp.zeros_like(s_ref)

    ii = lax.broadcasted_iota(jnp.int32, (C, C), 0)
    jj = lax.broadcasted_iota(jnp.int32, (C, C), 1)
    incl = (jj >= ii) if rev else (jj <= ii)
    strict = (jj > ii) if rev else (jj < ii)
    tri_col = jnp.where(incl, 1.0, 0.0).astype(BF16)
    tri_row = jnp.where((ii >= jj) if rev else (ii <= jj), 1.0, 0.0).astype(BF16)
    gcol = gc_ref[0, 0]
    grow = gr_ref[0, 0, 0]
    gc_col = sum(_dot(tri_col, p) for p in _split3(gcol))
    gc_row = sum(_dot(p, tri_row) for p in _split3(grow))
    last = 0 if rev else C - 1
    d0 = 2 * HG if rev else 0
    HD = GDN_HD
    heads = range(HG)
    pairs = range(HG // 2)
    qs = [q_ref[0, :, p * HD:(p + 1) * HD] for p in pairs]
    ks = [k_ref[0, :, p * HD:(p + 1) * HD] for p in pairs]
    kfs = [k.astype(F32) for k in ks]
    kq = [_dot_nt(jnp.concatenate([ks[p], qs[p]], axis=0), ks[p]) for p in pairs]
    gram = [r[:C] for r in kq]
    qk = [r[C:] for r in kq]
    beta = [gcol[:, d0 + h:d0 + h + 1] for h in heads]
    gcc = [gc_col[:, d0 + HG + h:d0 + HG + h + 1] for h in heads]
    gcr = [gc_row[d0 + HG + h:d0 + HG + h + 1, :] for h in heads]
    gtot = [g[last:last + 1, :] for g in gcc]
    decay = [jnp.where(incl, jnp.exp(jnp.where(incl, gcc[h] - gcr[h], 0.0)), 0.0) for h in heads]
    eg = [jnp.exp(g) for g in gcc]
    ms = [-jnp.where(strict, beta[h] * gram[h // 2] * decay[h], 0.0) for h in heads]
    pbs = [m.astype(BF16) for m in ms]
    ps = [_dot(pb, pb) for pb in pbs]
    pbs = [p.astype(BF16) for p in ps]
    for _ in range(int(math.log2(C)) - 2):
        res = [_dot(jnp.concatenate([ms[h].astype(BF16), pbs[h]], axis=0), pbs[h]) for h in heads]
        ms = [ms[h] + ps[h] + res[h][:C] for h in heads]
        ps = [r[C:] for r in res]
        pbs = [p.astype(BF16) for p in ps]
    ms = [ms[h] + ps[h] + _dot(ms[h].astype(BF16), pbs[h]) for h in heads]
    rhs = [jnp.concatenate([v_ref[0, :, h * HD:(h + 1) * HD].astype(F32) * beta[h],
                            kfs[h // 2] * (beta[h] * eg[h])], axis=1) for h in heads]
    sol = [rhs[h] + _dot(ms[h].astype(BF16), rhs[h].astype(BF16)) for h in heads]
    st = [s_ref[h] for h in heads]
    stb = [s.astype(BF16) for s in st]
    lhs = [jnp.concatenate([sol[h][:, HD:].astype(BF16), (qs[h // 2].astype(F32) * eg[h]).astype(BF16)], axis=0)
           for h in heads]
    ws = [_dot(lhs[h], stb[h]) for h in heads]
    vnb = [(sol[h][:, :HD] - ws[h][:C]).astype(BF16) for h in heads]
    outs = [ws[h][C:] + _dot((qk[h // 2] * decay[h]).astype(BF16), vnb[h]) for h in heads]
    upd = [_dot_tn((kfs[h // 2] * jnp.exp(gtot[h] - gcc[h])).astype(BF16), vnb[h]) for h in heads]
    for h in heads:
        s_ref[h] = st[h] * jnp.exp(gtot[h]) + upd[h]
        sl = slice(h * HD, (h + 1) * HD)
        if final:
            o = outs[h] + of_ref[0, :, sl]
            z = z_ref[0, :, sl]
            o = o * lax.rsqrt(jnp.mean(o * o, axis=-1, keepdims=True) + NORM_EPS) * nw_ref[...]
            o_ref[0, :, sl] = (o * (z * jax.nn.sigmoid(z))).astype(o_ref.dtype)
        else:
            o_ref[0, :, sl] = outs[h]


def _gdn_dir(q, k, v, gcol, grow, rev, o_fwd=None, z=None, norm_w=None):
    B, Lp, _ = v.shape
    C = GDN_CHUNK
    nc = Lp // C
    ng = GDN_V_HEADS // GDN_HG
    qw = GDN_HG // 2 * GDN_HD
    vw = GDN_HG * GDN_HD
    final = o_fwd is not None

    def cc(c):
        return nc - 1 - c if rev else c

    in_specs = [
        pl.BlockSpec((1, C, qw), lambda b, g, c: (b, cc(c), g)),
        pl.BlockSpec((1, C, qw), lambda b, g, c: (b, cc(c), g)),
        pl.BlockSpec((1, C, vw), lambda b, g, c: (b, cc(c), g)),
        pl.BlockSpec((1, 1, C, 4 * GDN_HG), lambda b, g, c: (b, g, cc(c), 0)),
        pl.BlockSpec((1, 1, 1, 4 * GDN_HG, C), lambda b, g, c: (b, g, cc(c), 0, 0)),
    ]
    args = [q, k, v, gcol, grow]
    if final:
        in_specs += [pl.BlockSpec((1, C, vw), lambda b, g, c: (b, cc(c), g)),
                     pl.BlockSpec((1, C, vw), lambda b, g, c: (b, cc(c), g)),
                     pl.BlockSpec((1, GDN_HD), lambda b, g, c: (0, 0))]
        args += [o_fwd, z, norm_w.astype(F32).reshape(1, GDN_HD)]
    return pl.pallas_call(
        functools.partial(_gdn_kernel, rev=rev, final=final),
        grid=(B, ng, nc),
        in_specs=in_specs,
        out_specs=pl.BlockSpec((1, C, vw), lambda b, g, c: (b, cc(c), g)),
        out_shape=jax.ShapeDtypeStruct((B, Lp, GDN_V_W), BF16 if final else F32),
        scratch_shapes=[pltpu.VMEM((GDN_HG, GDN_HD, GDN_HD), F32)],
        compiler_params=_cp("arbitrary", "arbitrary", "arbitrary"),
        name="gdn_bwd" if rev else "gdn_fwd",
    )(*args)


def _gdn_mixer(hb, B, Lp, w_in, conv_w, a_log, dt_bias, norm_w, w_out):
    qkv = _matmul([hb], w_in, 0, GDN_CONV_CH, 1024, out_dtype=BF16).reshape(B, Lp, GDN_CONV_CH)
    z = _matmul([hb], w_in, GDN_CONV_CH, GDN_V_W, 1024).reshape(B, Lp, GDN_V_W)
    ba = _matmul([hb], w_in, GDN_CONV_CH + GDN_V_W, 4 * GDN_V_HEADS, 128).reshape(B, Lp, 4 * GDN_V_HEADS)
    q = _short_conv(qkv, conv_w, "q", 0, GDN_K_W)
    k = _short_conv(qkv, conv_w, "k", GDN_K_W, GDN_K_W)
    v = _short_conv(qkv, conv_w, "v", 2 * GDN_K_W, GDN_V_W)
    gb = _gates(ba, a_log, dt_bias)
    ng = GDN_V_HEADS // GDN_HG
    nc = Lp // GDN_CHUNK
    gcol = gb.reshape(B, Lp, 4, ng, GDN_HG).transpose(0, 3, 1, 2, 4).reshape(B, ng, Lp, 4 * GDN_HG)
    grow = gcol.reshape(B, ng, nc, GDN_CHUNK, 4 * GDN_HG).transpose(0, 1, 2, 4, 3)
    o_f = _gdn_dir(q, k, v, gcol, grow, rev=False)
    o = _gdn_dir(q, k, v, gcol, grow, rev=True, o_fwd=o_f, z=z, norm_w=norm_w)
    return _matmul([o.reshape(B * Lp, GDN_V_W)], w_out, 0, D_MODEL, 512)


def _even_mixer(hb, B, Lp, w_in, w_out, sink):
    proj = _matmul([hb], w_in, 0, EVEN_IN, 512).reshape(B, Lp, EVEN_IN)
    ret = _retention(proj).reshape(B * Lp, RET_W)
    att = _window_attention(proj, sink).reshape(B * Lp, ATT_HEADS * ATT_HD)
    return _matmul([ret, att], w_out, 0, D_MODEL, 1024)


def _router_kernel(x_ref, wt_ref, bias_ref, eidx_ref, wts_ref, rank_ref, cnt_ref, carry_ref, *, TM):
    E = N_EXPERTS

    @pl.when(pl.program_id(0) == 0)
    def _():
        carry_ref[...] = jnp.zeros_like(carry_ref)

    xh, xm, _ = _split3(x_ref[...])
    wh, wm, _ = _split3(wt_ref[...])
    lead = _dot_nt(jnp.concatenate([wh, wm], axis=0), xh)
    logits = lead[:E] + (lead[E:] + _dot_nt(wh, xm))
    scores = jax.nn.sigmoid(logits)
    choice = scores + bias_ref[...]
    ninf = -jnp.inf
    io8 = lax.broadcasted_iota(jnp.int32, (GROUP_SIZE, TM), 0)
    gs_rows = []
    for g in range(N_GROUPS):
        cg = choice[g * GROUP_SIZE:(g + 1) * GROUP_SIZE, :]
        m1 = jnp.max(cg, axis=0, keepdims=True)
        i1 = jnp.min(jnp.where(cg == m1, io8, GROUP_SIZE), axis=0, keepdims=True)
        m2 = jnp.max(jnp.where(io8 == i1, ninf, cg), axis=0, keepdims=True)
        gs_rows.append(m1 + m2)
    gs = jnp.concatenate(gs_rows, axis=0)
    gsel = jnp.zeros((N_GROUPS, TM), jnp.int32)
    for _ in range(TOPK_GROUPS):
        m = jnp.max(gs, axis=0, keepdims=True)
        idx = jnp.min(jnp.where(gs == m, io8, N_GROUPS), axis=0, keepdims=True)
        hit = io8 == idx
        gsel = jnp.where(hit, 1, gsel)
        gs = jnp.where(hit, ninf, gs)
    masked = jnp.concatenate(
        [jnp.where(gsel[g:g + 1, :] > 0, choice[g * GROUP_SIZE:(g + 1) * GROUP_SIZE, :], ninf)
         for g in range(N_GROUPS)], axis=0)
    ioe = lax.broadcasted_iota(jnp.int32, (E, TM), 0)
    sel = jnp.zeros((E, TM), F32)
    idx_rows, w_rows = [], []
    for _ in range(TOP_K):
        m = jnp.max(masked, axis=0, keepdims=True)
        idx = jnp.min(jnp.where(masked == m, ioe, E), axis=0, keepdims=True)
        hit = ioe == idx
        idx_rows.append(idx)
        w_rows.append(jnp.sum(jnp.where(hit, scores, 0.0), axis=0, keepdims=True))
        sel = jnp.where(hit, 1.0, sel)
        masked = jnp.where(hit, ninf, masked)
    wsum = w_rows[0]
    for w in w_rows[1:]:
        wsum = wsum + w
    ti = lax.broadcasted_iota(jnp.int32, (TM, TM), 0)
    tj = lax.broadcasted_iota(jnp.int32, (TM, TM), 1)
    before = jnp.where(ti < tj, 1.0, 0.0).astype(BF16)
    rank = _dot(sel.astype(BF16), before) + carry_ref[:, 0:1]
    rank_rows = [jnp.sum(jnp.where(ioe == idx, rank, 0.0), axis=0, keepdims=True) for idx in idx_rows]
    eidx_ref[...] = jnp.concatenate(idx_rows, axis=0)
    wts_ref[...] = jnp.concatenate([w / wsum * ROUTE_SCALE for w in w_rows], axis=0)
    rank_ref[...] = jnp.concatenate(rank_rows, axis=0).astype(jnp.int32)
    carry_ref[...] = carry_ref[...] + jnp.sum(sel, axis=1, keepdims=True)
    cnt_ref[...] = carry_ref[...]


def _router(h, w_router, router_bias):
    T, Dm = h.shape
    TM = _pick_tile(T, 256)
    tok = pl.BlockSpec((TOP_K, TM), lambda i: (0, i))
    return pl.pallas_call(
        functools.partial(_router_kernel, TM=TM),
        grid=(T // TM,),
        in_specs=[pl.BlockSpec((TM, Dm), lambda i: (i, 0)),
                  pl.BlockSpec((N_EXPERTS, Dm), lambda i: (0, 0)),
                  pl.BlockSpec((N_EXPERTS, 1), lambda i: (0, 0))],
        out_specs=[tok, tok, tok, pl.BlockSpec((N_EXPERTS, 128), lambda i: (0, 0))],
        out_shape=[jax.ShapeDtypeStruct((TOP_K, T), jnp.int32), jax.ShapeDtypeStruct((TOP_K, T), F32),
                   jax.ShapeDtypeStruct((TOP_K, T), jnp.int32), jax.ShapeDtypeStruct((N_EXPERTS, 128), F32)],
        scratch_shapes=[pltpu.VMEM((N_EXPERTS, 128), F32)],
        compiler_params=_cp("arbitrary"),
        name="moe_router",
    )(h, w_router.T, router_bias.astype(F32).reshape(N_EXPERTS, 1))


def _dispatch_kernel(dest_hbm, fill_hbm, hp_hbm, xs_hbm, dest_smem, fill_smem, zrow_ref, xbuf_ref,
                     sem, tsem, isem, fsem, zsem, *, TM, nt, nfill):
    i = pl.program_id(0)
    cp = pltpu.make_async_copy(dest_hbm.at[i], dest_smem, isem)
    cp.start()

    R = PACK_ROWS

    def slot_rows(d):
        return xs_hbm.at[pl.ds(pl.multiple_of(d * R, R), R)]

    @pl.when(i == 0)
    def _():
        zrow_ref[...] = jnp.zeros_like(zrow_ref)
        fc = pltpu.make_async_copy(fill_hbm, fill_smem, fsem)
        fc.start()
        fc.wait()

        def per_expert(e, carry):
            first = fill_smem[e]

            def body(r, c):
                pltpu.make_async_copy(zrow_ref, slot_rows(first + r), zsem).start()
                return c

            return lax.fori_loop(0, fill_smem[N_EXPERTS + e], body, carry)

        lax.fori_loop(0, N_EXPERTS, per_expert, 0)

        def per_expert_wait(e, carry):
            def wbody(r, c):
                pltpu.make_async_copy(zrow_ref, slot_rows(0), zsem).wait()
                return c

            return lax.fori_loop(0, fill_smem[N_EXPERTS + e], wbody, carry)

        lax.fori_loop(0, N_EXPERTS, per_expert_wait, 0)

    def tile_copy(tile, b):
        rows = pl.ds(pl.multiple_of(tile * (TM * R), TM * R), TM * R)
        return pltpu.make_async_copy(hp_hbm.at[rows], xbuf_ref.at[b], tsem.at[b])

    @pl.when(i == 0)
    def _():
        tile_copy(0, 0).start()
        if nt > 1:
            tile_copy(1, 1).start()

    cp.wait()
    cur = i % 3
    tile_copy(i, cur).wait()

    def issue(grp, carry):
        t0 = grp * ISSUE_GROUP
        base = t0 * TOP_K
        slots = [[dest_smem[base + (u * TOP_K + k)] for k in range(TOP_K)] for u in range(ISSUE_GROUP)]
        for u in range(ISSUE_GROUP):
            src = xbuf_ref.at[cur, pl.ds(pl.multiple_of((t0 + u) * R, R), R)]
            for k in range(TOP_K):
                pltpu.make_async_copy(src, slot_rows(slots[u][k]), sem.at[cur]).start(priority=k % 2)
        return carry

    lax.fori_loop(0, TM // ISSUE_GROUP, issue, 0)

    def retire(b):
        def drain(t, carry):
            for k in range(TOP_K):
                pltpu.make_async_copy(xbuf_ref.at[b, pl.ds(0, R)], slot_rows(0), sem.at[b]).wait()
            return carry

        lax.fori_loop(0, TM, drain, 0)

    @pl.when(i > 0)
    def _():
        retire((i + 2) % 3)

    @pl.when(i + 2 < nt)
    def _():
        tile_copy(i + 2, (i + 2) % 3).start()

    @pl.when(i == nt - 1)
    def _():
        retire(cur)


def _dispatch(hp, dest_tiles, fill, rows):
    nt = dest_tiles.shape[0]
    TM = dest_tiles.shape[1] // TOP_K
    nfill = fill.shape[0]
    return pl.pallas_call(
        functools.partial(_dispatch_kernel, TM=TM, nt=nt, nfill=nfill),
        grid=(nt,),
        in_specs=[pl.BlockSpec(memory_space=pl.ANY), pl.BlockSpec(memory_space=pl.ANY),
                  pl.BlockSpec(memory_space=pl.ANY)],
        out_specs=pl.BlockSpec(memory_space=pl.ANY),
        out_shape=jax.ShapeDtypeStruct((rows * PACK_ROWS, 128), jnp.uint32),
        scratch_shapes=[pltpu.SMEM((TOP_K * TM,), jnp.int32), pltpu.SMEM((nfill,), jnp.int32),
                        pltpu.VMEM((PACK_ROWS, 128), jnp.uint32),
                        pltpu.VMEM((3, TM * PACK_ROWS, 128), jnp.uint32),
                        pltpu.SemaphoreType.DMA((3,)), pltpu.SemaphoreType.DMA((3,)),
                        pltpu.SemaphoreType.DMA, pltpu.SemaphoreType.DMA, pltpu.SemaphoreType.DMA],
        compiler_params=_cp("arbitrary"),
        name="moe_dispatch",
    )(dest_tiles, fill, hp)


def _expert_kernel(be_ref, na_ref, x_ref, wg_ref, wu_ref, wd_ref, o_ref, wgb_ref, wub_ref, wdb_ref):
    i = pl.program_id(0)
    active = i < na_ref[0]
    new_expert = jnp.logical_or(i == 0, be_ref[i] != be_ref[jnp.maximum(i - 1, 0)])

    @pl.when(jnp.logical_and(active, new_expert))
    def _():
        wgb_ref[...] = wg_ref[0, 0].astype(BF16)
        wub_ref[...] = wu_ref[0, 0].astype(BF16)
        wdb_ref[...] = wd_ref[0, 0].astype(BF16)

    @pl.when(active)
    def _():
        x = jnp.concatenate([c.astype(BF16) for c in _unpack_rows(x_ref, MOE_BLOCK)], axis=1)
        a = _dot(x, wgb_ref[...])
        b = _dot(x, wub_ref[...])
        hmid = (a * jax.nn.sigmoid(a) * b).astype(BF16)
        _pack_rows(o_ref, _dot(hmid, wdb_ref[...]))


def _experts(xs, blk_e, nact, w_gate, w_up, w_down, layer):
    Dm = D_MODEL
    rows = xs.shape[0] // PACK_ROWS
    nblk = rows // MOE_BLOCK
    pblock = MOE_BLOCK * PACK_ROWS

    def row(i, be, na):
        return (jnp.minimum(i, na[0] - 1), 0)

    def wsel(i, be, na):
        return (layer, be[jnp.minimum(i, na[0] - 1)], 0, 0)

    return pl.pallas_call(
        _expert_kernel,
        grid_spec=pltpu.PrefetchScalarGridSpec(
            num_scalar_prefetch=2,
            grid=(nblk,),
            in_specs=[pl.BlockSpec((pblock, 128), row),
                      pl.BlockSpec((1, 1, Dm, D_EXPERT), wsel),
                      pl.BlockSpec((1, 1, Dm, D_EXPERT), wsel),
                      pl.BlockSpec((1, 1, D_EXPERT, Dm), wsel)],
            out_specs=pl.BlockSpec((pblock, 128), row),
            scratch_shapes=[pltpu.VMEM((Dm, D_EXPERT), BF16), pltpu.VMEM((Dm, D_EXPERT), BF16),
                            pltpu.VMEM((D_EXPERT, Dm), BF16)],
        ),
        out_shape=jax.ShapeDtypeStruct((rows * PACK_ROWS, 128), jnp.uint32),
        compiler_params=_cp("arbitrary"),
        name="moe_experts",
    )(blk_e, nact, xs, w_gate, w_up, w_down)


def _combine_kernel(dest_hbm, yb_hbm, w_ref, h_ref, sh_ref, g_ref, b_ref, o_ref, ob_ref,
                    dest_smem, buf_ref, routed_ref, sem, isem, *, TM, nt):
    i = pl.program_id(0)
    R = PACK_ROWS
    G = 8
    half = D_MODEL // 2

    def load_slots(tile, slot):
        cp = pltpu.make_async_copy(dest_hbm.at[tile],
                                   dest_smem.at[pl.ds(pl.multiple_of(slot * (TM * TOP_K), TM * TOP_K), TM * TOP_K)], isem)
        cp.start()
        cp.wait()

    def issue_group(slot, grp):
        t0 = grp * G
        base = slot * (TM * TOP_K) + t0 * TOP_K
        slots = [[dest_smem[base + (u * TOP_K + k)] for k in range(TOP_K)] for u in range(G)]
        for u in range(G):
            for k in range(TOP_K):
                src = yb_hbm.at[pl.ds(pl.multiple_of(slots[u][k] * R, R), R)]
                dst = buf_ref.at[slot, k, pl.ds(pl.multiple_of((t0 + u) * R, R), R)]
                pltpu.make_async_copy(src, dst, sem.at[slot]).start(priority=k % 2)

    @pl.when(i == 0)
    def _():
        load_slots(0, 0)

        def first(grp, carry):
            issue_group(0, grp)
            return carry

        lax.fori_loop(0, TM // G, first, 0)

    slot = i % 2
    nslot = (i + 1) % 2
    load_slots(jnp.minimum(i + 1, nt - 1), nslot)

    def retire(b):
        def drain(t, carry):
            for k in range(TOP_K):
                pltpu.make_async_copy(yb_hbm.at[pl.ds(0, R)], buf_ref.at[b, k, pl.ds(0, R)], sem.at[b]).wait()
            return carry

        lax.fori_loop(0, TM, drain, 0)

    retire(slot)

    def step(grp, carry):
        t0 = grp * G
        base = nslot * (TM * TOP_K) + t0 * TOP_K
        slots = [[dest_smem[base + (u * TOP_K + k)] for k in range(TOP_K)] for u in range(G)]
        r0 = pl.multiple_of(grp * G, G)
        w = w_ref[pl.ds(r0, G), :]
        lo = [None] * R
        hi = [None] * R
        for k in range(TOP_K):
            wk = w[:, k:k + 1]
            words = [buf_ref[slot, k, pl.ds(grp * (G * R) + j, G, stride=R), :] for j in range(R)]
            u = k
            for kk in range(TOP_K):
                src = yb_hbm.at[pl.ds(pl.multiple_of(slots[u][kk] * R, R), R)]
                dst = buf_ref.at[nslot, kk, pl.ds(pl.multiple_of((t0 + u) * R, R), R)]
                pltpu.make_async_copy(src, dst, sem.at[nslot]).start(priority=kk % 2)
            for j in range(R):
                a = pltpu.bitcast(words[j] << 16, F32) * wk
                b = pltpu.bitcast(words[j] & jnp.uint32(HI_MASK), F32) * wk
                lo[j] = a if lo[j] is None else lo[j] + a
                hi[j] = b if hi[j] is None else hi[j] + b
        for j in range(R):
            routed_ref[pl.ds(r0, G), 128 * j:128 * (j + 1)] = lo[j]
            routed_ref[pl.ds(r0, G), half + 128 * j:half + 128 * (j + 1)] = hi[j]
        return carry

    lax.fori_loop(0, TM // G, step, 0)

    @pl.when(i == nt - 1)
    def _():
        retire(nslot)

    out = _layer_norm_rows(DN_ALPHA * h_ref[...] + (routed_ref[...] + sh_ref[...]), g_ref[...], b_ref[...])
    o_ref[...] = out
    ob_ref[...] = out.astype(BF16)


def _combine_ln(yb, dest_tiles, wts_tok, h, shared, g, b):
    nt = dest_tiles.shape[0]
    TM = dest_tiles.shape[1] // TOP_K
    T, Dm = h.shape
    row = pl.BlockSpec((TM, Dm), lambda i: (i, 0))
    vec = pl.BlockSpec((1, Dm), lambda i: (0, 0))
    hbm = pl.BlockSpec(memory_space=pl.ANY)
    return pl.pallas_call(
        functools.partial(_combine_kernel, TM=TM, nt=nt),
        grid=(nt,),
        in_specs=[hbm, hbm, pl.BlockSpec((TM, TOP_K), lambda i: (i, 0)), row, row, vec, vec],
        out_specs=[row, row],
        out_shape=[jax.ShapeDtypeStruct((T, Dm), F32), jax.ShapeDtypeStruct((T, Dm), BF16)],
        scratch_shapes=[pltpu.SMEM((2 * TOP_K * TM,), jnp.int32),
                        pltpu.VMEM((2, TOP_K, TM * PACK_ROWS, 128), jnp.uint32),
                        pltpu.VMEM((TM, Dm), F32),
                        pltpu.SemaphoreType.DMA((2,)), pltpu.SemaphoreType.DMA],
        compiler_params=_cp("arbitrary"),
        name="moe_combine",
    )(dest_tiles, yb, wts_tok, h, shared, g.reshape(1, Dm), b.reshape(1, Dm))


def _shared_kernel(x_ref, wg_ref, wu_ref, wd_ref, o_ref, wgb_ref, wub_ref, wdb_ref):
    @pl.when(pl.program_id(0) == 0)
    def _():
        wgb_ref[...] = wg_ref[...].astype(BF16)
        wub_ref[...] = wu_ref[...].astype(BF16)
        wdb_ref[...] = wd_ref[...].astype(BF16)

    x = x_ref[...]
    a = _dot(x, wgb_ref[...])
    b = _dot(x, wub_ref[...])
    o_ref[...] = _dot((a * jax.nn.sigmoid(a) * b).astype(BF16), wdb_ref[...])


def _shared_expert(hb, wg, wu, wd):
    T, Dm = hb.shape
    Ds = wg.shape[1]
    TM = _pick_tile(T, DENSE_TM)
    full = lambda r, c: pl.BlockSpec((r, c), lambda i: (0, 0))
    return pl.pallas_call(
        _shared_kernel,
        grid=(T // TM,),
        in_specs=[pl.BlockSpec((TM, Dm), lambda i: (i, 0)), full(Dm, Ds), full(Dm, Ds), full(Ds, Dm)],
        out_specs=pl.BlockSpec((TM, Dm), lambda i: (i, 0)),
        out_shape=jax.ShapeDtypeStruct((T, Dm), F32),
        scratch_shapes=[pltpu.VMEM((Dm, Ds), BF16), pltpu.VMEM((Dm, Ds), BF16), pltpu.VMEM((Ds, Dm), BF16)],
        compiler_params=_cp("arbitrary"),
        name="moe_shared",
    )(hb, wg, wu, wd)


def _moe_ln(h, hb, hp, w_router, router_bias, w_gate, w_up, w_down, layer, ws_gate, ws_up, ws_down, g, b):
    T, Dm = h.shape
    eidx, wts, rank, cnt = _router(h, w_router, router_bias)
    counts = cnt[:, 0].astype(jnp.int32)
    pcounts = (counts + MOE_BLOCK - 1) // MOE_BLOCK * MOE_BLOCK
    pends = jnp.cumsum(pcounts)
    pstarts = pends - pcounts
    nblk = -(-T * TOP_K // MOE_BLOCK) + N_EXPERTS
    rows = nblk * MOE_BLOCK
    eid = jnp.arange(N_EXPERTS, dtype=jnp.int32)
    dest = jnp.sum(jnp.where(eidx[..., None] == eid, pstarts.astype(jnp.int32), 0), axis=-1) + rank
    TM = _pick_tile(T, 128)
    dest_tiles = dest.T.reshape(T // TM, TM * TOP_K)
    blk_start = jnp.arange(nblk, dtype=jnp.int32) * MOE_BLOCK
    blk_e = jnp.minimum(jnp.sum((pends[None, :] <= blk_start[:, None]).astype(jnp.int32), axis=1),
                        N_EXPERTS - 1)
    nact = (pends[-1:] // MOE_BLOCK).astype(jnp.int32)
    fill = jnp.concatenate([pstarts + counts, pcounts - counts]).astype(jnp.int32)
    xs = _dispatch(hp, dest_tiles, fill, rows)
    yb = _experts(xs, blk_e, nact, w_gate, w_up, w_down, layer)
    shared = _shared_expert(hb, ws_gate, ws_up, ws_down)
    return _combine_ln(yb, dest_tiles, wts.T, h, shared, g, b)


def kernel(x, meta_tokens, ev_w_in, ev_w_out, ev_sink, od_w_in, od_conv_w, od_a_log, od_dt_bias, od_norm_w,
           od_w_out, ln_g, ln_b, w_router, router_bias, w_gate, w_up, w_down, ws_gate, ws_up, ws_down):
    B, S, Dm = x.shape
    Lp = S + FRONT
    meta = jnp.broadcast_to(meta_tokens.astype(x.dtype)[None], (B, N_META, Dm))
    h = jnp.concatenate([jnp.zeros((B, DEAD, Dm), x.dtype), meta, x], axis=1).reshape(B * Lp, Dm)
    hb = h.astype(BF16)
    for layer in range(DEPTH):
        i = layer // 2
        if layer % 2 == 0:
            mix = _even_mixer(hb, B, Lp, ev_w_in[i], ev_w_out[i], ev_sink[i])
        else:
            mix = _gdn_mixer(hb, B, Lp, od_w_in[i], od_conv_w[i], od_a_log[i], od_dt_bias[i],
                             od_norm_w[i], od_w_out[i])
        h, hb, hp = _residual_ln(h, mix, ln_g[layer, 0], ln_b[layer, 0])
        h, hb = _moe_ln(h, hb, hp, w_router[layer], router_bias[layer], w_gate, w_up, w_down, layer,
                        ws_gate[layer], ws_up[layer], ws_down[layer], ln_g[layer, 1], ln_b[layer, 1])
    return h.reshape(B, Lp, Dm)[:, FRONT:]
```

```python
import functools
import math

import numpy as np
import jax
import jax.numpy as jnp
from jax import lax
from jax.experimental import pallas as pl
from jax.experimental.pallas import tpu as pltpu

F32 = jnp.float32
BF16 = jnp.bfloat16

D_MODEL = 2048
DEPTH = 4
N_META = 16
FRONT = 128
DEAD = FRONT - N_META
RET_HEADS = 8
RET_HD = 128
RET_W = RET_HEADS * RET_HD
RET_CHUNK = 128
ATT_HEADS = 8
ATT_KV_HEADS = 2
ATT_GROUP = ATT_HEADS // ATT_KV_HEADS
ATT_HD = 128
ATT_BLOCK = 128
WINDOW = 128
EVEN_IN = 4 * RET_W + ATT_HEADS * ATT_HD + 2 * ATT_KV_HEADS * ATT_HD
GDN_QK_HEADS = 16
GDN_V_HEADS = 32
GDN_HD = 128
GDN_K_W = GDN_QK_HEADS * GDN_HD
GDN_V_W = GDN_V_HEADS * GDN_HD
GDN_CONV_CH = 2 * GDN_K_W + GDN_V_W
GDN_CHUNK = 64
GDN_HG = 32
CONV_WIDTH = 5
N_EXPERTS = 64
TOP_K = 8
N_GROUPS = 8
GROUP_SIZE = N_EXPERTS // N_GROUPS
TOPK_GROUPS = 4
D_EXPERT = 384
ROUTE_SCALE = 2.5
MOE_BLOCK = 512
DENSE_TM = 1280
ISSUE_GROUP = 4
DN_ALPHA = (2 * DEPTH) ** 0.25
LN_EPS = 1e-5
NORM_EPS = 1e-6
NEG = -1e30

VMEM_LIMIT = 56 * 2**20


def _cp(*sem, vmem=VMEM_LIMIT):
    return pltpu.CompilerParams(dimension_semantics=sem, vmem_limit_bytes=vmem)


def _dot(a, b):
    return jnp.dot(a, b, preferred_element_type=F32)


def _dot_nt(a, b):
    return lax.dot_general(a, b, (((1,), (1,)), ((), ())), preferred_element_type=F32)


def _dot_tn(a, b):
    return lax.dot_general(a, b, (((0,), (0,)), ((), ())), preferred_element_type=F32)


def _split3(a):
    hi = a.astype(BF16)
    r1 = a - hi.astype(F32)
    mid = r1.astype(BF16)
    lo = (r1 - mid.astype(F32)).astype(BF16)
    return hi, mid, lo


def _pick_tile(n, cap, mult=8):
    for t in range(min(cap, n), 0, -1):
        if n % t == 0 and t % mult == 0:
            return t
    return n


def _mm_kernel(*refs, ksplits):
    nx = len(ksplits)
    x_refs, w_ref, o_ref, wb_ref = refs[:nx], refs[nx], refs[nx + 1], refs[nx + 2]

    @pl.when(pl.program_id(1) == 0)
    def _():
        wb_ref[...] = w_ref[...].astype(BF16)

    acc = None
    k0 = 0
    for x_ref, kk in zip(x_refs, ksplits):
        part = _dot(x_ref[...].astype(BF16), wb_ref[k0:k0 + kk, :])
        acc = part if acc is None else acc + part
        k0 += kk
    o_ref[...] = acc.astype(o_ref.dtype)


def _matmul(xs, w, idx, col0, ncols, tn, out_dtype=F32, tm_cap=DENSE_TM):
    M = xs[0].shape[0]
    ksplits = tuple(x.shape[1] for x in xs)
    K = sum(ksplits)
    assert w.shape[1] == K and ncols % tn == 0 and col0 % tn == 0
    tm = _pick_tile(M, tm_cap)
    in_specs = [pl.BlockSpec((tm, kk), lambda j, i: (i, 0)) for kk in ksplits]
    in_specs.append(pl.BlockSpec((None, K, tn), lambda j, i: (idx, 0, col0 // tn + j)))
    return pl.pallas_call(
        functools.partial(_mm_kernel, ksplits=ksplits),
        grid=(ncols // tn, M // tm),
        in_specs=in_specs,
        out_specs=pl.BlockSpec((tm, tn), lambda j, i: (i, j)),
        out_shape=jax.ShapeDtypeStruct((M, ncols), out_dtype),
        scratch_shapes=[pltpu.VMEM((K, tn), BF16)],
        compiler_params=_cp("arbitrary", "arbitrary"),
        name="matmul",
    )(*xs, w)


PACK_ROWS = D_MODEL // 256
HI_MASK = 0xFFFF0000


def _pack_rows(ref, x):
    n = x.shape[0]
    half = D_MODEL // 2
    for j in range(PACK_ROWS):
        lo = x[:, 128 * j:128 * (j + 1)].astype(BF16).astype(F32)
        hi = x[:, half + 128 * j:half + 128 * (j + 1)].astype(BF16).astype(F32)
        word = (pltpu.bitcast(lo, jnp.uint32) >> 16) | (pltpu.bitcast(hi, jnp.uint32) & jnp.uint32(HI_MASK))
        ref[pl.ds(j, n, stride=PACK_ROWS), :] = word


def _unpack_rows(ref, n):
    lo, hi = [], []
    for j in range(PACK_ROWS):
        word = ref[pl.ds(j, n, stride=PACK_ROWS), :]
        lo.append(pltpu.bitcast(word << 16, F32))
        hi.append(pltpu.bitcast(word & jnp.uint32(HI_MASK), F32))
    return lo + hi


def _layer_norm_rows(y, g, b):
    mu = jnp.mean(y, axis=-1, keepdims=True)
    d = y - mu
    var = jnp.mean(d * d, axis=-1, keepdims=True)
    return d * lax.rsqrt(var + LN_EPS) * g + b


def _ln_kernel(h_ref, a_ref, g_ref, b_ref, o_ref, ob_ref, op_ref):
    out = _layer_norm_rows(DN_ALPHA * h_ref[...] + a_ref[...], g_ref[...], b_ref[...])
    o_ref[...] = out
    ob_ref[...] = out.astype(BF16)
    _pack_rows(op_ref, out)


def _residual_ln(h, add, g, b):
    M, Dm = h.shape
    tm = _pick_tile(M, 256)
    row = pl.BlockSpec((tm, Dm), lambda i: (i, 0))
    vec = pl.BlockSpec((1, Dm), lambda i: (0, 0))
    return pl.pallas_call(
        _ln_kernel,
        grid=(M // tm,),
        in_specs=[row, row, vec, vec],
        out_specs=[row, row, pl.BlockSpec((tm * PACK_ROWS, 128), lambda i: (i, 0))],
        out_shape=[jax.ShapeDtypeStruct((M, Dm), F32), jax.ShapeDtypeStruct((M, Dm), BF16),
                   jax.ShapeDtypeStruct((M * PACK_ROWS, 128), jnp.uint32)],
        compiler_params=_cp("arbitrary"),
        name="residual_ln",
    )(h, add, g.reshape(1, Dm), b.reshape(1, Dm))


def _ret_tables(C):
    hh = np.arange(RET_HEADS, dtype=np.float64)
    lg = np.log(1.0 - 2.0 ** (-5.0 - hh))[:, None]
    pos = np.arange(C, dtype=np.float64)[None, :]
    vecs = np.stack([np.exp(lg * (pos + 1.0)),
                     np.exp(lg * (C - pos)),
                     np.exp(lg * (C - 1.0 - pos)),
                     np.exp(lg * pos),
                     np.exp(lg * C) * np.ones_like(pos)], axis=1)
    tab = np.broadcast_to(vecs[..., None], (RET_HEADS, 5, C, RET_HD))
    rel = np.abs(pos.T - pos)
    dsym = np.exp(lg[:, :, None] * rel[None])
    return jnp.asarray(tab, F32), jnp.asarray(dsym, F32)


def _ret_kernel(q_ref, k_ref, v_ref, g_ref, tab_ref, d_ref, o_ref, fst_ref, run_ref, *, nc, C):
    s = pl.program_id(1)
    fwd = s < nc
    c = jnp.where(fwd, s, 2 * nc - 1 - s)
    row = c * C + lax.broadcasted_iota(jnp.int32, (C, 1), 0)
    live = row >= DEAD
    heads = range(RET_HEADS)
    HD = RET_HD
    ks = [jnp.where(live, k_ref[0, :, h * HD:(h + 1) * HD] * (RET_HD ** -0.5), 0.0) for h in heads]
    vbs = [jnp.where(live, v_ref[0, :, h * HD:(h + 1) * HD], 0.0).astype(BF16) for h in heads]

    @pl.when(jnp.logical_or(s == 0, s == nc))
    def _():
        run_ref[...] = jnp.zeros_like(run_ref)

    @pl.when(fwd)
    def _():
        for h in heads:
            fst_ref[c, h] = run_ref[h].astype(BF16)
        upd = [_dot_tn((ks[h] * tab_ref[h, 2]).astype(BF16), vbs[h]) for h in heads]
        for h in heads:
            run_ref[h] = run_ref[h] * tab_ref[h, 4] + upd[h]

    @pl.when(jnp.logical_not(fwd))
    def _():
        qs = [q_ref[0, :, h * HD:(h + 1) * HD] for h in heads]
        sc = [(_dot_nt(qs[h].astype(BF16), ks[h].astype(BF16)) * d_ref[h]).astype(BF16) for h in heads]
        left = [_dot((qs[h] * tab_ref[h, 0]).astype(BF16), fst_ref[c, h]) for h in heads]
        right = [_dot((qs[h] * tab_ref[h, 1]).astype(BF16), run_ref[h].astype(BF16)) for h in heads]
        intra = [_dot(sc[h], vbs[h]) for h in heads]
        upd = [_dot_tn((ks[h] * tab_ref[h, 3]).astype(BF16), vbs[h]) for h in heads]
        for h in heads:
            run_ref[h] = run_ref[h] * tab_ref[h, 4] + upd[h]
            out = intra[h] + left[h] + right[h]
            mu = jnp.mean(out, axis=-1, keepdims=True)
            dlt = out - mu
            var = jnp.mean(dlt * dlt, axis=-1, keepdims=True)
            normed = dlt * lax.rsqrt(var + NORM_EPS)
            g = g_ref[0, :, h * HD:(h + 1) * HD]
            o_ref[0, :, h * HD:(h + 1) * HD] = (g * jax.nn.sigmoid(g) * normed).astype(o_ref.dtype)


def _retention(proj):
    B, Lp, _ = proj.shape
    C = RET_CHUNK
    nc = Lp // C
    tab, dsym = _ret_tables(C)

    def cidx(s):
        return jnp.where(s < nc, s, 2 * nc - 1 - s)

    def cidx_out(s):
        return jnp.where(s < nc, nc - 1, 2 * nc - 1 - s)

    return pl.pallas_call(
        functools.partial(_ret_kernel, nc=nc, C=C),
        grid=(B, 2 * nc),
        in_specs=[
            pl.BlockSpec((1, C, RET_W), lambda b, s: (b, cidx_out(s), 0)),
            pl.BlockSpec((1, C, RET_W), lambda b, s: (b, cidx(s), 1)),
            pl.BlockSpec((1, C, RET_W), lambda b, s: (b, cidx(s), 2)),
            pl.BlockSpec((1, C, RET_W), lambda b, s: (b, cidx_out(s), 3)),
            pl.BlockSpec((RET_HEADS, 5, C, RET_HD), lambda b, s: (0, 0, 0, 0)),
            pl.BlockSpec((RET_HEADS, C, C), lambda b, s: (0, 0, 0)),
        ],
        out_specs=pl.BlockSpec((1, C, RET_W), lambda b, s: (b, cidx_out(s), 0)),
        out_shape=jax.ShapeDtypeStruct((B, Lp, RET_W), BF16),
        scratch_shapes=[pltpu.VMEM((nc, RET_HEADS, RET_HD, RET_HD), BF16),
                        pltpu.VMEM((RET_HEADS, RET_HD, RET_HD), F32)],
        compiler_params=_cp("arbitrary", "arbitrary"),
        name="retention",
    )(proj, proj, proj, proj, tab, dsym)


def _att_kernel(sink_ref, slope_ref, q_ref, km_ref, kp_ref, kc_ref, kn_ref,
                vm_ref, vp_ref, vc_ref, vn_ref, o_ref, *, nb):
    j = pl.program_id(1)
    qb = pl.program_id(2)
    T = ATT_BLOCK
    ii = lax.broadcasted_iota(jnp.int32, (T, T), 0)
    jj = lax.broadcasted_iota(jnp.int32, (T, T), 1)
    meta_ok = jj >= DEAD
    pieces = []
    for off, k_ref, v_ref, ok in ((-T, kp_ref, vp_ref, qb >= 2),
                                  (0, kc_ref, vc_ref, qb >= 1),
                                  (T, kn_ref, vn_ref, qb + 1 <= nb)):
        dist = jnp.abs(jj + off - ii)
        pieces.append((dist.astype(F32), dist <= WINDOW, k_ref[0].astype(BF16), v_ref[0].astype(BF16), ok))
    km = km_ref[0].astype(BF16)
    vm = vm_ref[0].astype(BF16)
    vals = [vm] + [p[3] for p in pieces]
    groups = range(ATT_GROUP)
    qs = [(q_ref[0, :, g * ATT_HD:(g + 1) * ATT_HD] * (ATT_HD ** -0.5)).astype(BF16) for g in groups]
    raw = [[_dot_nt(qs[g], km)] + [_dot_nt(qs[g], p[2]) for p in pieces] for g in groups]
    es, dens = [], []
    for g in groups:
        slope = slope_ref[j, g]
        sink = sink_ref[j, g]
        s_list = [jnp.where(meta_ok, raw[g][0], NEG)]
        for (dist, inwin, _, _, ok), r in zip(pieces, raw[g][1:]):
            s_list.append(jnp.where(ok, jnp.where(inwin, r - slope * dist, NEG), NEG))
        m = jnp.full((T, 1), sink, F32)
        for sb in s_list:
            m = jnp.maximum(m, jnp.max(sb, axis=-1, keepdims=True))
        e_list = [jnp.exp(sb - m) for sb in s_list]
        den = jnp.exp(sink - m)
        for e in e_list:
            den = den + jnp.sum(e, axis=-1, keepdims=True)
        es.append([e.astype(BF16) for e in e_list])
        dens.append(den)
    pv = [[_dot(e, vb) for e, vb in zip(es[g], vals)] for g in groups]
    for g in groups:
        acc = pv[g][0] + pv[g][1] + pv[g][2] + pv[g][3]
        o_ref[0, :, g * ATT_HD:(g + 1) * ATT_HD] = (acc / dens[g]).astype(o_ref.dtype)


def _window_attention(proj, sink):
    B, Lp, _ = proj.shape
    nb = Lp // ATT_BLOCK - 1
    gw = ATT_GROUP * ATT_HD
    q0 = 4 * RET_W // gw
    k0 = (4 * RET_W + ATT_HEADS * ATT_HD) // ATT_HD
    v0 = k0 + ATT_KV_HEADS
    slopes = np.asarray(2.0 ** (-8.0 * (np.arange(ATT_HEADS) + 1.0) / ATT_HEADS), np.float32)
    smem = pl.BlockSpec(memory_space=pltpu.SMEM)

    def kv(c0, f):
        return pl.BlockSpec((1, ATT_BLOCK, ATT_HD), lambda b, j, t: (b, f(t), c0 + j))

    rows = [lambda t: 0, lambda t: jnp.maximum(t - 1, 0), lambda t: t, lambda t: jnp.minimum(t + 1, nb)]
    return pl.pallas_call(
        functools.partial(_att_kernel, nb=nb),
        grid=(B, ATT_KV_HEADS, nb + 1),
        in_specs=[smem, smem, pl.BlockSpec((1, ATT_BLOCK, gw), lambda b, j, t: (b, t, q0 + j))]
        + [kv(k0, f) for f in rows] + [kv(v0, f) for f in rows],
        out_specs=pl.BlockSpec((1, ATT_BLOCK, gw), lambda b, j, t: (b, t, j)),
        out_shape=jax.ShapeDtypeStruct((B, Lp, ATT_HEADS * ATT_HD), BF16),
        compiler_params=_cp("arbitrary", "arbitrary", "arbitrary"),
        name="window_attention",
    )(sink.astype(F32).reshape(ATT_KV_HEADS, ATT_GROUP), jnp.asarray(slopes).reshape(ATT_KV_HEADS, ATT_GROUP),
      *([proj] * 9))


CONV_HALO = 16


def _conv_shift_matrix(TT):
    half = CONV_WIDTH // 2
    t = np.arange(TT)[:, None]
    s = np.arange(TT + 2 * CONV_HALO)[None, :]
    blocks = [(s == t + CONV_HALO + tap - half) for tap in range(CONV_WIDTH) if tap != half]
    return jnp.asarray(np.concatenate(blocks, axis=0), BF16)


def _conv_kernel(xp_ref, x_ref, xn_ref, w_ref, sh_ref, o_ref, *, mode, TT, nt):
    t = pl.program_id(1)
    H = CONV_HALO
    rows = t * TT + lax.broadcasted_iota(jnp.int32, (TT, 1), 0)
    live = rows >= DEAD
    x = jnp.where(live, x_ref[0].astype(F32), 0.0)
    prow = t * TT - H + lax.broadcasted_iota(jnp.int32, (H, 1), 0)
    prev = jnp.where(prow >= DEAD, xp_ref[0].astype(F32), 0.0)
    nrow = (t + 1) * TT + lax.broadcasted_iota(jnp.int32, (H, 1), 0)
    nxt = jnp.where(jnp.logical_and(nrow >= DEAD, t < nt - 1), xn_ref[0].astype(F32), 0.0)
    staged = jnp.concatenate([prev.astype(BF16), x.astype(BF16), nxt.astype(BF16)], axis=0)
    shifted = _dot(sh_ref[...], staged)
    w = w_ref[...]
    half = CONV_WIDTH // 2
    acc = x * w[half:half + 1, :]
    blk = 0
    for tap in range(CONV_WIDTH):
        if tap == half:
            continue
        acc = acc + shifted[blk * TT:(blk + 1) * TT, :] * w[tap:tap + 1, :]
        blk += 1
    y = acc * jax.nn.sigmoid(acc)
    if mode in ("q", "k"):
        scale = GDN_HD ** -0.5 if mode == "q" else 1.0
        for a in range(y.shape[1] // GDN_HD):
            ya = y[:, a * GDN_HD:(a + 1) * GDN_HD]
            ya = ya * lax.rsqrt(jnp.sum(ya * ya, axis=-1, keepdims=True) + NORM_EPS)
            if mode == "q":
                ya = ya * scale
            o_ref[0, :, a * GDN_HD:(a + 1) * GDN_HD] = jnp.where(live, ya, 0.0).astype(o_ref.dtype)
    else:
        o_ref[0] = jnp.where(live, y, 0.0).astype(o_ref.dtype)


def _short_conv(qkv, conv_w, mode, col0, ncols):
    B, Lp, _ = qkv.shape
    H = CONV_HALO
    TT = _pick_tile(Lp, 256, mult=H)
    TC = 1024
    nt = Lp // TT
    c0 = col0 // TC
    shift = _conv_shift_matrix(TT)
    return pl.pallas_call(
        functools.partial(_conv_kernel, mode=mode, TT=TT, nt=nt),
        grid=(B, nt, ncols // TC),
        in_specs=[
            pl.BlockSpec((1, H, TC), lambda b, t, c: (b, jnp.maximum(t * (TT // H) - 1, 0), c0 + c)),
            pl.BlockSpec((1, TT, TC), lambda b, t, c: (b, t, c0 + c)),
            pl.BlockSpec((1, H, TC), lambda b, t, c: (b, jnp.minimum((t + 1) * (TT // H), Lp // H - 1), c0 + c)),
            pl.BlockSpec((CONV_WIDTH, TC), lambda b, t, c: (0, c0 + c)),
            pl.BlockSpec(shift.shape, lambda b, t, c: (0, 0)),
        ],
        out_specs=pl.BlockSpec((1, TT, TC), lambda b, t, c: (b, t, c)),
        out_shape=jax.ShapeDtypeStruct((B, Lp, ncols), BF16),
        compiler_params=_cp("arbitrary", "arbitrary", "arbitrary"),
        name="short_conv_" + mode,
    )(qkv, qkv, qkv, conv_w, shift)


def _gate_kernel(ba_ref, alog_ref, dtb_ref, o_ref, *, TT):
    t = pl.program_id(1)
    rows = t * TT + lax.broadcasted_iota(jnp.int32, (TT, 1), 0)
    lane = lax.broadcasted_iota(jnp.int32, (1, 4 * GDN_V_HEADS), 1)
    is_g = (lane // GDN_V_HEADS) % 2 == 1
    x = ba_ref[0]
    beta = jax.nn.sigmoid(x)
    xs = x + dtb_ref[...]
    softplus = jnp.maximum(xs, 0.0) + jnp.log(1.0 + jnp.exp(-jnp.abs(xs)))
    g = -jnp.exp(alog_ref[...]) * softplus
    o_ref[0] = jnp.where(rows >= DEAD, jnp.where(is_g, g, beta), 0.0)


def _gates(ba, a_log, dt_bias):
    B, Lp, W = ba.shape
    TT = _pick_tile(Lp, 1024)
    zeros = jnp.zeros((2, 1, GDN_V_HEADS), F32)
    alog = jnp.concatenate([zeros, a_log.astype(F32)[:, None, :]], axis=1).reshape(1, W)
    dtb = jnp.concatenate([zeros, dt_bias.astype(F32)[:, None, :]], axis=1).reshape(1, W)
    vec = pl.BlockSpec((1, W), lambda b, t: (0, 0))
    return pl.pallas_call(
        functools.partial(_gate_kernel, TT=TT),
        grid=(B, Lp // TT),
        in_specs=[pl.BlockSpec((1, TT, W), lambda b, t: (b, t, 0)), vec, vec],
        out_specs=pl.BlockSpec((1, TT, W), lambda b, t: (b, t, 0)),
        out_shape=jax.ShapeDtypeStruct((B, Lp, W), F32),
        compiler_params=_cp("arbitrary", "arbitrary"),
        name="gdn_gates",
    )(ba, alog, dtb)


def _gdn_kernel(*refs, rev, final):
    if final:
        q_ref, k_ref, v_ref, gc_ref, gr_ref, of_ref, z_ref, nw_ref, o_ref, s_ref = refs
    else:
        q_ref, k_ref, v_ref, gc_ref, gr_ref, o_ref, s_ref = refs
    C = GDN_CHUNK
    HG = GDN_HG

    @pl.when(pl.program_id(2) == 0)
    def _():
        s_ref[...] = jnp.zeros_like(s_ref)

    ii = lax.broadcasted_iota(jnp.int32, (C, C), 0)
    jj = lax.broadcasted_iota(jnp.int32, (C, C), 1)
    incl = (jj >= ii) if rev else (jj <= ii)
    strict = (jj > ii) if rev else (jj < ii)
    tri_col = jnp.where(incl, 1.0, 0.0).astype(BF16)
    tri_row = jnp.where((ii >= jj) if rev else (ii <= jj), 1.0, 0.0).astype(BF16)
    gcol = gc_ref[0, 0]
    grow = gr_ref[0, 0, 0]
    gc_col = sum(_dot(tri_col, p) for p in _split3(gcol))
    gc_row = sum(_dot(p, tri_row) for p in _split3(grow))
    last = 0 if rev else C - 1
    d0 = 2 * HG if rev else 0
    HD = GDN_HD
    heads = range(HG)
    pairs = range(HG // 2)
    qs = [q_ref[0, :, p * HD:(p + 1) * HD] for p in pairs]
    ks = [k_ref[0, :, p * HD:(p + 1) * HD] for p in pairs]
    kfs = [k.astype(F32) for k in ks]
    kq = [_dot_nt(jnp.concatenate([ks[p], qs[p]], axis=0), ks[p]) for p in pairs]
    gram = [r[:C] for r in kq]
    qk = [r[C:] for r in kq]
    beta = [gcol[:, d0 + h:d0 + h + 1] for h in heads]
    gcc = [gc_col[:, d0 + HG + h:d0 + HG + h + 1] for h in heads]
    gcr = [gc_row[d0 + HG + h:d0 + HG + h + 1, :] for h in heads]
    gtot = [g[last:last + 1, :] for g in gcc]
    decay = [jnp.where(incl, jnp.exp(jnp.where(incl, gcc[h] - gcr[h], 0.0)), 0.0) for h in heads]
    eg = [jnp.exp(g) for g in gcc]
    ms = [-jnp.where(strict, beta[h] * gram[h // 2] * decay[h], 0.0) for h in heads]
    pbs = [m.astype(BF16) for m in ms]
    ps = [_dot(pb, pb) for pb in pbs]
    pbs = [p.astype(BF16) for p in ps]
    for _ in range(int(math.log2(C)) - 2):
        res = [_dot(jnp.concatenate([ms[h].astype(BF16), pbs[h]], axis=0), pbs[h]) for h in heads]
        ms = [ms[h] + ps[h] + res[h][:C] for h in heads]
        ps = [r[C:] for r in res]
        pbs = [p.astype(BF16) for p in ps]
    ms = [ms[h] + ps[h] + _dot(ms[h].astype(BF16), pbs[h]) for h in heads]
    rhs = [jnp.concatenate([v_ref[0, :, h * HD:(h + 1) * HD].astype(F32) * beta[h],
                            kfs[h // 2] * (beta[h] * eg[h])], axis=1) for h in heads]
    sol = [rhs[h] + _dot(ms[h].astype(BF16), rhs[h].astype(BF16)) for h in heads]
    st = [s_ref[h] for h in heads]
    stb = [s.astype(BF16) for s in st]
    lhs = [jnp.concatenate([sol[h][:, HD:].astype(BF16), (qs[h // 2].astype(F32) * eg[h]).astype(BF16)], axis=0)
           for h in heads]
    ws = [_dot(lhs[h], stb[h]) for h in heads]
    vnb = [(sol[h][:, :HD] - ws[h][:C]).astype(BF16) for h in heads]
    outs = [ws[h][C:] + _dot((qk[h // 2] * decay[h]).astype(BF16), vnb[h]) for h in heads]
    upd = [_dot_tn((kfs[h // 2] * jnp.exp(gtot[h] - gcc[h])).astype(BF16), vnb[h]) for h in heads]
    for h in heads:
        s_ref[h] = st[h] * jnp.exp(gtot[h]) + upd[h]
        sl = slice(h * HD, (h + 1) * HD)
        if final:
            o = outs[h] + of_ref[0, :, sl]
            z = z_ref[0, :, sl]
            o = o * lax.rsqrt(jnp.mean(o * o, axis=-1, keepdims=True) + NORM_EPS) * nw_ref[...]
            o_ref[0, :, sl] = (o * (z * jax.nn.sigmoid(z))).astype(o_ref.dtype)
        else:
            o_ref[0, :, sl] = outs[h]


def _gdn_dir(q, k, v, gcol, grow, rev, o_fwd=None, z=None, norm_w=None):
    B, Lp, _ = v.shape
    C = GDN_CHUNK
    nc = Lp // C
    ng = GDN_V_HEADS // GDN_HG
    qw = GDN_HG // 2 * GDN_HD
    vw = GDN_HG * GDN_HD
    final = o_fwd is not None

    def cc(c):
        return nc - 1 - c if rev else c

    in_specs = [
        pl.BlockSpec((1, C, qw), lambda b, g, c: (b, cc(c), g)),
        pl.BlockSpec((1, C, qw), lambda b, g, c: (b, cc(c), g)),
        pl.BlockSpec((1, C, vw), lambda b, g, c: (b, cc(c), g)),
        pl.BlockSpec((1, 1, C, 4 * GDN_HG), lambda b, g, c: (b, g, cc(c), 0)),
        pl.BlockSpec((1, 1, 1, 4 * GDN_HG, C), lambda b, g, c: (b, g, cc(c), 0, 0)),
    ]
    args = [q, k, v, gcol, grow]
    if final:
        in_specs += [pl.BlockSpec((1, C, vw), lambda b, g, c: (b, cc(c), g)),
                     pl.BlockSpec((1, C, vw), lambda b, g, c: (b, cc(c), g)),
                     pl.BlockSpec((1, GDN_HD), lambda b, g, c: (0, 0))]
        args += [o_fwd, z, norm_w.astype(F32).reshape(1, GDN_HD)]
    return pl.pallas_call(
        functools.partial(_gdn_kernel, rev=rev, final=final),
        grid=(B, ng, nc),
        in_specs=in_specs,
        out_specs=pl.BlockSpec((1, C, vw), lambda b, g, c: (b, cc(c), g)),
        out_shape=jax.ShapeDtypeStruct((B, Lp, GDN_V_W), BF16 if final else F32),
        scratch_shapes=[pltpu.VMEM((GDN_HG, GDN_HD, GDN_HD), F32)],
        compiler_params=_cp("arbitrary", "arbitrary", "arbitrary"),
        name="gdn_bwd" if rev else "gdn_fwd",
    )(*args)


def _gdn_mixer(hb, B, Lp, i, w_in, conv_w, a_log, dt_bias, norm_w, w_out):
    qkv = _matmul([hb], w_in, i, 0, GDN_CONV_CH, 1024, out_dtype=BF16).reshape(B, Lp, GDN_CONV_CH)
    z = _matmul([hb], w_in, i, GDN_CONV_CH, GDN_V_W, 1024).reshape(B, Lp, GDN_V_W)
    ba = _matmul([hb], w_in, i, GDN_CONV_CH + GDN_V_W, 4 * GDN_V_HEADS, 128).reshape(B, Lp, 4 * GDN_V_HEADS)
    q = _short_conv(qkv, conv_w, "q", 0, GDN_K_W)
    k = _short_conv(qkv, conv_w, "k", GDN_K_W, GDN_K_W)
    v = _short_conv(qkv, conv_w, "v", 2 * GDN_K_W, GDN_V_W)
    gb = _gates(ba, a_log, dt_bias)
    ng = GDN_V_HEADS // GDN_HG
    nc = Lp // GDN_CHUNK
    gcol = gb.reshape(B, Lp, 4, ng, GDN_HG).transpose(0, 3, 1, 2, 4).reshape(B, ng, Lp, 4 * GDN_HG)
    grow = gcol.reshape(B, ng, nc, GDN_CHUNK, 4 * GDN_HG).transpose(0, 1, 2, 4, 3)
    o_f = _gdn_dir(q, k, v, gcol, grow, rev=False)
    o = _gdn_dir(q, k, v, gcol, grow, rev=True, o_fwd=o_f, z=z, norm_w=norm_w)
    return _matmul([o.reshape(B * Lp, GDN_V_W)], w_out, i, 0, D_MODEL, 512)


def _even_mixer(hb, B, Lp, i, w_in, w_out, sink):
    proj = _matmul([hb], w_in, i, 0, EVEN_IN, 512).reshape(B, Lp, EVEN_IN)
    ret = _retention(proj).reshape(B * Lp, RET_W)
    att = _window_attention(proj, sink).reshape(B * Lp, ATT_HEADS * ATT_HD)
    return _matmul([ret, att], w_out, i, 0, D_MODEL, 1024)


def _router_kernel(x_ref, wt_ref, bias_ref, eidx_ref, wts_ref, rank_ref, cnt_ref, carry_ref, *, TM):
    E = N_EXPERTS

    @pl.when(pl.program_id(0) == 0)
    def _():
        carry_ref[...] = jnp.zeros_like(carry_ref)

    xh, xm, _ = _split3(x_ref[...])
    wh, wm, _ = _split3(wt_ref[...])
    lead = _dot_nt(jnp.concatenate([wh, wm], axis=0), xh)
    logits = lead[:E] + (lead[E:] + _dot_nt(wh, xm))
    scores = jax.nn.sigmoid(logits)
    choice = scores + bias_ref[...]
    ninf = -jnp.inf
    io8 = lax.broadcasted_iota(jnp.int32, (GROUP_SIZE, TM), 0)
    gs_rows = []
    for g in range(N_GROUPS):
        cg = choice[g * GROUP_SIZE:(g + 1) * GROUP_SIZE, :]
        m1 = jnp.max(cg, axis=0, keepdims=True)
        i1 = jnp.min(jnp.where(cg == m1, io8, GROUP_SIZE), axis=0, keepdims=True)
        m2 = jnp.max(jnp.where(io8 == i1, ninf, cg), axis=0, keepdims=True)
        gs_rows.append(m1 + m2)
    gs = jnp.concatenate(gs_rows, axis=0)
    gsel = jnp.zeros((N_GROUPS, TM), jnp.int32)
    for _ in range(TOPK_GROUPS):
        m = jnp.max(gs, axis=0, keepdims=True)
        idx = jnp.min(jnp.where(gs == m, io8, N_GROUPS), axis=0, keepdims=True)
        hit = io8 == idx
        gsel = jnp.where(hit, 1, gsel)
        gs = jnp.where(hit, ninf, gs)
    masked = jnp.concatenate(
        [jnp.where(gsel[g:g + 1, :] > 0, choice[g * GROUP_SIZE:(g + 1) * GROUP_SIZE, :], ninf)
         for g in range(N_GROUPS)], axis=0)
    ioe = lax.broadcasted_iota(jnp.int32, (E, TM), 0)
    sel = jnp.zeros((E, TM), F32)
    idx_rows, w_rows = [], []
    for _ in range(TOP_K):
        m = jnp.max(masked, axis=0, keepdims=True)
        idx = jnp.min(jnp.where(masked == m, ioe, E), axis=0, keepdims=True)
        hit = ioe == idx
        idx_rows.append(idx)
        w_rows.append(jnp.sum(jnp.where(hit, scores, 0.0), axis=0, keepdims=True))
        sel = jnp.where(hit, 1.0, sel)
        masked = jnp.where(hit, ninf, masked)
    wsum = w_rows[0]
    for w in w_rows[1:]:
        wsum = wsum + w
    ti = lax.broadcasted_iota(jnp.int32, (TM, TM), 0)
    tj = lax.broadcasted_iota(jnp.int32, (TM, TM), 1)
    before = jnp.where(ti < tj, 1.0, 0.0).astype(BF16)
    rank = _dot(sel.astype(BF16), before) + carry_ref[:, 0:1]
    rank_rows = [jnp.sum(jnp.where(ioe == idx, rank, 0.0), axis=0, keepdims=True) for idx in idx_rows]
    eidx_ref[...] = jnp.concatenate(idx_rows, axis=0)
    wts_ref[...] = jnp.concatenate([w / wsum * ROUTE_SCALE for w in w_rows], axis=0)
    rank_ref[...] = jnp.concatenate(rank_rows, axis=0).astype(jnp.int32)
    carry_ref[...] = carry_ref[...] + jnp.sum(sel, axis=1, keepdims=True)
    cnt_ref[...] = carry_ref[...]


def _router(h, w_router, router_bias):
    T, Dm = h.shape
    TM = _pick_tile(T, 256)
    tok = pl.BlockSpec((TOP_K, TM), lambda i: (0, i))
    return pl.pallas_call(
        functools.partial(_router_kernel, TM=TM),
        grid=(T // TM,),
        in_specs=[pl.BlockSpec((TM, Dm), lambda i: (i, 0)),
                  pl.BlockSpec((N_EXPERTS, Dm), lambda i: (0, 0)),
                  pl.BlockSpec((N_EXPERTS, 1), lambda i: (0, 0))],
        out_specs=[tok, tok, tok, pl.BlockSpec((N_EXPERTS, 128), lambda i: (0, 0))],
        out_shape=[jax.ShapeDtypeStruct((TOP_K, T), jnp.int32), jax.ShapeDtypeStruct((TOP_K, T), F32),
                   jax.ShapeDtypeStruct((TOP_K, T), jnp.int32), jax.ShapeDtypeStruct((N_EXPERTS, 128), F32)],
        scratch_shapes=[pltpu.VMEM((N_EXPERTS, 128), F32)],
        compiler_params=_cp("arbitrary"),
        name="moe_router",
    )(h, w_router.T, router_bias.astype(F32).reshape(N_EXPERTS, 1))


def _dispatch_kernel(dest_hbm, fill_hbm, hp_hbm, xs_hbm, dest_smem, fill_smem, zrow_ref, xbuf_ref,
                     sem, tsem, isem, fsem, zsem, *, TM, nt, nfill):
    i = pl.program_id(0)
    cp = pltpu.make_async_copy(dest_hbm.at[i], dest_smem, isem)
    cp.start()

    R = PACK_ROWS

    def slot_rows(d):
        return xs_hbm.at[pl.ds(pl.multiple_of(d * R, R), R)]

    @pl.when(i == 0)
    def _():
        zrow_ref[...] = jnp.zeros_like(zrow_ref)
        fc = pltpu.make_async_copy(fill_hbm, fill_smem, fsem)
        fc.start()
        fc.wait()

        def per_expert(e, carry):
            first = fill_smem[e]

            def body(r, c):
                pltpu.make_async_copy(zrow_ref, slot_rows(first + r), zsem).start()
                return c

            return lax.fori_loop(0, fill_smem[N_EXPERTS + e], body, carry)

        lax.fori_loop(0, N_EXPERTS, per_expert, 0)

        def per_expert_wait(e, carry):
            def wbody(r, c):
                pltpu.make_async_copy(zrow_ref, slot_rows(0), zsem).wait()
                return c

            return lax.fori_loop(0, fill_smem[N_EXPERTS + e], wbody, carry)

        lax.fori_loop(0, N_EXPERTS, per_expert_wait, 0)

    def tile_copy(tile, b):
        rows = pl.ds(pl.multiple_of(tile * (TM * R), TM * R), TM * R)
        return pltpu.make_async_copy(hp_hbm.at[rows], xbuf_ref.at[b], tsem.at[b])

    @pl.when(i == 0)
    def _():
        tile_copy(0, 0).start()
        if nt > 1:
            tile_copy(1, 1).start()

    cp.wait()
    cur = i % 3
    tile_copy(i, cur).wait()

    def issue(grp, carry):
        t0 = grp * ISSUE_GROUP
        base = t0 * TOP_K
        slots = [[dest_smem[base + (u * TOP_K + k)] for k in range(TOP_K)] for u in range(ISSUE_GROUP)]
        for u in range(ISSUE_GROUP):
            src = xbuf_ref.at[cur, pl.ds(pl.multiple_of((t0 + u) * R, R), R)]
            for k in range(TOP_K):
                pltpu.make_async_copy(src, slot_rows(slots[u][k]), sem.at[cur]).start(priority=k % 2)
        return carry

    lax.fori_loop(0, TM // ISSUE_GROUP, issue, 0)

    def retire(b):
        def drain(t, carry):
            for k in range(TOP_K):
                pltpu.make_async_copy(xbuf_ref.at[b, pl.ds(0, R)], slot_rows(0), sem.at[b]).wait()
            return carry

        lax.fori_loop(0, TM, drain, 0)

    @pl.when(i > 0)
    def _():
        retire((i + 2) % 3)

    @pl.when(i + 2 < nt)
    def _():
        tile_copy(i + 2, (i + 2) % 3).start()

    @pl.when(i == nt - 1)
    def _():
        retire(cur)


def _dispatch(hp, dest_tiles, fill, rows):
    nt = dest_tiles.shape[0]
    TM = dest_tiles.shape[1] // TOP_K
    nfill = fill.shape[0]
    return pl.pallas_call(
        functools.partial(_dispatch_kernel, TM=TM, nt=nt, nfill=nfill),
        grid=(nt,),
        in_specs=[pl.BlockSpec(memory_space=pl.ANY), pl.BlockSpec(memory_space=pl.ANY),
                  pl.BlockSpec(memory_space=pl.ANY)],
        out_specs=pl.BlockSpec(memory_space=pl.ANY),
        out_shape=jax.ShapeDtypeStruct((rows * PACK_ROWS, 128), jnp.uint32),
        scratch_shapes=[pltpu.SMEM((TOP_K * TM,), jnp.int32), pltpu.SMEM((nfill,), jnp.int32),
                        pltpu.VMEM((PACK_ROWS, 128), jnp.uint32),
                        pltpu.VMEM((3, TM * PACK_ROWS, 128), jnp.uint32),
                        pltpu.SemaphoreType.DMA((3,)), pltpu.SemaphoreType.DMA((3,)),
                        pltpu.SemaphoreType.DMA, pltpu.SemaphoreType.DMA, pltpu.SemaphoreType.DMA],
        compiler_params=_cp("arbitrary"),
        name="moe_dispatch",
    )(dest_tiles, fill, hp)


def _expert_kernel(be_ref, na_ref, x_ref, wg_ref, wu_ref, wd_ref, o_ref, wgb_ref, wub_ref, wdb_ref):
    i = pl.program_id(0)
    active = i < na_ref[0]
    new_expert = jnp.logical_or(i == 0, be_ref[i] != be_ref[jnp.maximum(i - 1, 0)])

    @pl.when(jnp.logical_and(active, new_expert))
    def _():
        wgb_ref[...] = wg_ref[0, 0].astype(BF16)
        wub_ref[...] = wu_ref[0, 0].astype(BF16)
        wdb_ref[...] = wd_ref[0, 0].astype(BF16)

    @pl.when(active)
    def _():
        x = jnp.concatenate([c.astype(BF16) for c in _unpack_rows(x_ref, MOE_BLOCK)], axis=1)
        a = _dot(x, wgb_ref[...])
        b = _dot(x, wub_ref[...])
        hmid = (a * jax.nn.sigmoid(a) * b).astype(BF16)
        _pack_rows(o_ref, _dot(hmid, wdb_ref[...]))


def _experts(xs, blk_e, nact, w_gate, w_up, w_down, layer):
    Dm = D_MODEL
    rows = xs.shape[0] // PACK_ROWS
    nblk = rows // MOE_BLOCK
    pblock = MOE_BLOCK * PACK_ROWS

    def row(i, be, na):
        return (jnp.minimum(i, na[0] - 1), 0)

    def wsel(i, be, na):
        return (layer, be[jnp.minimum(i, na[0] - 1)], 0, 0)

    return pl.pallas_call(
        _expert_kernel,
        grid_spec=pltpu.PrefetchScalarGridSpec(
            num_scalar_prefetch=2,
            grid=(nblk,),
            in_specs=[pl.BlockSpec((pblock, 128), row),
                      pl.BlockSpec((1, 1, Dm, D_EXPERT), wsel),
                      pl.BlockSpec((1, 1, Dm, D_EXPERT), wsel),
                      pl.BlockSpec((1, 1, D_EXPERT, Dm), wsel)],
            out_specs=pl.BlockSpec((pblock, 128), row),
            scratch_shapes=[pltpu.VMEM((Dm, D_EXPERT), BF16), pltpu.VMEM((Dm, D_EXPERT), BF16),
                            pltpu.VMEM((D_EXPERT, Dm), BF16)],
        ),
        out_shape=jax.ShapeDtypeStruct((rows * PACK_ROWS, 128), jnp.uint32),
        compiler_params=_cp("arbitrary"),
        name="moe_experts",
    )(blk_e, nact, xs, w_gate, w_up, w_down)


def _combine_kernel(dest_hbm, yb_hbm, w_ref, h_ref, sh_ref, g_ref, b_ref, o_ref, ob_ref,
                    dest_smem, buf_ref, routed_ref, sem, isem, *, TM, nt):
    i = pl.program_id(0)
    R = PACK_ROWS
    G = 8
    half = D_MODEL // 2

    def load_slots(tile, slot):
        cp = pltpu.make_async_copy(dest_hbm.at[tile],
                                   dest_smem.at[pl.ds(pl.multiple_of(slot * (TM * TOP_K), TM * TOP_K), TM * TOP_K)], isem)
        cp.start()
        cp.wait()

    def issue_group(slot, grp):
        t0 = grp * G
        base = slot * (TM * TOP_K) + t0 * TOP_K
        slots = [[dest_smem[base + (u * TOP_K + k)] for k in range(TOP_K)] for u in range(G)]
        for u in range(G):
            for k in range(TOP_K):
                src = yb_hbm.at[pl.ds(pl.multiple_of(slots[u][k] * R, R), R)]
                dst = buf_ref.at[slot, k, pl.ds(pl.multiple_of((t0 + u) * R, R), R)]
                pltpu.make_async_copy(src, dst, sem.at[slot]).start(priority=k % 2)

    @pl.when(i == 0)
    def _():
        load_slots(0, 0)

        def first(grp, carry):
            issue_group(0, grp)
            return carry

        lax.fori_loop(0, TM // G, first, 0)

    slot = i % 2
    nslot = (i + 1) % 2
    load_slots(jnp.minimum(i + 1, nt - 1), nslot)

    def retire(b):
        def drain(t, carry):
            for k in range(TOP_K):
                pltpu.make_async_copy(yb_hbm.at[pl.ds(0, R)], buf_ref.at[b, k, pl.ds(0, R)], sem.at[b]).wait()
            return carry

        lax.fori_loop(0, TM, drain, 0)

    retire(slot)

    def step(grp, carry):
        t0 = grp * G
        base = nslot * (TM * TOP_K) + t0 * TOP_K
        slots = [[dest_smem[base + (u * TOP_K + k)] for k in range(TOP_K)] for u in range(G)]
        r0 = pl.multiple_of(grp * G, G)
        w = w_ref[pl.ds(r0, G), :]
        lo = [None] * R
        hi = [None] * R
        for k in range(TOP_K):
            wk = w[:, k:k + 1]
            words = [buf_ref[slot, k, pl.ds(grp * (G * R) + j, G, stride=R), :] for j in range(R)]
            u = k
            for kk in range(TOP_K):
                src = yb_hbm.at[pl.ds(pl.multiple_of(slots[u][kk] * R, R), R)]
                dst = buf_ref.at[nslot, kk, pl.ds(pl.multiple_of((t0 + u) * R, R), R)]
                pltpu.make_async_copy(src, dst, sem.at[nslot]).start(priority=kk % 2)
            for j in range(R):
                a = pltpu.bitcast(words[j] << 16, F32) * wk
                b = pltpu.bitcast(words[j] & jnp.uint32(HI_MASK), F32) * wk
                lo[j] = a if lo[j] is None else lo[j] + a
                hi[j] = b if hi[j] is None else hi[j] + b
        for j in range(R):
            routed_ref[pl.ds(r0, G), 128 * j:128 * (j + 1)] = lo[j]
            routed_ref[pl.ds(r0, G), half + 128 * j:half + 128 * (j + 1)] = hi[j]
        return carry

    lax.fori_loop(0, TM // G, step, 0)

    @pl.when(i == nt - 1)
    def _():
        retire(nslot)

    out = _layer_norm_rows(DN_ALPHA * h_ref[...] + (routed_ref[...] + sh_ref[...]), g_ref[...], b_ref[...])
    o_ref[...] = out
    ob_ref[...] = out.astype(BF16)


def _combine_ln(yb, dest_tiles, wts_tok, h, shared, g, b, drop_front=None):
    nt = dest_tiles.shape[0]
    TM = dest_tiles.shape[1] // TOP_K
    T, Dm = h.shape
    row = pl.BlockSpec((TM, Dm), lambda i: (i, 0))
    vec = pl.BlockSpec((1, Dm), lambda i: (0, 0))
    hbm = pl.BlockSpec(memory_space=pl.ANY)
    out_row, out_rows = row, T
    if drop_front is not None:
        assert TM == FRONT
        npb = drop_front
        out_row = pl.BlockSpec((TM, Dm), lambda i: ((i // npb) * (npb - 1) + jnp.maximum(i % npb - 1, 0), 0))
        out_rows = T - (T // (npb * TM)) * FRONT
    return pl.pallas_call(
        functools.partial(_combine_kernel, TM=TM, nt=nt),
        grid=(nt,),
        in_specs=[hbm, hbm, pl.BlockSpec((TM, TOP_K), lambda i: (i, 0)), row, row, vec, vec],
        out_specs=[out_row, row],
        out_shape=[jax.ShapeDtypeStruct((out_rows, Dm), F32), jax.ShapeDtypeStruct((T, Dm), BF16)],
        scratch_shapes=[pltpu.SMEM((2 * TOP_K * TM,), jnp.int32),
                        pltpu.VMEM((2, TOP_K, TM * PACK_ROWS, 128), jnp.uint32),
                        pltpu.VMEM((TM, Dm), F32),
                        pltpu.SemaphoreType.DMA((2,)), pltpu.SemaphoreType.DMA],
        compiler_params=_cp("arbitrary"),
        name="moe_combine",
    )(dest_tiles, yb, wts_tok, h, shared, g.reshape(1, Dm), b.reshape(1, Dm))


def _shared_kernel(x_ref, wg_ref, wu_ref, wd_ref, o_ref, wgb_ref, wub_ref, wdb_ref):
    @pl.when(pl.program_id(0) == 0)
    def _():
        wgb_ref[...] = wg_ref[...].astype(BF16)
        wub_ref[...] = wu_ref[...].astype(BF16)
        wdb_ref[...] = wd_ref[...].astype(BF16)

    x = x_ref[...]
    a = _dot(x, wgb_ref[...])
    b = _dot(x, wub_ref[...])
    o_ref[...] = _dot((a * jax.nn.sigmoid(a) * b).astype(BF16), wdb_ref[...])


def _shared_expert(hb, wg, wu, wd):
    T, Dm = hb.shape
    Ds = wg.shape[1]
    TM = _pick_tile(T, DENSE_TM)
    full = lambda r, c: pl.BlockSpec((r, c), lambda i: (0, 0))
    return pl.pallas_call(
        _shared_kernel,
        grid=(T // TM,),
        in_specs=[pl.BlockSpec((TM, Dm), lambda i: (i, 0)), full(Dm, Ds), full(Dm, Ds), full(Ds, Dm)],
        out_specs=pl.BlockSpec((TM, Dm), lambda i: (i, 0)),
        out_shape=jax.ShapeDtypeStruct((T, Dm), F32),
        scratch_shapes=[pltpu.VMEM((Dm, Ds), BF16), pltpu.VMEM((Dm, Ds), BF16), pltpu.VMEM((Ds, Dm), BF16)],
        compiler_params=_cp("arbitrary"),
        name="moe_shared",
    )(hb, wg, wu, wd)


def _moe_ln(h, hb, hp, w_router, router_bias, w_gate, w_up, w_down, layer, ws_gate, ws_up, ws_down, g, b,
            drop_front=None):
    T, Dm = h.shape
    eidx, wts, rank, cnt = _router(h, w_router, router_bias)
    counts = cnt[:, 0].astype(jnp.int32)
    pcounts = (counts + MOE_BLOCK - 1) // MOE_BLOCK * MOE_BLOCK
    pends = jnp.cumsum(pcounts)
    pstarts = pends - pcounts
    nblk = -(-T * TOP_K // MOE_BLOCK) + N_EXPERTS
    rows = nblk * MOE_BLOCK
    eid = jnp.arange(N_EXPERTS, dtype=jnp.int32)
    dest = jnp.sum(jnp.where(eidx[..., None] == eid, pstarts.astype(jnp.int32), 0), axis=-1) + rank
    TM = _pick_tile(T, 128)
    dest_tiles = dest.T.reshape(T // TM, TM * TOP_K)
    blk_start = jnp.arange(nblk, dtype=jnp.int32) * MOE_BLOCK
    blk_e = jnp.minimum(jnp.sum((pends[None, :] <= blk_start[:, None]).astype(jnp.int32), axis=1),
                        N_EXPERTS - 1)
    nact = (pends[-1:] // MOE_BLOCK).astype(jnp.int32)
    fill = jnp.concatenate([pstarts + counts, pcounts - counts]).astype(jnp.int32)
    xs = _dispatch(hp, dest_tiles, fill, rows)
    yb = _experts(xs, blk_e, nact, w_gate, w_up, w_down, layer)
    shared = _shared_expert(hb, ws_gate, ws_up, ws_down)
    return _combine_ln(yb, dest_tiles, wts.T, h, shared, g, b, drop_front)


def kernel(x, meta_tokens, ev_w_in, ev_w_out, ev_sink, od_w_in, od_conv_w, od_a_log, od_dt_bias, od_norm_w,
           od_w_out, ln_g, ln_b, w_router, router_bias, w_gate, w_up, w_down, ws_gate, ws_up, ws_down):
    B, S, Dm = x.shape
    Lp = S + FRONT
    meta = jnp.broadcast_to(meta_tokens.astype(x.dtype)[None], (B, N_META, Dm))
    h = jnp.concatenate([jnp.zeros((B, DEAD, Dm), x.dtype), meta, x], axis=1).reshape(B * Lp, Dm)
    hb = h.astype(BF16)
    for layer in range(DEPTH):
        i = layer // 2
        if layer % 2 == 0:
            mix = _even_mixer(hb, B, Lp, i, ev_w_in, ev_w_out, ev_sink[i])
        else:
            mix = _gdn_mixer(hb, B, Lp, i, od_w_in, od_conv_w[i], od_a_log[i], od_dt_bias[i],
                             od_norm_w[i], od_w_out)
        h, hb, hp = _residual_ln(h, mix, ln_g[layer, 0], ln_b[layer, 0])
        last = layer == DEPTH - 1
        h, hb = _moe_ln(h, hb, hp, w_router[layer], router_bias[layer], w_gate, w_up, w_down, layer,
                        ws_gate[layer], ws_up[layer], ws_down[layer], ln_g[layer, 1], ln_b[layer, 1],
                        drop_front=Lp // FRONT if last else None)
    return h.reshape(B, S, Dm)
```

```python
import functools
import math

import numpy as np
import jax
import jax.numpy as jnp
from jax import lax
from jax.experimental import pallas as pl
from jax.experimental.pallas import tpu as pltpu

F32 = jnp.float32
BF16 = jnp.bfloat16

D_MODEL = 2048
DEPTH = 4
N_META = 16
FRONT = 128
DEAD = FRONT - N_META
RET_HEADS = 8
RET_HD = 128
RET_W = RET_HEADS * RET_HD
RET_CHUNK = 128
ATT_HEADS = 8
ATT_KV_HEADS = 2
ATT_GROUP = ATT_HEADS // ATT_KV_HEADS
ATT_HD = 128
ATT_BLOCK = 128
WINDOW = 128
EVEN_IN = 4 * RET_W + ATT_HEADS * ATT_HD + 2 * ATT_KV_HEADS * ATT_HD
GDN_QK_HEADS = 16
GDN_V_HEADS = 32
GDN_HD = 128
GDN_K_W = GDN_QK_HEADS * GDN_HD
GDN_V_W = GDN_V_HEADS * GDN_HD
GDN_CONV_CH = 2 * GDN_K_W + GDN_V_W
GDN_CHUNK = 64
GDN_HG = 32
CONV_WIDTH = 5
N_EXPERTS = 64
TOP_K = 8
N_GROUPS = 8
GROUP_SIZE = N_EXPERTS // N_GROUPS
TOPK_GROUPS = 4
D_EXPERT = 384
ROUTE_SCALE = 2.5
MOE_BLOCK = 512
DENSE_TM = 1280
ISSUE_GROUP = 4
DN_ALPHA = (2 * DEPTH) ** 0.25
LN_EPS = 1e-5
NORM_EPS = 1e-6
NEG = -1e30

VMEM_LIMIT = 56 * 2**20


def _cp(*sem, vmem=VMEM_LIMIT):
    return pltpu.CompilerParams(dimension_semantics=sem, vmem_limit_bytes=vmem)


def _dot(a, b):
    return jnp.dot(a, b, preferred_element_type=F32)


def _dot_nt(a, b):
    return lax.dot_general(a, b, (((1,), (1,)), ((), ())), preferred_element_type=F32)


def _dot_tn(a, b):
    return lax.dot_general(a, b, (((0,), (0,)), ((), ())), preferred_element_type=F32)


def _split3(a):
    hi = a.astype(BF16)
    r1 = a - hi.astype(F32)
    mid = r1.astype(BF16)
    lo = (r1 - mid.astype(F32)).astype(BF16)
    return hi, mid, lo


def _pick_tile(n, cap, mult=8):
    for t in range(min(cap, n), 0, -1):
        if n % t == 0 and t % mult == 0:
            return t
    return n


def _mm_kernel(*refs, ksplits):
    nx = len(ksplits)
    x_refs, w_ref, o_ref, wb_ref = refs[:nx], refs[nx], refs[nx + 1], refs[nx + 2]

    @pl.when(pl.program_id(1) == 0)
    def _():
        wb_ref[...] = w_ref[...].astype(BF16)

    acc = None
    k0 = 0
    for x_ref, kk in zip(x_refs, ksplits):
        part = _dot(x_ref[...].astype(BF16), wb_ref[k0:k0 + kk, :])
        acc = part if acc is None else acc + part
        k0 += kk
    o_ref[...] = acc.astype(o_ref.dtype)


def _matmul(xs, w, idx, col0, ncols, tn, out_dtype=F32, tm_cap=DENSE_TM):
    M = xs[0].shape[0]
    ksplits = tuple(x.shape[1] for x in xs)
    K = sum(ksplits)
    assert w.shape[1] == K and ncols % tn == 0 and col0 % tn == 0
    tm = _pick_tile(M, tm_cap)
    in_specs = [pl.BlockSpec((tm, kk), lambda j, i: (i, 0)) for kk in ksplits]
    in_specs.append(pl.BlockSpec((None, K, tn), lambda j, i: (idx, 0, col0 // tn + j)))
    return pl.pallas_call(
        functools.partial(_mm_kernel, ksplits=ksplits),
        grid=(ncols // tn, M // tm),
        in_specs=in_specs,
        out_specs=pl.BlockSpec((tm, tn), lambda j, i: (i, j)),
        out_shape=jax.ShapeDtypeStruct((M, ncols), out_dtype),
        scratch_shapes=[pltpu.VMEM((K, tn), BF16)],
        compiler_params=_cp("arbitrary", "arbitrary"),
        name="matmul",
    )(*xs, w)


PACK_ROWS = D_MODEL // 256
HI_MASK = 0xFFFF0000


def _pack_rows(ref, x):
    n = x.shape[0]
    half = D_MODEL // 2
    for j in range(PACK_ROWS):
        lo = x[:, 128 * j:128 * (j + 1)].astype(BF16).astype(F32)
        hi = x[:, half + 128 * j:half + 128 * (j + 1)].astype(BF16).astype(F32)
        word = (pltpu.bitcast(lo, jnp.uint32) >> 16) | (pltpu.bitcast(hi, jnp.uint32) & jnp.uint32(HI_MASK))
        ref[pl.ds(j, n, stride=PACK_ROWS), :] = word


def _unpack_rows(ref, n):
    lo, hi = [], []
    for j in range(PACK_ROWS):
        word = ref[pl.ds(j, n, stride=PACK_ROWS), :]
        lo.append(pltpu.bitcast(word << 16, F32))
        hi.append(pltpu.bitcast(word & jnp.uint32(HI_MASK), F32))
    return lo + hi


def _layer_norm_rows(y, g, b):
    mu = jnp.mean(y, axis=-1, keepdims=True)
    d = y - mu
    var = jnp.mean(d * d, axis=-1, keepdims=True)
    return d * lax.rsqrt(var + LN_EPS) * g + b


def _ln_kernel(h_ref, a_ref, g_ref, b_ref, o_ref, ob_ref, op_ref):
    out = _layer_norm_rows(DN_ALPHA * h_ref[...] + a_ref[...], g_ref[...], b_ref[...])
    o_ref[...] = out
    ob_ref[...] = out.astype(BF16)
    _pack_rows(op_ref, out)


def _residual_ln(h, add, g, b):
    M, Dm = h.shape
    tm = _pick_tile(M, 256)
    row = pl.BlockSpec((tm, Dm), lambda i: (i, 0))
    vec = pl.BlockSpec((1, Dm), lambda i: (0, 0))
    return pl.pallas_call(
        _ln_kernel,
        grid=(M // tm,),
        in_specs=[row, row, vec, vec],
        out_specs=[row, row, pl.BlockSpec((tm * PACK_ROWS, 128), lambda i: (i, 0))],
        out_shape=[jax.ShapeDtypeStruct((M, Dm), F32), jax.ShapeDtypeStruct((M, Dm), BF16),
                   jax.ShapeDtypeStruct((M * PACK_ROWS, 128), jnp.uint32)],
        compiler_params=_cp("arbitrary"),
        name="residual_ln",
    )(h, add, g.reshape(1, Dm), b.reshape(1, Dm))


def _ret_tables(C):
    hh = np.arange(RET_HEADS, dtype=np.float64)
    lg = np.log(1.0 - 2.0 ** (-5.0 - hh))[:, None]
    pos = np.arange(C, dtype=np.float64)[None, :]
    vecs = np.stack([np.exp(lg * (pos + 1.0)),
                     np.exp(lg * (C - pos)),
                     np.exp(lg * (C - 1.0 - pos)),
                     np.exp(lg * pos),
                     np.exp(lg * C) * np.ones_like(pos)], axis=1)
    tab = np.broadcast_to(vecs[..., None], (RET_HEADS, 5, C, RET_HD))
    rel = np.abs(pos.T - pos)
    dsym = np.exp(lg[:, :, None] * rel[None])
    return jnp.asarray(tab, F32), jnp.asarray(dsym, F32)


def _ret_kernel(q_ref, k_ref, v_ref, g_ref, tab_ref, d_ref, o_ref, fst_ref, run_ref, *, nc, C):
    s = pl.program_id(1)
    fwd = s < nc
    c = jnp.where(fwd, s, 2 * nc - 1 - s)
    row = c * C + lax.broadcasted_iota(jnp.int32, (C, 1), 0)
    live = row >= DEAD
    heads = range(RET_HEADS)
    HD = RET_HD
    ks = [jnp.where(live, k_ref[0, :, h * HD:(h + 1) * HD] * (RET_HD ** -0.5), 0.0) for h in heads]
    vbs = [jnp.where(live, v_ref[0, :, h * HD:(h + 1) * HD], 0.0).astype(BF16) for h in heads]

    @pl.when(jnp.logical_or(s == 0, s == nc))
    def _():
        run_ref[...] = jnp.zeros_like(run_ref)

    @pl.when(fwd)
    def _():
        for h in heads:
            fst_ref[c, h] = run_ref[h].astype(BF16)
        upd = [_dot_tn((ks[h] * tab_ref[h, 2]).astype(BF16), vbs[h]) for h in heads]
        for h in heads:
            run_ref[h] = run_ref[h] * tab_ref[h, 4] + upd[h]

    @pl.when(jnp.logical_not(fwd))
    def _():
        qs = [q_ref[0, :, h * HD:(h + 1) * HD] for h in heads]
        sc = [(_dot_nt(qs[h].astype(BF16), ks[h].astype(BF16)) * d_ref[h]).astype(BF16) for h in heads]
        left = [_dot((qs[h] * tab_ref[h, 0]).astype(BF16), fst_ref[c, h]) for h in heads]
        right = [_dot((qs[h] * tab_ref[h, 1]).astype(BF16), run_ref[h].astype(BF16)) for h in heads]
        intra = [_dot(sc[h], vbs[h]) for h in heads]
        upd = [_dot_tn((ks[h] * tab_ref[h, 3]).astype(BF16), vbs[h]) for h in heads]
        for h in heads:
            run_ref[h] = run_ref[h] * tab_ref[h, 4] + upd[h]
            out = intra[h] + left[h] + right[h]
            mu = jnp.mean(out, axis=-1, keepdims=True)
            dlt = out - mu
            var = jnp.mean(dlt * dlt, axis=-1, keepdims=True)
            normed = dlt * lax.rsqrt(var + NORM_EPS)
            g = g_ref[0, :, h * HD:(h + 1) * HD]
            o_ref[0, :, h * HD:(h + 1) * HD] = (g * jax.nn.sigmoid(g) * normed).astype(o_ref.dtype)


def _retention(proj):
    B, Lp, _ = proj.shape
    C = RET_CHUNK
    nc = Lp // C
    tab, dsym = _ret_tables(C)

    def cidx(s):
        return jnp.where(s < nc, s, 2 * nc - 1 - s)

    def cidx_out(s):
        return jnp.where(s < nc, nc - 1, 2 * nc - 1 - s)

    return pl.pallas_call(
        functools.partial(_ret_kernel, nc=nc, C=C),
        grid=(B, 2 * nc),
        in_specs=[
            pl.BlockSpec((1, C, RET_W), lambda b, s: (b, cidx_out(s), 0)),
            pl.BlockSpec((1, C, RET_W), lambda b, s: (b, cidx(s), 1)),
            pl.BlockSpec((1, C, RET_W), lambda b, s: (b, cidx(s), 2)),
            pl.BlockSpec((1, C, RET_W), lambda b, s: (b, cidx_out(s), 3)),
            pl.BlockSpec((RET_HEADS, 5, C, RET_HD), lambda b, s: (0, 0, 0, 0)),
            pl.BlockSpec((RET_HEADS, C, C), lambda b, s: (0, 0, 0)),
        ],
        out_specs=pl.BlockSpec((1, C, RET_W), lambda b, s: (b, cidx_out(s), 0)),
        out_shape=jax.ShapeDtypeStruct((B, Lp, RET_W), BF16),
        scratch_shapes=[pltpu.VMEM((nc, RET_HEADS, RET_HD, RET_HD), BF16),
                        pltpu.VMEM((RET_HEADS, RET_HD, RET_HD), F32)],
        compiler_params=_cp("arbitrary", "arbitrary"),
        name="retention",
    )(proj, proj, proj, proj, tab, dsym)


def _att_kernel(sink_ref, slope_ref, q_ref, km_ref, kp_ref, kc_ref, kn_ref,
                vm_ref, vp_ref, vc_ref, vn_ref, o_ref, *, nb):
    qb = pl.program_id(1)
    T = ATT_BLOCK
    HD = ATT_HD
    ii = lax.broadcasted_iota(jnp.int32, (T, T), 0)
    jj = lax.broadcasted_iota(jnp.int32, (T, T), 1)
    meta_ok = jj >= DEAD
    pieces = []
    for off, k_ref, v_ref, ok in ((-T, kp_ref, vp_ref, qb >= 2),
                                  (0, kc_ref, vc_ref, qb >= 1),
                                  (T, kn_ref, vn_ref, qb + 1 <= nb)):
        dist = jnp.abs(jj + off - ii)
        pieces.append((dist.astype(F32), dist <= WINDOW, k_ref, v_ref, ok))
    kv_heads = range(ATT_KV_HEADS)
    keys = [[km_ref[0, :, j * HD:(j + 1) * HD].astype(BF16)]
            + [p[2][0, :, j * HD:(j + 1) * HD].astype(BF16) for p in pieces] for j in kv_heads]
    vals = [[vm_ref[0, :, j * HD:(j + 1) * HD].astype(BF16)]
            + [p[3][0, :, j * HD:(j + 1) * HD].astype(BF16) for p in pieces] for j in kv_heads]
    groups = range(ATT_HEADS)
    qs = [(q_ref[0, :, g * HD:(g + 1) * HD] * (ATT_HD ** -0.5)).astype(BF16) for g in groups]
    raw = [[_dot_nt(qs[g], kk) for kk in keys[g // ATT_GROUP]] for g in groups]
    es, dens = [], []
    for g in groups:
        slope = slope_ref[g // ATT_GROUP, g % ATT_GROUP]
        sink = sink_ref[g // ATT_GROUP, g % ATT_GROUP]
        s_list = [jnp.where(meta_ok, raw[g][0], NEG)]
        for (dist, inwin, _, _, ok), r in zip(pieces, raw[g][1:]):
            s_list.append(jnp.where(ok, jnp.where(inwin, r - slope * dist, NEG), NEG))
        m = jnp.full((T, 1), sink, F32)
        for sb in s_list:
            m = jnp.maximum(m, jnp.max(sb, axis=-1, keepdims=True))
        e_list = [jnp.exp(sb - m) for sb in s_list]
        den = jnp.exp(sink - m)
        for e in e_list:
            den = den + jnp.sum(e, axis=-1, keepdims=True)
        es.append([e.astype(BF16) for e in e_list])
        dens.append(den)
    pv = [[_dot(e, vb) for e, vb in zip(es[g], vals[g // ATT_GROUP])] for g in groups]
    for g in groups:
        acc = pv[g][0] + pv[g][1] + pv[g][2] + pv[g][3]
        o_ref[0, :, g * HD:(g + 1) * HD] = (acc / dens[g]).astype(o_ref.dtype)


def _window_attention(proj, sink):
    B, Lp, _ = proj.shape
    nb = Lp // ATT_BLOCK - 1
    qw = ATT_HEADS * ATT_HD
    kw = ATT_KV_HEADS * ATT_HD
    q0 = 4 * RET_W // qw
    k0 = (4 * RET_W + qw) // kw
    v0 = k0 + 1
    slopes = np.asarray(2.0 ** (-8.0 * (np.arange(ATT_HEADS) + 1.0) / ATT_HEADS), np.float32)
    smem = pl.BlockSpec(memory_space=pltpu.SMEM)

    def kv(c0, f):
        return pl.BlockSpec((1, ATT_BLOCK, kw), lambda b, t: (b, f(t), c0))

    rows = [lambda t: 0, lambda t: jnp.maximum(t - 1, 0), lambda t: t, lambda t: jnp.minimum(t + 1, nb)]
    return pl.pallas_call(
        functools.partial(_att_kernel, nb=nb),
        grid=(B, nb + 1),
        in_specs=[smem, smem, pl.BlockSpec((1, ATT_BLOCK, qw), lambda b, t: (b, t, q0))]
        + [kv(k0, f) for f in rows] + [kv(v0, f) for f in rows],
        out_specs=pl.BlockSpec((1, ATT_BLOCK, qw), lambda b, t: (b, t, 0)),
        out_shape=jax.ShapeDtypeStruct((B, Lp, qw), BF16),
        compiler_params=_cp("arbitrary", "arbitrary"),
        name="window_attention",
    )(sink.astype(F32).reshape(ATT_KV_HEADS, ATT_GROUP), jnp.asarray(slopes).reshape(ATT_KV_HEADS, ATT_GROUP),
      *([proj] * 9))


CONV_HALO = 16


def _conv_shift_matrix(TT):
    half = CONV_WIDTH // 2
    t = np.arange(TT)[:, None]
    s = np.arange(TT + 2 * CONV_HALO)[None, :]
    blocks = [(s == t + CONV_HALO + tap - half) for tap in range(CONV_WIDTH) if tap != half]
    return jnp.asarray(np.concatenate(blocks, axis=0), BF16)


def _conv_kernel(xp_ref, x_ref, xn_ref, w_ref, sh_ref, o_ref, *, mode, TT, nt):
    t = pl.program_id(1)
    H = CONV_HALO
    rows = t * TT + lax.broadcasted_iota(jnp.int32, (TT, 1), 0)
    live = rows >= DEAD
    x = jnp.where(live, x_ref[0].astype(F32), 0.0)
    prow = t * TT - H + lax.broadcasted_iota(jnp.int32, (H, 1), 0)
    prev = jnp.where(prow >= DEAD, xp_ref[0].astype(F32), 0.0)
    nrow = (t + 1) * TT + lax.broadcasted_iota(jnp.int32, (H, 1), 0)
    nxt = jnp.where(jnp.logical_and(nrow >= DEAD, t < nt - 1), xn_ref[0].astype(F32), 0.0)
    staged = jnp.concatenate([prev.astype(BF16), x.astype(BF16), nxt.astype(BF16)], axis=0)
    shifted = _dot(sh_ref[...], staged)
    w = w_ref[...]
    half = CONV_WIDTH // 2
    acc = x * w[half:half + 1, :]
    blk = 0
    for tap in range(CONV_WIDTH):
        if tap == half:
            continue
        acc = acc + shifted[blk * TT:(blk + 1) * TT, :] * w[tap:tap + 1, :]
        blk += 1
    y = acc * jax.nn.sigmoid(acc)
    if mode in ("q", "k"):
        scale = GDN_HD ** -0.5 if mode == "q" else 1.0
        for a in range(y.shape[1] // GDN_HD):
            ya = y[:, a * GDN_HD:(a + 1) * GDN_HD]
            ya = ya * lax.rsqrt(jnp.sum(ya * ya, axis=-1, keepdims=True) + NORM_EPS)
            if mode == "q":
                ya = ya * scale
            o_ref[0, :, a * GDN_HD:(a + 1) * GDN_HD] = jnp.where(live, ya, 0.0).astype(o_ref.dtype)
    else:
        o_ref[0] = jnp.where(live, y, 0.0).astype(o_ref.dtype)


def _short_conv(qkv, conv_w, mode, col0, ncols):
    B, Lp, _ = qkv.shape
    H = CONV_HALO
    TT = _pick_tile(Lp, 256, mult=H)
    TC = 1024
    nt = Lp // TT
    c0 = col0 // TC
    shift = _conv_shift_matrix(TT)
    return pl.pallas_call(
        functools.partial(_conv_kernel, mode=mode, TT=TT, nt=nt),
        grid=(B, nt, ncols // TC),
        in_specs=[
            pl.BlockSpec((1, H, TC), lambda b, t, c: (b, jnp.maximum(t * (TT // H) - 1, 0), c0 + c)),
            pl.BlockSpec((1, TT, TC), lambda b, t, c: (b, t, c0 + c)),
            pl.BlockSpec((1, H, TC), lambda b, t, c: (b, jnp.minimum((t + 1) * (TT // H), Lp // H - 1), c0 + c)),
            pl.BlockSpec((CONV_WIDTH, TC), lambda b, t, c: (0, c0 + c)),
            pl.BlockSpec(shift.shape, lambda b, t, c: (0, 0)),
        ],
        out_specs=pl.BlockSpec((1, TT, TC), lambda b, t, c: (b, t, c)),
        out_shape=jax.ShapeDtypeStruct((B, Lp, ncols), BF16),
        compiler_params=_cp("arbitrary", "arbitrary", "arbitrary"),
        name="short_conv_" + mode,
    )(qkv, qkv, qkv, conv_w, shift)


def _gate_kernel(ba_ref, alog_ref, dtb_ref, o_ref, *, TT):
    t = pl.program_id(1)
    rows = t * TT + lax.broadcasted_iota(jnp.int32, (TT, 1), 0)
    lane = lax.broadcasted_iota(jnp.int32, (1, 4 * GDN_V_HEADS), 1)
    is_g = (lane // GDN_V_HEADS) % 2 == 1
    x = ba_ref[0]
    beta = jax.nn.sigmoid(x)
    xs = x + dtb_ref[...]
    softplus = jnp.maximum(xs, 0.0) + jnp.log(1.0 + jnp.exp(-jnp.abs(xs)))
    g = -jnp.exp(alog_ref[...]) * softplus
    o_ref[0] = jnp.where(rows >= DEAD, jnp.where(is_g, g, beta), 0.0)


def _gates(ba, a_log, dt_bias):
    B, Lp, W = ba.shape
    TT = _pick_tile(Lp, 1024)
    zeros = jnp.zeros((2, 1, GDN_V_HEADS), F32)
    alog = jnp.concatenate([zeros, a_log.astype(F32)[:, None, :]], axis=1).reshape(1, W)
    dtb = jnp.concatenate([zeros, dt_bias.astype(F32)[:, None, :]], axis=1).reshape(1, W)
    vec = pl.BlockSpec((1, W), lambda b, t: (0, 0))
    return pl.pallas_call(
        functools.partial(_gate_kernel, TT=TT),
        grid=(B, Lp // TT),
        in_specs=[pl.BlockSpec((1, TT, W), lambda b, t: (b, t, 0)), vec, vec],
        out_specs=pl.BlockSpec((1, TT, W), lambda b, t: (b, t, 0)),
        out_shape=jax.ShapeDtypeStruct((B, Lp, W), F32),
        compiler_params=_cp("arbitrary", "arbitrary"),
        name="gdn_gates",
    )(ba, alog, dtb)


def _gdn_kernel(*refs, rev, final):
    if final:
        q_ref, k_ref, v_ref, gc_ref, gr_ref, of_ref, z_ref, nw_ref, o_ref, s_ref = refs
    else:
        q_ref, k_ref, v_ref, gc_ref, gr_ref, o_ref, s_ref = refs
    C = GDN_CHUNK
    HG = GDN_HG

    @pl.when(pl.program_id(2) == 0)
    def _():
        s_ref[...] = jnp.zeros_like(s_ref)

    ii = lax.broadcasted_iota(jnp.int32, (C, C), 0)
    jj = lax.broadcasted_iota(jnp.int32, (C, C), 1)
    incl = (jj >= ii) if rev else (jj <= ii)
    strict = (jj > ii) if rev else (jj < ii)
    tri_col = jnp.where(incl, 1.0, 0.0).astype(BF16)
    tri_row = jnp.where((ii >= jj) if rev else (ii <= jj), 1.0, 0.0).astype(BF16)
    gcol = gc_ref[0, 0]
    grow = gr_ref[0, 0, 0]
    gc_col = sum(_dot(tri_col, p) for p in _split3(gcol))
    gc_row = sum(_dot(p, tri_row) for p in _split3(grow))
    last = 0 if rev else C - 1
    d0 = 2 * HG if rev else 0
    HD = GDN_HD
    heads = range(HG)
    pairs = range(HG // 2)
    qs = [q_ref[0, :, p * HD:(p + 1) * HD] for p in pairs]
    ks = [k_ref[0, :, p * HD:(p + 1) * HD] for p in pairs]
    kfs = [k.astype(F32) for k in ks]
    kq = [_dot_nt(jnp.concatenate([ks[p], qs[p]], axis=0), ks[p]) for p in pairs]
    gram = [r[:C] for r in kq]
    qk = [r[C:] for r in kq]
    beta = [gcol[:, d0 + h:d0 + h + 1] for h in heads]
    gcc = [gc_col[:, d0 + HG + h:d0 + HG + h + 1] for h in heads]
    gcr = [gc_row[d0 + HG + h:d0 + HG + h + 1, :] for h in heads]
    gtot = [g[last:last + 1, :] for g in gcc]
    decay = [jnp.where(incl, jnp.exp(jnp.where(incl, gcc[h] - gcr[h], 0.0)), 0.0) for h in heads]
    eg = [jnp.exp(g) for g in gcc]
    ms = [-jnp.where(strict, beta[h] * gram[h // 2] * decay[h], 0.0) for h in heads]
    pbs = [m.astype(BF16) for m in ms]
    ps = [_dot(pb, pb) for pb in pbs]
    pbs = [p.astype(BF16) for p in ps]
    for _ in range(int(math.log2(C)) - 2):
        res = [_dot(jnp.concatenate([ms[h].astype(BF16), pbs[h]], axis=0), pbs[h]) for h in heads]
        ms = [ms[h] + ps[h] + res[h][:C] for h in heads]
        ps = [r[C:] for r in res]
        pbs = [p.astype(BF16) for p in ps]
    ms = [ms[h] + ps[h] + _dot(ms[h].astype(BF16), pbs[h]) for h in heads]
    rhs = [jnp.concatenate([v_ref[0, :, h * HD:(h + 1) * HD].astype(F32) * beta[h],
                            kfs[h // 2] * (beta[h] * eg[h])], axis=1) for h in heads]
    sol = [rhs[h] + _dot(ms[h].astype(BF16), rhs[h].astype(BF16)) for h in heads]
    st = [s_ref[h] for h in heads]
    stb = [s.astype(BF16) for s in st]
    lhs = [jnp.concatenate([sol[h][:, HD:].astype(BF16), (qs[h // 2].astype(F32) * eg[h]).astype(BF16)], axis=0)
           for h in heads]
    ws = [_dot(lhs[h], stb[h]) for h in heads]
    vnb = [(sol[h][:, :HD] - ws[h][:C]).astype(BF16) for h in heads]
    outs = [ws[h][C:] + _dot((qk[h // 2] * decay[h]).astype(BF16), vnb[h]) for h in heads]
    upd = [_dot_tn((kfs[h // 2] * jnp.exp(gtot[h] - gcc[h])).astype(BF16), vnb[h]) for h in heads]
    for h in heads:
        s_ref[h] = st[h] * jnp.exp(gtot[h]) + upd[h]
        sl = slice(h * HD, (h + 1) * HD)
        if final:
            o = outs[h] + of_ref[0, :, sl]
            z = z_ref[0, :, sl]
            o = o * lax.rsqrt(jnp.mean(o * o, axis=-1, keepdims=True) + NORM_EPS) * nw_ref[...]
            o_ref[0, :, sl] = (o * (z * jax.nn.sigmoid(z))).astype(o_ref.dtype)
        else:
            o_ref[0, :, sl] = outs[h]


def _gdn_dir(q, k, v, gcol, grow, rev, o_fwd=None, z=None, norm_w=None):
    B, Lp, _ = v.shape
    C = GDN_CHUNK
    nc = Lp // C
    ng = GDN_V_HEADS // GDN_HG
    qw = GDN_HG // 2 * GDN_HD
    vw = GDN_HG * GDN_HD
    final = o_fwd is not None

    def cc(c):
        return nc - 1 - c if rev else c

    in_specs = [
        pl.BlockSpec((1, C, qw), lambda b, g, c: (b, cc(c), g)),
        pl.BlockSpec((1, C, qw), lambda b, g, c: (b, cc(c), g)),
        pl.BlockSpec((1, C, vw), lambda b, g, c: (b, cc(c), g)),
        pl.BlockSpec((1, 1, C, 4 * GDN_HG), lambda b, g, c: (b, g, cc(c), 0)),
        pl.BlockSpec((1, 1, 1, 4 * GDN_HG, C), lambda b, g, c: (b, g, cc(c), 0, 0)),
    ]
    args = [q, k, v, gcol, grow]
    if final:
        in_specs += [pl.BlockSpec((1, C, vw), lambda b, g, c: (b, cc(c), g)),
                     pl.BlockSpec((1, C, vw), lambda b, g, c: (b, cc(c), g)),
                     pl.BlockSpec((1, GDN_HD), lambda b, g, c: (0, 0))]
        args += [o_fwd, z, norm_w.astype(F32).reshape(1, GDN_HD)]
    return pl.pallas_call(
        functools.partial(_gdn_kernel, rev=rev, final=final),
        grid=(B, ng, nc),
        in_specs=in_specs,
        out_specs=pl.BlockSpec((1, C, vw), lambda b, g, c: (b, cc(c), g)),
        out_shape=jax.ShapeDtypeStruct((B, Lp, GDN_V_W), BF16 if final else F32),
        scratch_shapes=[pltpu.VMEM((GDN_HG, GDN_HD, GDN_HD), F32)],
        compiler_params=_cp("arbitrary", "arbitrary", "arbitrary"),
        name="gdn_bwd" if rev else "gdn_fwd",
    )(*args)


def _gdn_mixer(hb, B, Lp, i, w_in, conv_w, a_log, dt_bias, norm_w, w_out):
    qkv = _matmul([hb], w_in, i, 0, GDN_CONV_CH, 1024, out_dtype=BF16).reshape(B, Lp, GDN_CONV_CH)
    z = _matmul([hb], w_in, i, GDN_CONV_CH, GDN_V_W, 1024).reshape(B, Lp, GDN_V_W)
    ba = _matmul([hb], w_in, i, GDN_CONV_CH + GDN_V_W, 4 * GDN_V_HEADS, 128).reshape(B, Lp, 4 * GDN_V_HEADS)
    q = _short_conv(qkv, conv_w, "q", 0, GDN_K_W)
    k = _short_conv(qkv, conv_w, "k", GDN_K_W, GDN_K_W)
    v = _short_conv(qkv, conv_w, "v", 2 * GDN_K_W, GDN_V_W)
    gb = _gates(ba, a_log, dt_bias)
    ng = GDN_V_HEADS // GDN_HG
    nc = Lp // GDN_CHUNK
    gcol = gb.reshape(B, Lp, 4, ng, GDN_HG).transpose(0, 3, 1, 2, 4).reshape(B, ng, Lp, 4 * GDN_HG)
    grow = gcol.reshape(B, ng, nc, GDN_CHUNK, 4 * GDN_HG).transpose(0, 1, 2, 4, 3)
    o_f = _gdn_dir(q, k, v, gcol, grow, rev=False)
    o = _gdn_dir(q, k, v, gcol, grow, rev=True, o_fwd=o_f, z=z, norm_w=norm_w)
    return _matmul([o.reshape(B * Lp, GDN_V_W)], w_out, i, 0, D_MODEL, 512)


def _even_mixer(hb, B, Lp, i, w_in, w_out, sink):
    proj = _matmul([hb], w_in, i, 0, EVEN_IN, 512).reshape(B, Lp, EVEN_IN)
    ret = _retention(proj).reshape(B * Lp, RET_W)
    att = _window_attention(proj, sink).reshape(B * Lp, ATT_HEADS * ATT_HD)
    return _matmul([ret, att], w_out, i, 0, D_MODEL, 1024)


def _router_kernel(x_ref, wt_ref, bias_ref, eidx_ref, wts_ref, rank_ref, cnt_ref, carry_ref, *, TM):
    E = N_EXPERTS

    @pl.when(pl.program_id(0) == 0)
    def _():
        carry_ref[...] = jnp.zeros_like(carry_ref)

    xh, xm, _ = _split3(x_ref[...])
    wh, wm, _ = _split3(wt_ref[...])
    lead = _dot_nt(jnp.concatenate([wh, wm], axis=0), xh)
    logits = lead[:E] + (lead[E:] + _dot_nt(wh, xm))
    scores = jax.nn.sigmoid(logits)
    choice = scores + bias_ref[...]
    ninf = -jnp.inf
    io8 = lax.broadcasted_iota(jnp.int32, (GROUP_SIZE, TM), 0)
    gs_rows = []
    for g in range(N_GROUPS):
        cg = choice[g * GROUP_SIZE:(g + 1) * GROUP_SIZE, :]
        m1 = jnp.max(cg, axis=0, keepdims=True)
        i1 = jnp.min(jnp.where(cg == m1, io8, GROUP_SIZE), axis=0, keepdims=True)
        m2 = jnp.max(jnp.where(io8 == i1, ninf, cg), axis=0, keepdims=True)
        gs_rows.append(m1 + m2)
    gs = jnp.concatenate(gs_rows, axis=0)
    gsel = jnp.zeros((N_GROUPS, TM), jnp.int32)
    for _ in range(TOPK_GROUPS):
        m = jnp.max(gs, axis=0, keepdims=True)
        idx = jnp.min(jnp.where(gs == m, io8, N_GROUPS), axis=0, keepdims=True)
        hit = io8 == idx
        gsel = jnp.where(hit, 1, gsel)
        gs = jnp.where(hit, ninf, gs)
    masked = jnp.concatenate(
        [jnp.where(gsel[g:g + 1, :] > 0, choice[g * GROUP_SIZE:(g + 1) * GROUP_SIZE, :], ninf)
         for g in range(N_GROUPS)], axis=0)
    ioe = lax.broadcasted_iota(jnp.int32, (E, TM), 0)
    sel = jnp.zeros((E, TM), F32)
    idx_rows, w_rows = [], []
    for _ in range(TOP_K):
        m = jnp.max(masked, axis=0, keepdims=True)
        idx = jnp.min(jnp.where(masked == m, ioe, E), axis=0, keepdims=True)
        hit = ioe == idx
        idx_rows.append(idx)
        w_rows.append(jnp.sum(jnp.where(hit, scores, 0.0), axis=0, keepdims=True))
        sel = jnp.where(hit, 1.0, sel)
        masked = jnp.where(hit, ninf, masked)
    wsum = w_rows[0]
    for w in w_rows[1:]:
        wsum = wsum + w
    ti = lax.broadcasted_iota(jnp.int32, (TM, TM), 0)
    tj = lax.broadcasted_iota(jnp.int32, (TM, TM), 1)
    before = jnp.where(ti < tj, 1.0, 0.0).astype(BF16)
    rank = _dot(sel.astype(BF16), before) + carry_ref[:, 0:1]
    rank_rows = [jnp.sum(jnp.where(ioe == idx, rank, 0.0), axis=0, keepdims=True) for idx in idx_rows]
    eidx_ref[...] = jnp.concatenate(idx_rows, axis=0)
    wts_ref[...] = jnp.concatenate([w / wsum * ROUTE_SCALE for w in w_rows], axis=0)
    rank_ref[...] = jnp.concatenate(rank_rows, axis=0).astype(jnp.int32)
    carry_ref[...] = carry_ref[...] + jnp.sum(sel, axis=1, keepdims=True)
    cnt_ref[...] = carry_ref[...]


def _router(h, w_router, router_bias):
    T, Dm = h.shape
    TM = _pick_tile(T, 256)
    tok = pl.BlockSpec((TOP_K, TM), lambda i: (0, i))
    return pl.pallas_call(
        functools.partial(_router_kernel, TM=TM),
        grid=(T // TM,),
        in_specs=[pl.BlockSpec((TM, Dm), lambda i: (i, 0)),
                  pl.BlockSpec((N_EXPERTS, Dm), lambda i: (0, 0)),
                  pl.BlockSpec((N_EXPERTS, 1), lambda i: (0, 0))],
        out_specs=[tok, tok, tok, pl.BlockSpec((N_EXPERTS, 128), lambda i: (0, 0))],
        out_shape=[jax.ShapeDtypeStruct((TOP_K, T), jnp.int32), jax.ShapeDtypeStruct((TOP_K, T), F32),
                   jax.ShapeDtypeStruct((TOP_K, T), jnp.int32), jax.ShapeDtypeStruct((N_EXPERTS, 128), F32)],
        scratch_shapes=[pltpu.VMEM((N_EXPERTS, 128), F32)],
        compiler_params=_cp("arbitrary"),
        name="moe_router",
    )(h, w_router.T, router_bias.astype(F32).reshape(N_EXPERTS, 1))


def _dispatch_kernel(dest_hbm, fill_hbm, hp_hbm, xs_hbm, dest_smem, fill_smem, zrow_ref, xbuf_ref,
                     sem, tsem, isem, fsem, zsem, *, TM, nt, nfill):
    i = pl.program_id(0)
    cp = pltpu.make_async_copy(dest_hbm.at[i], dest_smem, isem)
    cp.start()

    R = PACK_ROWS

    def slot_rows(d):
        return xs_hbm.at[pl.ds(pl.multiple_of(d * R, R), R)]

    @pl.when(i == 0)
    def _():
        zrow_ref[...] = jnp.zeros_like(zrow_ref)
        fc = pltpu.make_async_copy(fill_hbm, fill_smem, fsem)
        fc.start()
        fc.wait()

        def per_expert(e, carry):
            first = fill_smem[e]

            def body(r, c):
                pltpu.make_async_copy(zrow_ref, slot_rows(first + r), zsem).start()
                return c

            return lax.fori_loop(0, fill_smem[N_EXPERTS + e], body, carry)

        lax.fori_loop(0, N_EXPERTS, per_expert, 0)

        def per_expert_wait(e, carry):
            def wbody(r, c):
                pltpu.make_async_copy(zrow_ref, slot_rows(0), zsem).wait()
                return c

            return lax.fori_loop(0, fill_smem[N_EXPERTS + e], wbody, carry)

        lax.fori_loop(0, N_EXPERTS, per_expert_wait, 0)

    def tile_copy(tile, b):
        rows = pl.ds(pl.multiple_of(tile * (TM * R), TM * R), TM * R)
        return pltpu.make_async_copy(hp_hbm.at[rows], xbuf_ref.at[b], tsem.at[b])

    @pl.when(i == 0)
    def _():
        tile_copy(0, 0).start()
        if nt > 1:
            tile_copy(1, 1).start()

    cp.wait()
    cur = i % 3
    tile_copy(i, cur).wait()

    def issue(grp, carry):
        t0 = grp * ISSUE_GROUP
        base = t0 * TOP_K
        slots = [[dest_smem[base + (u * TOP_K + k)] for k in range(TOP_K)] for u in range(ISSUE_GROUP)]
        for u in range(ISSUE_GROUP):
            src = xbuf_ref.at[cur, pl.ds(pl.multiple_of((t0 + u) * R, R), R)]
            for k in range(TOP_K):
                pltpu.make_async_copy(src, slot_rows(slots[u][k]), sem.at[cur]).start(priority=k % 2)
        return carry

    lax.fori_loop(0, TM // ISSUE_GROUP, issue, 0)

    def retire(b):
        for k in range(TOP_K):
            pltpu.make_async_copy(xbuf_ref.at[b], xs_hbm.at[pl.ds(0, TM * R)], sem.at[b]).wait()

    @pl.when(i > 0)
    def _():
        retire((i + 2) % 3)

    @pl.when(i + 2 < nt)
    def _():
        tile_copy(i + 2, (i + 2) % 3).start()

    @pl.when(i == nt - 1)
    def _():
        retire(cur)


def _dispatch(hp, dest_tiles, fill, rows):
    nt = dest_tiles.shape[0]
    TM = dest_tiles.shape[1] // TOP_K
    nfill = fill.shape[0]
    return pl.pallas_call(
        functools.partial(_dispatch_kernel, TM=TM, nt=nt, nfill=nfill),
        grid=(nt,),
        in_specs=[pl.BlockSpec(memory_space=pl.ANY), pl.BlockSpec(memory_space=pl.ANY),
                  pl.BlockSpec(memory_space=pl.ANY)],
        out_specs=pl.BlockSpec(memory_space=pl.ANY),
        out_shape=jax.ShapeDtypeStruct((rows * PACK_ROWS, 128), jnp.uint32),
        scratch_shapes=[pltpu.SMEM((TOP_K * TM,), jnp.int32), pltpu.SMEM((nfill,), jnp.int32),
                        pltpu.VMEM((PACK_ROWS, 128), jnp.uint32),
                        pltpu.VMEM((3, TM * PACK_ROWS, 128), jnp.uint32),
                        pltpu.SemaphoreType.DMA((3,)), pltpu.SemaphoreType.DMA((3,)),
                        pltpu.SemaphoreType.DMA, pltpu.SemaphoreType.DMA, pltpu.SemaphoreType.DMA],
        compiler_params=_cp("arbitrary"),
        name="moe_dispatch",
    )(dest_tiles, fill, hp)


def _expert_kernel(be_ref, na_ref, x_ref, wg_ref, wu_ref, wd_ref, o_ref, wgb_ref, wub_ref, wdb_ref):
    i = pl.program_id(0)
    active = i < na_ref[0]
    new_expert = jnp.logical_or(i == 0, be_ref[i] != be_ref[jnp.maximum(i - 1, 0)])

    @pl.when(jnp.logical_and(active, new_expert))
    def _():
        wgb_ref[...] = wg_ref[0, 0].astype(BF16)
        wub_ref[...] = wu_ref[0, 0].astype(BF16)
        wdb_ref[...] = wd_ref[0, 0].astype(BF16)

    @pl.when(active)
    def _():
        x = jnp.concatenate([c.astype(BF16) for c in _unpack_rows(x_ref, MOE_BLOCK)], axis=1)
        a = _dot(x, wgb_ref[...])
        b = _dot(x, wub_ref[...])
        hmid = (a * jax.nn.sigmoid(a) * b).astype(BF16)
        _pack_rows(o_ref, _dot(hmid, wdb_ref[...]))


def _experts(xs, blk_e, nact, w_gate, w_up, w_down, layer):
    Dm = D_MODEL
    rows = xs.shape[0] // PACK_ROWS
    nblk = rows // MOE_BLOCK
    pblock = MOE_BLOCK * PACK_ROWS

    def row(i, be, na):
        return (jnp.minimum(i, na[0] - 1), 0)

    def wsel(i, be, na):
        return (layer, be[jnp.minimum(i, na[0] - 1)], 0, 0)

    return pl.pallas_call(
        _expert_kernel,
        grid_spec=pltpu.PrefetchScalarGridSpec(
            num_scalar_prefetch=2,
            grid=(nblk,),
            in_specs=[pl.BlockSpec((pblock, 128), row),
                      pl.BlockSpec((1, 1, Dm, D_EXPERT), wsel),
                      pl.BlockSpec((1, 1, Dm, D_EXPERT), wsel),
                      pl.BlockSpec((1, 1, D_EXPERT, Dm), wsel)],
            out_specs=pl.BlockSpec((pblock, 128), row),
            scratch_shapes=[pltpu.VMEM((Dm, D_EXPERT), BF16), pltpu.VMEM((Dm, D_EXPERT), BF16),
                            pltpu.VMEM((D_EXPERT, Dm), BF16)],
        ),
        out_shape=jax.ShapeDtypeStruct((rows * PACK_ROWS, 128), jnp.uint32),
        compiler_params=_cp("arbitrary"),
        name="moe_experts",
    )(blk_e, nact, xs, w_gate, w_up, w_down)


def _combine_kernel(dest_hbm, yb_hbm, w_ref, h_ref, sh_ref, g_ref, b_ref, o_ref, ob_ref,
                    dest_smem, buf_ref, routed_ref, sem, isem, *, TM, nt):
    i = pl.program_id(0)
    R = PACK_ROWS
    G = 8
    half = D_MODEL // 2

    def load_slots(tile, slot):
        cp = pltpu.make_async_copy(dest_hbm.at[tile],
                                   dest_smem.at[pl.ds(pl.multiple_of(slot * (TM * TOP_K), TM * TOP_K), TM * TOP_K)], isem)
        cp.start()
        cp.wait()

    def issue_group(slot, grp):
        t0 = grp * G
        base = slot * (TM * TOP_K) + t0 * TOP_K
        slots = [[dest_smem[base + (u * TOP_K + k)] for k in range(TOP_K)] for u in range(G)]
        for u in range(G):
            for k in range(TOP_K):
                src = yb_hbm.at[pl.ds(pl.multiple_of(slots[u][k] * R, R), R)]
                dst = buf_ref.at[slot, k, pl.ds(pl.multiple_of((t0 + u) * R, R), R)]
                pltpu.make_async_copy(src, dst, sem.at[slot]).start(priority=k % 2)

    @pl.when(i == 0)
    def _():
        load_slots(0, 0)

        def first(grp, carry):
            issue_group(0, grp)
            return carry

        lax.fori_loop(0, TM // G, first, 0)

    slot = i % 2
    nslot = (i + 1) % 2
    load_slots(jnp.minimum(i + 1, nt - 1), nslot)

    def retire(b):
        for k in range(TOP_K):
            pltpu.make_async_copy(yb_hbm.at[pl.ds(0, TM * R)], buf_ref.at[b, k], sem.at[b]).wait()

    retire(slot)

    def step(grp, carry):
        t0 = grp * G
        base = nslot * (TM * TOP_K) + t0 * TOP_K
        slots = [[dest_smem[base + (u * TOP_K + k)] for k in range(TOP_K)] for u in range(G)]
        r0 = pl.multiple_of(grp * G, G)
        w = w_ref[pl.ds(r0, G), :]
        lo = [None] * R
        hi = [None] * R
        for k in range(TOP_K):
            wk = w[:, k:k + 1]
            words = [buf_ref[slot, k, pl.ds(grp * (G * R) + j, G, stride=R), :] for j in range(R)]
            u = k
            for kk in range(TOP_K):
                src = yb_hbm.at[pl.ds(pl.multiple_of(slots[u][kk] * R, R), R)]
                dst = buf_ref.at[nslot, kk, pl.ds(pl.multiple_of((t0 + u) * R, R), R)]
                pltpu.make_async_copy(src, dst, sem.at[nslot]).start(priority=kk % 2)
            for j in range(R):
                a = pltpu.bitcast(words[j] << 16, F32) * wk
                b = pltpu.bitcast(words[j] & jnp.uint32(HI_MASK), F32) * wk
                lo[j] = a if lo[j] is None else lo[j] + a
                hi[j] = b if hi[j] is None else hi[j] + b
        for j in range(R):
            routed_ref[pl.ds(r0, G), 128 * j:128 * (j + 1)] = lo[j]
            routed_ref[pl.ds(r0, G), half + 128 * j:half + 128 * (j + 1)] = hi[j]
        return carry

    lax.fori_loop(0, TM // G, step, 0)

    @pl.when(i == nt - 1)
    def _():
        retire(nslot)

    out = _layer_norm_rows(DN_ALPHA * h_ref[...] + (routed_ref[...] + sh_ref[...]), g_ref[...], b_ref[...])
    o_ref[...] = out
    ob_ref[...] = out.astype(BF16)


def _combine_ln(yb, dest_tiles, wts_tok, h, shared, g, b, drop_front=None):
    nt = dest_tiles.shape[0]
    TM = dest_tiles.shape[1] // TOP_K
    T, Dm = h.shape
    row = pl.BlockSpec((TM, Dm), lambda i: (i, 0))
    vec = pl.BlockSpec((1, Dm), lambda i: (0, 0))
    hbm = pl.BlockSpec(memory_space=pl.ANY)
    out_row, out_rows = row, T
    if drop_front is not None:
        assert TM == FRONT
        npb = drop_front
        out_row = pl.BlockSpec((TM, Dm), lambda i: ((i // npb) * (npb - 1) + jnp.maximum(i % npb - 1, 0), 0))
        out_rows = T - (T // (npb * TM)) * FRONT
    return pl.pallas_call(
        functools.partial(_combine_kernel, TM=TM, nt=nt),
        grid=(nt,),
        in_specs=[hbm, hbm, pl.BlockSpec((TM, TOP_K), lambda i: (i, 0)), row, row, vec, vec],
        out_specs=[out_row, row],
        out_shape=[jax.ShapeDtypeStruct((out_rows, Dm), F32), jax.ShapeDtypeStruct((T, Dm), BF16)],
        scratch_shapes=[pltpu.SMEM((2 * TOP_K * TM,), jnp.int32),
                        pltpu.VMEM((2, TOP_K, TM * PACK_ROWS, 128), jnp.uint32),
                        pltpu.VMEM((TM, Dm), F32),
                        pltpu.SemaphoreType.DMA((2,)), pltpu.SemaphoreType.DMA],
        compiler_params=_cp("arbitrary"),
        name="moe_combine",
    )(dest_tiles, yb, wts_tok, h, shared, g.reshape(1, Dm), b.reshape(1, Dm))


def _shared_kernel(x_ref, wg_ref, wu_ref, wd_ref, o_ref, wgb_ref, wub_ref, wdb_ref):
    @pl.when(pl.program_id(0) == 0)
    def _():
        wgb_ref[...] = wg_ref[...].astype(BF16)
        wub_ref[...] = wu_ref[...].astype(BF16)
        wdb_ref[...] = wd_ref[...].astype(BF16)

    x = x_ref[...]
    a = _dot(x, wgb_ref[...])
    b = _dot(x, wub_ref[...])
    o_ref[...] = _dot((a * jax.nn.sigmoid(a) * b).astype(BF16), wdb_ref[...])


def _shared_expert(hb, wg, wu, wd):
    T, Dm = hb.shape
    Ds = wg.shape[1]
    TM = _pick_tile(T, DENSE_TM)
    full = lambda r, c: pl.BlockSpec((r, c), lambda i: (0, 0))
    return pl.pallas_call(
        _shared_kernel,
        grid=(T // TM,),
        in_specs=[pl.BlockSpec((TM, Dm), lambda i: (i, 0)), full(Dm, Ds), full(Dm, Ds), full(Ds, Dm)],
        out_specs=pl.BlockSpec((TM, Dm), lambda i: (i, 0)),
        out_shape=jax.ShapeDtypeStruct((T, Dm), F32),
        scratch_shapes=[pltpu.VMEM((Dm, Ds), BF16), pltpu.VMEM((Dm, Ds), BF16), pltpu.VMEM((Ds, Dm), BF16)],
        compiler_params=_cp("arbitrary"),
        name="moe_shared",
    )(hb, wg, wu, wd)


def _moe_ln(h, hb, hp, w_router, router_bias, w_gate, w_up, w_down, layer, ws_gate, ws_up, ws_down, g, b,
            drop_front=None):
    T, Dm = h.shape
    eidx, wts, rank, cnt = _router(h, w_router, router_bias)
    counts = cnt[:, 0].astype(jnp.int32)
    pcounts = (counts + MOE_BLOCK - 1) // MOE_BLOCK * MOE_BLOCK
    pends = jnp.cumsum(pcounts)
    pstarts = pends - pcounts
    nblk = -(-T * TOP_K // MOE_BLOCK) + N_EXPERTS
    rows = nblk * MOE_BLOCK
    eid = jnp.arange(N_EXPERTS, dtype=jnp.int32)
    dest = jnp.sum(jnp.where(eidx[..., None] == eid, pstarts.astype(jnp.int32), 0), axis=-1) + rank
    TM = _pick_tile(T, 128)
    dest_tiles = dest.T.reshape(T // TM, TM * TOP_K)
    blk_start = jnp.arange(nblk, dtype=jnp.int32) * MOE_BLOCK
    blk_e = jnp.minimum(jnp.sum((pends[None, :] <= blk_start[:, None]).astype(jnp.int32), axis=1),
                        N_EXPERTS - 1)
    nact = (pends[-1:] // MOE_BLOCK).astype(jnp.int32)
    fill = jnp.concatenate([pstarts + counts, pcounts - counts]).astype(jnp.int32)
    xs = _dispatch(hp, dest_tiles, fill, rows)
    yb = _experts(xs, blk_e, nact, w_gate, w_up, w_down, layer)
    shared = _shared_expert(hb, ws_gate, ws_up, ws_down)
    return _combine_ln(yb, dest_tiles, wts.T, h, shared, g, b, drop_front)


def kernel(x, meta_tokens, ev_w_in, ev_w_out, ev_sink, od_w_in, od_conv_w, od_a_log, od_dt_bias, od_norm_w,
           od_w_out, ln_g, ln_b, w_router, router_bias, w_gate, w_up, w_down, ws_gate, ws_up, ws_down):
    B, S, Dm = x.shape
    Lp = S + FRONT
    meta = jnp.broadcast_to(meta_tokens.astype(x.dtype)[None], (B, N_META, Dm))
    h = jnp.concatenate([jnp.zeros((B, DEAD, Dm), x.dtype), meta, x], axis=1).reshape(B * Lp, Dm)
    hb = h.astype(BF16)
    for layer in range(DEPTH):
        i = layer // 2
        if layer % 2 == 0:
            mix = _even_mixer(hb, B, Lp, i, ev_w_in, ev_w_out, ev_sink[i])
        else:
            mix = _gdn_mixer(hb, B, Lp, i, od_w_in, od_conv_w[i], od_a_log[i], od_dt_bias[i],
                             od_norm_w[i], od_w_out)
        h, hb, hp = _residual_ln(h, mix, ln_g[layer, 0], ln_b[layer, 0])
        last = layer == DEPTH - 1
        h, hb = _moe_ln(h, hb, hp, w_router[layer], router_bias[layer], w_gate, w_up, w_down, layer,
                        ws_gate[layer], ws_up[layer], ws_down[layer], ln_g[layer, 1], ln_b[layer, 1],
                        drop_front=Lp // FRONT if last else None)
    return h.reshape(B, S, Dm)
```

```python
import functools
import math

import numpy as np
import jax
import jax.numpy as jnp
from jax import lax
from jax.experimental import pallas as pl
from jax.experimental.pallas import tpu as pltpu

F32 = jnp.float32
BF16 = jnp.bfloat16

D_MODEL = 2048
DEPTH = 4
N_META = 16
FRONT = 128
DEAD = FRONT - N_META
RET_HEADS = 8
RET_HD = 128
RET_W = RET_HEADS * RET_HD
RET_CHUNK = 128
ATT_HEADS = 8
ATT_KV_HEADS = 2
ATT_GROUP = ATT_HEADS // ATT_KV_HEADS
ATT_HD = 128
ATT_BLOCK = 128
WINDOW = 128
EVEN_IN = 4 * RET_W + ATT_HEADS * ATT_HD + 2 * ATT_KV_HEADS * ATT_HD
GDN_QK_HEADS = 16
GDN_V_HEADS = 32
GDN_HD = 128
GDN_K_W = GDN_QK_HEADS * GDN_HD
GDN_V_W = GDN_V_HEADS * GDN_HD
GDN_CONV_CH = 2 * GDN_K_W + GDN_V_W
GDN_CHUNK = 64
GDN_HG = 32
CONV_WIDTH = 5
N_EXPERTS = 64
TOP_K = 8
N_GROUPS = 8
GROUP_SIZE = N_EXPERTS // N_GROUPS
TOPK_GROUPS = 4
D_EXPERT = 384
ROUTE_SCALE = 2.5
MOE_BLOCK = 512
DENSE_TM = 1280
ISSUE_GROUP = 4
DN_ALPHA = (2 * DEPTH) ** 0.25
LN_EPS = 1e-5
NORM_EPS = 1e-6
NEG = -1e30

VMEM_LIMIT = 56 * 2**20


def _cp(*sem, vmem=VMEM_LIMIT):
    return pltpu.CompilerParams(dimension_semantics=sem, vmem_limit_bytes=vmem)


def _dot(a, b):
    return jnp.dot(a, b, preferred_element_type=F32)


def _dot_nt(a, b):
    return lax.dot_general(a, b, (((1,), (1,)), ((), ())), preferred_element_type=F32)


def _dot_tn(a, b):
    return lax.dot_general(a, b, (((0,), (0,)), ((), ())), preferred_element_type=F32)


def _split3(a):
    hi = a.astype(BF16)
    r1 = a - hi.astype(F32)
    mid = r1.astype(BF16)
    lo = (r1 - mid.astype(F32)).astype(BF16)
    return hi, mid, lo


def _pick_tile(n, cap, mult=8):
    for t in range(min(cap, n), 0, -1):
        if n % t == 0 and t % mult == 0:
            return t
    return n


def _mm_kernel(*refs, ksplits):
    nx = len(ksplits)
    x_refs, w_ref, o_ref, wb_ref = refs[:nx], refs[nx], refs[nx + 1], refs[nx + 2]

    @pl.when(pl.program_id(1) == 0)
    def _():
        wb_ref[...] = w_ref[...].astype(BF16)

    acc = None
    k0 = 0
    for x_ref, kk in zip(x_refs, ksplits):
        part = _dot(x_ref[...].astype(BF16), wb_ref[k0:k0 + kk, :])
        acc = part if acc is None else acc + part
        k0 += kk
    o_ref[...] = acc.astype(o_ref.dtype)


def _matmul(xs, w, idx, col0, ncols, tn, out_dtype=F32, tm_cap=DENSE_TM):
    M = xs[0].shape[0]
    ksplits = tuple(x.shape[1] for x in xs)
    K = sum(ksplits)
    assert w.shape[1] == K and ncols % tn == 0 and col0 % tn == 0
    tm = _pick_tile(M, tm_cap)
    in_specs = [pl.BlockSpec((tm, kk), lambda j, i: (i, 0)) for kk in ksplits]
    in_specs.append(pl.BlockSpec((None, K, tn), lambda j, i: (idx, 0, col0 // tn + j)))
    return pl.pallas_call(
        functools.partial(_mm_kernel, ksplits=ksplits),
        grid=(ncols // tn, M // tm),
        in_specs=in_specs,
        out_specs=pl.BlockSpec((tm, tn), lambda j, i: (i, j)),
        out_shape=jax.ShapeDtypeStruct((M, ncols), out_dtype),
        scratch_shapes=[pltpu.VMEM((K, tn), BF16)],
        compiler_params=_cp("arbitrary", "arbitrary"),
        name="matmul",
    )(*xs, w)


PACK_ROWS = D_MODEL // 256
HI_MASK = 0xFFFF0000


def _pack_rows(ref, x):
    n = x.shape[0]
    half = D_MODEL // 2
    for j in range(PACK_ROWS):
        lo = x[:, 128 * j:128 * (j + 1)].astype(BF16).astype(F32)
        hi = x[:, half + 128 * j:half + 128 * (j + 1)].astype(BF16).astype(F32)
        word = (pltpu.bitcast(lo, jnp.uint32) >> 16) | (pltpu.bitcast(hi, jnp.uint32) & jnp.uint32(HI_MASK))
        ref[pl.ds(j, n, stride=PACK_ROWS), :] = word


def _unpack_rows(ref, n):
    lo, hi = [], []
    for j in range(PACK_ROWS):
        word = ref[pl.ds(j, n, stride=PACK_ROWS), :]
        lo.append(pltpu.bitcast(word << 16, F32))
        hi.append(pltpu.bitcast(word & jnp.uint32(HI_MASK), F32))
    return lo + hi


def _layer_norm_rows(y, g, b):
    mu = jnp.mean(y, axis=-1, keepdims=True)
    d = y - mu
    var = jnp.mean(d * d, axis=-1, keepdims=True)
    return d * lax.rsqrt(var + LN_EPS) * g + b


def _ln_kernel(h_ref, a_ref, g_ref, b_ref, o_ref, ob_ref, op_ref):
    out = _layer_norm_rows(DN_ALPHA * h_ref[...] + a_ref[...], g_ref[...], b_ref[...])
    o_ref[...] = out
    ob_ref[...] = out.astype(BF16)
    _pack_rows(op_ref, out)


def _residual_ln(h, add, g, b):
    M, Dm = h.shape
    tm = _pick_tile(M, 256)
    row = pl.BlockSpec((tm, Dm), lambda i: (i, 0))
    vec = pl.BlockSpec((1, Dm), lambda i: (0, 0))
    return pl.pallas_call(
        _ln_kernel,
        grid=(M // tm,),
        in_specs=[row, row, vec, vec],
        out_specs=[row, row, pl.BlockSpec((tm * PACK_ROWS, 128), lambda i: (i, 0))],
        out_shape=[jax.ShapeDtypeStruct((M, Dm), F32), jax.ShapeDtypeStruct((M, Dm), BF16),
                   jax.ShapeDtypeStruct((M * PACK_ROWS, 128), jnp.uint32)],
        compiler_params=_cp("arbitrary"),
        name="residual_ln",
    )(h, add, g.reshape(1, Dm), b.reshape(1, Dm))


def _ret_tables(C):
    hh = np.arange(RET_HEADS, dtype=np.float64)
    lg = np.log(1.0 - 2.0 ** (-5.0 - hh))[:, None]
    pos = np.arange(C, dtype=np.float64)[None, :]
    vecs = np.stack([np.exp(lg * (pos + 1.0)),
                     np.exp(lg * (C - pos)),
                     np.exp(lg * (C - 1.0 - pos)),
                     np.exp(lg * pos),
                     np.exp(lg * C) * np.ones_like(pos)], axis=1)
    tab = np.broadcast_to(vecs[..., None], (RET_HEADS, 5, C, RET_HD))
    rel = np.abs(pos.T - pos)
    dsym = np.exp(lg[:, :, None] * rel[None])
    return jnp.asarray(tab, F32), jnp.asarray(dsym, F32)


def _ret_kernel(q_ref, k_ref, v_ref, g_ref, tab_ref, d_ref, o_ref, fst_ref, run_ref, *, nc, C):
    s = pl.program_id(1)
    fwd = s < nc
    c = jnp.where(fwd, s, 2 * nc - 1 - s)
    row = c * C + lax.broadcasted_iota(jnp.int32, (C, 1), 0)
    live = row >= DEAD
    heads = range(RET_HEADS)
    HD = RET_HD
    ks = [jnp.where(live, k_ref[0, :, h * HD:(h + 1) * HD] * (RET_HD ** -0.5), 0.0) for h in heads]
    vbs = [jnp.where(live, v_ref[0, :, h * HD:(h + 1) * HD], 0.0).astype(BF16) for h in heads]

    @pl.when(jnp.logical_or(s == 0, s == nc))
    def _():
        run_ref[...] = jnp.zeros_like(run_ref)

    @pl.when(fwd)
    def _():
        for h in heads:
            fst_ref[c, h] = run_ref[h].astype(BF16)
        upd = [_dot_tn((ks[h] * tab_ref[h, 2]).astype(BF16), vbs[h]) for h in heads]
        for h in heads:
            run_ref[h] = run_ref[h] * tab_ref[h, 4] + upd[h]

    @pl.when(jnp.logical_not(fwd))
    def _():
        qs = [q_ref[0, :, h * HD:(h + 1) * HD] for h in heads]
        sc = [(_dot_nt(qs[h].astype(BF16), ks[h].astype(BF16)) * d_ref[h]).astype(BF16) for h in heads]
        left = [_dot((qs[h] * tab_ref[h, 0]).astype(BF16), fst_ref[c, h]) for h in heads]
        right = [_dot((qs[h] * tab_ref[h, 1]).astype(BF16), run_ref[h].astype(BF16)) for h in heads]
        intra = [_dot(sc[h], vbs[h]) for h in heads]
        upd = [_dot_tn((ks[h] * tab_ref[h, 3]).astype(BF16), vbs[h]) for h in heads]
        for h in heads:
            run_ref[h] = run_ref[h] * tab_ref[h, 4] + upd[h]
            out = intra[h] + left[h] + right[h]
            mu = jnp.mean(out, axis=-1, keepdims=True)
            dlt = out - mu
            var = jnp.mean(dlt * dlt, axis=-1, keepdims=True)
            normed = dlt * lax.rsqrt(var + NORM_EPS)
            g = g_ref[0, :, h * HD:(h + 1) * HD]
            o_ref[0, :, h * HD:(h + 1) * HD] = (g * jax.nn.sigmoid(g) * normed).astype(o_ref.dtype)


def _retention(proj):
    B, Lp, _ = proj.shape
    C = RET_CHUNK
    nc = Lp // C
    tab, dsym = _ret_tables(C)

    def cidx(s):
        return jnp.where(s < nc, s, 2 * nc - 1 - s)

    def cidx_out(s):
        return jnp.where(s < nc, nc - 1, 2 * nc - 1 - s)

    return pl.pallas_call(
        functools.partial(_ret_kernel, nc=nc, C=C),
        grid=(B, 2 * nc),
        in_specs=[
            pl.BlockSpec((1, C, RET_W), lambda b, s: (b, cidx_out(s), 0)),
            pl.BlockSpec((1, C, RET_W), lambda b, s: (b, cidx(s), 1)),
            pl.BlockSpec((1, C, RET_W), lambda b, s: (b, cidx(s), 2)),
            pl.BlockSpec((1, C, RET_W), lambda b, s: (b, cidx_out(s), 3)),
            pl.BlockSpec((RET_HEADS, 5, C, RET_HD), lambda b, s: (0, 0, 0, 0)),
            pl.BlockSpec((RET_HEADS, C, C), lambda b, s: (0, 0, 0)),
        ],
        out_specs=pl.BlockSpec((1, C, RET_W), lambda b, s: (b, cidx_out(s), 0)),
        out_shape=jax.ShapeDtypeStruct((B, Lp, RET_W), BF16),
        scratch_shapes=[pltpu.VMEM((nc, RET_HEADS, RET_HD, RET_HD), BF16),
                        pltpu.VMEM((RET_HEADS, RET_HD, RET_HD), F32)],
        compiler_params=_cp("arbitrary", "arbitrary"),
        name="retention",
    )(proj, proj, proj, proj, tab, dsym)


def _att_kernel(sink_ref, slope_ref, q_ref, km_ref, kp_ref, kc_ref, kn_ref,
                vm_ref, vp_ref, vc_ref, vn_ref, o_ref, *, nb):
    qb = pl.program_id(1)
    T = ATT_BLOCK
    HD = ATT_HD
    ii = lax.broadcasted_iota(jnp.int32, (T, T), 0)
    jj = lax.broadcasted_iota(jnp.int32, (T, T), 1)
    meta_ok = jj >= DEAD
    pieces = []
    for off, k_ref, v_ref, ok in ((-T, kp_ref, vp_ref, qb >= 2),
                                  (0, kc_ref, vc_ref, qb >= 1),
                                  (T, kn_ref, vn_ref, qb + 1 <= nb)):
        dist = jnp.abs(jj + off - ii)
        pieces.append((dist.astype(F32), dist <= WINDOW, k_ref, v_ref, ok))
    kv_heads = range(ATT_KV_HEADS)
    keys = [[km_ref[0, :, j * HD:(j + 1) * HD].astype(BF16)]
            + [p[2][0, :, j * HD:(j + 1) * HD].astype(BF16) for p in pieces] for j in kv_heads]
    vals = [[vm_ref[0, :, j * HD:(j + 1) * HD].astype(BF16)]
            + [p[3][0, :, j * HD:(j + 1) * HD].astype(BF16) for p in pieces] for j in kv_heads]
    groups = range(ATT_HEADS)
    qs = [(q_ref[0, :, g * HD:(g + 1) * HD] * (ATT_HD ** -0.5)).astype(BF16) for g in groups]
    raw = [[_dot_nt(qs[g], kk) for kk in keys[g // ATT_GROUP]] for g in groups]
    es, dens = [], []
    for g in groups:
        slope = slope_ref[g // ATT_GROUP, g % ATT_GROUP]
        sink = sink_ref[g // ATT_GROUP, g % ATT_GROUP]
        s_list = [jnp.where(meta_ok, raw[g][0], NEG)]
        for (dist, inwin, _, _, ok), r in zip(pieces, raw[g][1:]):
            s_list.append(jnp.where(ok, jnp.where(inwin, r - slope * dist, NEG), NEG))
        m = jnp.full((T, 1), sink, F32)
        for sb in s_list:
            m = jnp.maximum(m, jnp.max(sb, axis=-1, keepdims=True))
        e_list = [jnp.exp(sb - m) for sb in s_list]
        den = jnp.exp(sink - m)
        for e in e_list:
            den = den + jnp.sum(e, axis=-1, keepdims=True)
        es.append([e.astype(BF16) for e in e_list])
        dens.append(den)
    pv = [[_dot(e, vb) for e, vb in zip(es[g], vals[g // ATT_GROUP])] for g in groups]
    for g in groups:
        acc = pv[g][0] + pv[g][1] + pv[g][2] + pv[g][3]
        o_ref[0, :, g * HD:(g + 1) * HD] = (acc / dens[g]).astype(o_ref.dtype)


def _window_attention(proj, sink):
    B, Lp, _ = proj.shape
    nb = Lp // ATT_BLOCK - 1
    qw = ATT_HEADS * ATT_HD
    kw = ATT_KV_HEADS * ATT_HD
    q0 = 4 * RET_W // qw
    k0 = (4 * RET_W + qw) // kw
    v0 = k0 + 1
    slopes = np.asarray(2.0 ** (-8.0 * (np.arange(ATT_HEADS) + 1.0) / ATT_HEADS), np.float32)
    smem = pl.BlockSpec(memory_space=pltpu.SMEM)

    def kv(c0, f):
        return pl.BlockSpec((1, ATT_BLOCK, kw), lambda b, t: (b, f(t), c0))

    rows = [lambda t: 0, lambda t: jnp.maximum(t - 1, 0), lambda t: t, lambda t: jnp.minimum(t + 1, nb)]
    return pl.pallas_call(
        functools.partial(_att_kernel, nb=nb),
        grid=(B, nb + 1),
        in_specs=[smem, smem, pl.BlockSpec((1, ATT_BLOCK, qw), lambda b, t: (b, t, q0))]
        + [kv(k0, f) for f in rows] + [kv(v0, f) for f in rows],
        out_specs=pl.BlockSpec((1, ATT_BLOCK, qw), lambda b, t: (b, t, 0)),
        out_shape=jax.ShapeDtypeStruct((B, Lp, qw), BF16),
        compiler_params=_cp("arbitrary", "arbitrary"),
        name="window_attention",
    )(sink.astype(F32).reshape(ATT_KV_HEADS, ATT_GROUP), jnp.asarray(slopes).reshape(ATT_KV_HEADS, ATT_GROUP),
      *([proj] * 9))


CONV_HALO = 16


def _conv_shift_matrix(TT):
    half = CONV_WIDTH // 2
    t = np.arange(TT)[:, None]
    s = np.arange(TT + 2 * CONV_HALO)[None, :]
    blocks = [(s == t + CONV_HALO + tap - half) for tap in range(CONV_WIDTH) if tap != half]
    return jnp.asarray(np.concatenate(blocks, axis=0), BF16)


def _conv_kernel(xp_ref, x_ref, xn_ref, w_ref, sh_ref, o_ref, *, mode, TT, nt):
    t = pl.program_id(1)
    H = CONV_HALO
    rows = t * TT + lax.broadcasted_iota(jnp.int32, (TT, 1), 0)
    live = rows >= DEAD
    x = jnp.where(live, x_ref[0].astype(F32), 0.0)
    prow = t * TT - H + lax.broadcasted_iota(jnp.int32, (H, 1), 0)
    prev = jnp.where(prow >= DEAD, xp_ref[0].astype(F32), 0.0)
    nrow = (t + 1) * TT + lax.broadcasted_iota(jnp.int32, (H, 1), 0)
    nxt = jnp.where(jnp.logical_and(nrow >= DEAD, t < nt - 1), xn_ref[0].astype(F32), 0.0)
    staged = jnp.concatenate([prev.astype(BF16), x.astype(BF16), nxt.astype(BF16)], axis=0)
    shifted = _dot(sh_ref[...], staged)
    w = w_ref[...]
    half = CONV_WIDTH // 2
    acc = x * w[half:half + 1, :]
    blk = 0
    for tap in range(CONV_WIDTH):
        if tap == half:
            continue
        acc = acc + shifted[blk * TT:(blk + 1) * TT, :] * w[tap:tap + 1, :]
        blk += 1
    y = acc * jax.nn.sigmoid(acc)
    if mode in ("q", "k"):
        scale = GDN_HD ** -0.5 if mode == "q" else 1.0
        for a in range(y.shape[1] // GDN_HD):
            ya = y[:, a * GDN_HD:(a + 1) * GDN_HD]
            ya = ya * lax.rsqrt(jnp.sum(ya * ya, axis=-1, keepdims=True) + NORM_EPS)
            if mode == "q":
                ya = ya * scale
            o_ref[0, :, a * GDN_HD:(a + 1) * GDN_HD] = jnp.where(live, ya, 0.0).astype(o_ref.dtype)
    else:
        o_ref[0] = jnp.where(live, y, 0.0).astype(o_ref.dtype)


def _short_conv(qkv, conv_w, mode, col0, ncols):
    B, Lp, _ = qkv.shape
    H = CONV_HALO
    TT = _pick_tile(Lp, 256, mult=H)
    TC = 1024
    nt = Lp // TT
    c0 = col0 // TC
    shift = _conv_shift_matrix(TT)
    return pl.pallas_call(
        functools.partial(_conv_kernel, mode=mode, TT=TT, nt=nt),
        grid=(B, nt, ncols // TC),
        in_specs=[
            pl.BlockSpec((1, H, TC), lambda b, t, c: (b, jnp.maximum(t * (TT // H) - 1, 0), c0 + c)),
            pl.BlockSpec((1, TT, TC), lambda b, t, c: (b, t, c0 + c)),
            pl.BlockSpec((1, H, TC), lambda b, t, c: (b, jnp.minimum((t + 1) * (TT // H), Lp // H - 1), c0 + c)),
            pl.BlockSpec((CONV_WIDTH, TC), lambda b, t, c: (0, c0 + c)),
            pl.BlockSpec(shift.shape, lambda b, t, c: (0, 0)),
        ],
        out_specs=pl.BlockSpec((1, TT, TC), lambda b, t, c: (b, t, c)),
        out_shape=jax.ShapeDtypeStruct((B, Lp, ncols), BF16),
        compiler_params=_cp("arbitrary", "arbitrary", "arbitrary"),
        name="short_conv_" + mode,
    )(qkv, qkv, qkv, conv_w, shift)


def _gate_kernel(ba_ref, alog_ref, dtb_ref, o_ref, *, TT):
    t = pl.program_id(1)
    rows = t * TT + lax.broadcasted_iota(jnp.int32, (TT, 1), 0)
    lane = lax.broadcasted_iota(jnp.int32, (1, 4 * GDN_V_HEADS), 1)
    is_g = (lane // GDN_V_HEADS) % 2 == 1
    x = ba_ref[0]
    beta = jax.nn.sigmoid(x)
    xs = x + dtb_ref[...]
    softplus = jnp.maximum(xs, 0.0) + jnp.log(1.0 + jnp.exp(-jnp.abs(xs)))
    g = -jnp.exp(alog_ref[...]) * softplus
    o_ref[0] = jnp.where(rows >= DEAD, jnp.where(is_g, g, beta), 0.0)


def _gates(ba, a_log, dt_bias):
    B, Lp, W = ba.shape
    TT = _pick_tile(Lp, 1024)
    zeros = jnp.zeros((2, 1, GDN_V_HEADS), F32)
    alog = jnp.concatenate([zeros, a_log.astype(F32)[:, None, :]], axis=1).reshape(1, W)
    dtb = jnp.concatenate([zeros, dt_bias.astype(F32)[:, None, :]], axis=1).reshape(1, W)
    vec = pl.BlockSpec((1, W), lambda b, t: (0, 0))
    return pl.pallas_call(
        functools.partial(_gate_kernel, TT=TT),
        grid=(B, Lp // TT),
        in_specs=[pl.BlockSpec((1, TT, W), lambda b, t: (b, t, 0)), vec, vec],
        out_specs=pl.BlockSpec((1, TT, W), lambda b, t: (b, t, 0)),
        out_shape=jax.ShapeDtypeStruct((B, Lp, W), F32),
        compiler_params=_cp("arbitrary", "arbitrary"),
        name="gdn_gates",
    )(ba, alog, dtb)


def _gdn_kernel(*refs, rev, final):
    if final:
        q_ref, k_ref, v_ref, gc_ref, gr_ref, of_ref, z_ref, nw_ref, o_ref, s_ref = refs
    else:
        q_ref, k_ref, v_ref, gc_ref, gr_ref, o_ref, s_ref = refs
    C = GDN_CHUNK
    HG = GDN_HG

    @pl.when(pl.program_id(2) == 0)
    def _():
        s_ref[...] = jnp.zeros_like(s_ref)

    ii = lax.broadcasted_iota(jnp.int32, (C, C), 0)
    jj = lax.broadcasted_iota(jnp.int32, (C, C), 1)
    incl = (jj >= ii) if rev else (jj <= ii)
    strict = (jj > ii) if rev else (jj < ii)
    tri_col = jnp.where(incl, 1.0, 0.0).astype(BF16)
    tri_row = jnp.where((ii >= jj) if rev else (ii <= jj), 1.0, 0.0).astype(BF16)
    gcol = gc_ref[0, 0]
    grow = gr_ref[0, 0, 0]
    gc_col = sum(_dot(tri_col, p) for p in _split3(gcol))
    gc_row = sum(_dot(p, tri_row) for p in _split3(grow))
    last = 0 if rev else C - 1
    d0 = 2 * HG if rev else 0
    HD = GDN_HD
    heads = range(HG)
    pairs = range(HG // 2)
    qs = [q_ref[0, :, p * HD:(p + 1) * HD] for p in pairs]
    ks = [k_ref[0, :, p * HD:(p + 1) * HD] for p in pairs]
    kfs = [k.astype(F32) for k in ks]
    kq = [_dot_nt(jnp.concatenate([ks[p], qs[p]], axis=0), ks[p]) for p in pairs]
    gram = [r[:C] for r in kq]
    qk = [r[C:] for r in kq]
    beta = [gcol[:, d0 + h:d0 + h + 1] for h in heads]
    gcc = [gc_col[:, d0 + HG + h:d0 + HG + h + 1] for h in heads]
    gcr = [gc_row[d0 + HG + h:d0 + HG + h + 1, :] for h in heads]
    gtot = [g[last:last + 1, :] for g in gcc]
    decay = [jnp.where(incl, jnp.exp(jnp.where(incl, gcc[h] - gcr[h], 0.0)), 0.0) for h in heads]
    eg = [jnp.exp(g) for g in gcc]
    ms = [-jnp.where(strict, beta[h] * gram[h // 2] * decay[h], 0.0) for h in heads]
    pbs = [m.astype(BF16) for m in ms]
    ps = [_dot(pb, pb) for pb in pbs]
    pbs = [p.astype(BF16) for p in ps]
    for _ in range(int(math.log2(C)) - 2):
        res = [_dot(jnp.concatenate([ms[h].astype(BF16), pbs[h]], axis=0), pbs[h]) for h in heads]
        ms = [ms[h] + ps[h] + res[h][:C] for h in heads]
        ps = [r[C:] for r in res]
        pbs = [p.astype(BF16) for p in ps]
    ms = [ms[h] + ps[h] + _dot(ms[h].astype(BF16), pbs[h]) for h in heads]
    rhs = [jnp.concatenate([v_ref[0, :, h * HD:(h + 1) * HD].astype(F32) * beta[h],
                            kfs[h // 2] * (beta[h] * eg[h])], axis=1) for h in heads]
    sol = [rhs[h] + _dot(ms[h].astype(BF16), rhs[h].astype(BF16)) for h in heads]
    st = [s_ref[h] for h in heads]
    stb = [s.astype(BF16) for s in st]
    lhs = [jnp.concatenate([sol[h][:, HD:].astype(BF16), (qs[h // 2].astype(F32) * eg[h]).astype(BF16)], axis=0)
           for h in heads]
    ws = [_dot(lhs[h], stb[h]) for h in heads]
    vnb = [(sol[h][:, :HD] - ws[h][:C]).astype(BF16) for h in heads]
    outs = [ws[h][C:] + _dot((qk[h // 2] * decay[h]).astype(BF16), vnb[h]) for h in heads]
    upd = [_dot_tn((kfs[h // 2] * jnp.exp(gtot[h] - gcc[h])).astype(BF16), vnb[h]) for h in heads]
    for h in heads:
        s_ref[h] = st[h] * jnp.exp(gtot[h]) + upd[h]
        sl = slice(h * HD, (h + 1) * HD)
        if final:
            o = outs[h] + of_ref[0, :, sl]
            z = z_ref[0, :, sl]
            o = o * lax.rsqrt(jnp.mean(o * o, axis=-1, keepdims=True) + NORM_EPS) * nw_ref[...]
            o_ref[0, :, sl] = (o * (z * jax.nn.sigmoid(z))).astype(o_ref.dtype)
        else:
            o_ref[0, :, sl] = outs[h]


def _gdn_dir(q, k, v, gcol, grow, rev, o_fwd=None, z=None, norm_w=None):
    B, Lp, _ = v.shape
    C = GDN_CHUNK
    nc = Lp // C
    ng = GDN_V_HEADS // GDN_HG
    qw = GDN_HG // 2 * GDN_HD
    vw = GDN_HG * GDN_HD
    final = o_fwd is not None

    def cc(c):
        return nc - 1 - c if rev else c

    in_specs = [
        pl.BlockSpec((1, C, qw), lambda b, g, c: (b, cc(c), g)),
        pl.BlockSpec((1, C, qw), lambda b, g, c: (b, cc(c), g)),
        pl.BlockSpec((1, C, vw), lambda b, g, c: (b, cc(c), g)),
        pl.BlockSpec((1, 1, C, 4 * GDN_HG), lambda b, g, c: (b, g, cc(c), 0)),
        pl.BlockSpec((1, 1, 1, 4 * GDN_HG, C), lambda b, g, c: (b, g, cc(c), 0, 0)),
    ]
    args = [q, k, v, gcol, grow]
    if final:
        in_specs += [pl.BlockSpec((1, C, vw), lambda b, g, c: (b, cc(c), g)),
                     pl.BlockSpec((1, C, vw), lambda b, g, c: (b, cc(c), g)),
                     pl.BlockSpec((1, GDN_HD), lambda b, g, c: (0, 0))]
        args += [o_fwd, z, norm_w.astype(F32).reshape(1, GDN_HD)]
    return pl.pallas_call(
        functools.partial(_gdn_kernel, rev=rev, final=final),
        grid=(B, ng, nc),
        in_specs=in_specs,
        out_specs=pl.BlockSpec((1, C, vw), lambda b, g, c: (b, cc(c), g)),
        out_shape=jax.ShapeDtypeStruct((B, Lp, GDN_V_W), BF16 if final else F32),
        scratch_shapes=[pltpu.VMEM((GDN_HG, GDN_HD, GDN_HD), F32)],
        compiler_params=_cp("arbitrary", "arbitrary", "arbitrary"),
        name="gdn_bwd" if rev else "gdn_fwd",
    )(*args)


def _gdn_mixer(hb, B, Lp, i, w_in, conv_w, a_log, dt_bias, norm_w, w_out):
    qkv = _matmul([hb], w_in, i, 0, GDN_CONV_CH, 1024, out_dtype=BF16).reshape(B, Lp, GDN_CONV_CH)
    z = _matmul([hb], w_in, i, GDN_CONV_CH, GDN_V_W, 1024).reshape(B, Lp, GDN_V_W)
    ba = _matmul([hb], w_in, i, GDN_CONV_CH + GDN_V_W, 4 * GDN_V_HEADS, 128).reshape(B, Lp, 4 * GDN_V_HEADS)
    q = _short_conv(qkv, conv_w, "q", 0, GDN_K_W)
    k = _short_conv(qkv, conv_w, "k", GDN_K_W, GDN_K_W)
    v = _short_conv(qkv, conv_w, "v", 2 * GDN_K_W, GDN_V_W)
    gb = _gates(ba, a_log, dt_bias)
    ng = GDN_V_HEADS // GDN_HG
    nc = Lp // GDN_CHUNK
    gcol = gb.reshape(B, Lp, 4, ng, GDN_HG).transpose(0, 3, 1, 2, 4).reshape(B, ng, Lp, 4 * GDN_HG)
    grow = gcol.reshape(B, ng, nc, GDN_CHUNK, 4 * GDN_HG).transpose(0, 1, 2, 4, 3)
    o_f = _gdn_dir(q, k, v, gcol, grow, rev=False)
    o = _gdn_dir(q, k, v, gcol, grow, rev=True, o_fwd=o_f, z=z, norm_w=norm_w)
    return _matmul([o.reshape(B * Lp, GDN_V_W)], w_out, i, 0, D_MODEL, 512)


def _even_mixer(hb, B, Lp, i, w_in, w_out, sink):
    proj = _matmul([hb], w_in, i, 0, EVEN_IN, 512).reshape(B, Lp, EVEN_IN)
    ret = _retention(proj).reshape(B * Lp, RET_W)
    att = _window_attention(proj, sink).reshape(B * Lp, ATT_HEADS * ATT_HD)
    return _matmul([ret, att], w_out, i, 0, D_MODEL, 1024)


def _router_kernel(x_ref, wt_ref, bias_ref, eidx_ref, wts_ref, rank_ref, cnt_ref, carry_ref, *, TM):
    E = N_EXPERTS

    @pl.when(pl.program_id(0) == 0)
    def _():
        carry_ref[...] = jnp.zeros_like(carry_ref)

    xh, xm, _ = _split3(x_ref[...])
    wh, wm, _ = _split3(wt_ref[...])
    lead = _dot_nt(jnp.concatenate([wh, wm], axis=0), xh)
    logits = lead[:E] + (lead[E:] + _dot_nt(wh, xm))
    scores = jax.nn.sigmoid(logits)
    choice = scores + bias_ref[...]
    ninf = -jnp.inf
    io8 = lax.broadcasted_iota(jnp.int32, (GROUP_SIZE, TM), 0)
    gs_rows = []
    for g in range(N_GROUPS):
        cg = choice[g * GROUP_SIZE:(g + 1) * GROUP_SIZE, :]
        m1 = jnp.max(cg, axis=0, keepdims=True)
        i1 = jnp.min(jnp.where(cg == m1, io8, GROUP_SIZE), axis=0, keepdims=True)
        m2 = jnp.max(jnp.where(io8 == i1, ninf, cg), axis=0, keepdims=True)
        gs_rows.append(m1 + m2)
    gs = jnp.concatenate(gs_rows, axis=0)
    gsel = jnp.zeros((N_GROUPS, TM), jnp.int32)
    for _ in range(TOPK_GROUPS):
        m = jnp.max(gs, axis=0, keepdims=True)
        idx = jnp.min(jnp.where(gs == m, io8, N_GROUPS), axis=0, keepdims=True)
        hit = io8 == idx
        gsel = jnp.where(hit, 1, gsel)
        gs = jnp.where(hit, ninf, gs)
    masked = jnp.concatenate(
        [jnp.where(gsel[g:g + 1, :] > 0, choice[g * GROUP_SIZE:(g + 1) * GROUP_SIZE, :], ninf)
         for g in range(N_GROUPS)], axis=0)
    ioe = lax.broadcasted_iota(jnp.int32, (E, TM), 0)
    sel = jnp.zeros((E, TM), F32)
    idx_rows, w_rows = [], []
    for _ in range(TOP_K):
        m = jnp.max(masked, axis=0, keepdims=True)
        idx = jnp.min(jnp.where(masked == m, ioe, E), axis=0, keepdims=True)
        hit = ioe == idx
        idx_rows.append(idx)
        w_rows.append(jnp.sum(jnp.where(hit, scores, 0.0), axis=0, keepdims=True))
        sel = jnp.where(hit, 1.0, sel)
        masked = jnp.where(hit, ninf, masked)
    wsum = w_rows[0]
    for w in w_rows[1:]:
        wsum = wsum + w
    ti = lax.broadcasted_iota(jnp.int32, (TM, TM), 0)
    tj = lax.broadcasted_iota(jnp.int32, (TM, TM), 1)
    before = jnp.where(ti < tj, 1.0, 0.0).astype(BF16)
    rank = _dot(sel.astype(BF16), before) + carry_ref[:, 0:1]
    rank_rows = [jnp.sum(jnp.where(ioe == idx, rank, 0.0), axis=0, keepdims=True) for idx in idx_rows]
    eidx_ref[...] = jnp.concatenate(idx_rows, axis=0)
    wts_ref[...] = jnp.concatenate([w / wsum * ROUTE_SCALE for w in w_rows], axis=0)
    rank_ref[...] = jnp.concatenate(rank_rows, axis=0).astype(jnp.int32)
    carry_ref[...] = carry_ref[...] + jnp.sum(sel, axis=1, keepdims=True)
    cnt_ref[...] = carry_ref[...]


def _router(h, w_router, router_bias):
    T, Dm = h.shape
    TM = _pick_tile(T, 256)
    tok = pl.BlockSpec((TOP_K, TM), lambda i: (0, i))
    return pl.pallas_call(
        functools.partial(_router_kernel, TM=TM),
        grid=(T // TM,),
        in_specs=[pl.BlockSpec((TM, Dm), lambda i: (i, 0)),
                  pl.BlockSpec((N_EXPERTS, Dm), lambda i: (0, 0)),
                  pl.BlockSpec((N_EXPERTS, 1), lambda i: (0, 0))],
        out_specs=[tok, tok, tok, pl.BlockSpec((N_EXPERTS, 128), lambda i: (0, 0))],
        out_shape=[jax.ShapeDtypeStruct((TOP_K, T), jnp.int32), jax.ShapeDtypeStruct((TOP_K, T), F32),
                   jax.ShapeDtypeStruct((TOP_K, T), jnp.int32), jax.ShapeDtypeStruct((N_EXPERTS, 128), F32)],
        scratch_shapes=[pltpu.VMEM((N_EXPERTS, 128), F32)],
        compiler_params=_cp("arbitrary"),
        name="moe_router",
    )(h, w_router.T, router_bias.astype(F32).reshape(N_EXPERTS, 1))


def _dispatch_kernel(dest_hbm, fill_hbm, hp_hbm, xs_hbm, dest_smem, fill_smem, zrow_ref, xbuf_ref,
                     sem, tsem, isem, fsem, zsem, *, TM, nt, nfill):
    i = pl.program_id(0)
    R = PACK_ROWS
    NI = TM * TOP_K

    def idx_copy(tile, region):
        return pltpu.make_async_copy(dest_hbm.at[tile], dest_smem.at[pl.ds(pl.multiple_of(region * NI, NI), NI)],
                                     isem.at[region])

    @pl.when(i == 0)
    def _():
        idx_copy(0, 0).start()

    @pl.when(i + 1 < nt)
    def _():
        idx_copy(i + 1, (i + 1) % 2).start()

    def slot_rows(d):
        return xs_hbm.at[pl.ds(pl.multiple_of(d * R, R), R)]

    @pl.when(i == 0)
    def _():
        zrow_ref[...] = jnp.zeros_like(zrow_ref)
        fc = pltpu.make_async_copy(fill_hbm, fill_smem, fsem)
        fc.start()
        fc.wait()

        def per_expert(e, carry):
            first = fill_smem[e]

            def body(r, c):
                pltpu.make_async_copy(zrow_ref, slot_rows(first + r), zsem).start()
                return c

            return lax.fori_loop(0, fill_smem[N_EXPERTS + e], body, carry)

        lax.fori_loop(0, N_EXPERTS, per_expert, 0)

        def per_expert_wait(e, carry):
            def wbody(r, c):
                pltpu.make_async_copy(zrow_ref, slot_rows(0), zsem).wait()
                return c

            return lax.fori_loop(0, fill_smem[N_EXPERTS + e], wbody, carry)

        lax.fori_loop(0, N_EXPERTS, per_expert_wait, 0)

    def tile_copy(tile, b):
        rows = pl.ds(pl.multiple_of(tile * (TM * R), TM * R), TM * R)
        return pltpu.make_async_copy(hp_hbm.at[rows], xbuf_ref.at[b], tsem.at[b])

    @pl.when(i == 0)
    def _():
        tile_copy(0, 0).start()
        if nt > 1:
            tile_copy(1, 1).start()

    idx_copy(i, i % 2).wait()
    ibase = (i % 2) * NI
    cur = i % 3
    tile_copy(i, cur).wait()

    def issue(grp, carry):
        t0 = grp * ISSUE_GROUP
        base = ibase + t0 * TOP_K
        slots = [[dest_smem[base + (u * TOP_K + k)] for k in range(TOP_K)] for u in range(ISSUE_GROUP)]
        for u in range(ISSUE_GROUP):
            src = xbuf_ref.at[cur, pl.ds(pl.multiple_of((t0 + u) * R, R), R)]
            for k in range(TOP_K):
                pltpu.make_async_copy(src, slot_rows(slots[u][k]), sem.at[cur]).start(priority=k % 2)
        return carry

    lax.fori_loop(0, TM // ISSUE_GROUP, issue, 0)

    def retire(b):
        for k in range(TOP_K):
            pltpu.make_async_copy(xbuf_ref.at[b], xs_hbm.at[pl.ds(0, TM * R)], sem.at[b]).wait()

    @pl.when(i > 0)
    def _():
        retire((i + 2) % 3)

    @pl.when(i + 2 < nt)
    def _():
        tile_copy(i + 2, (i + 2) % 3).start()

    @pl.when(i == nt - 1)
    def _():
        retire(cur)


def _dispatch(hp, dest_tiles, fill, rows):
    nt = dest_tiles.shape[0]
    TM = dest_tiles.shape[1] // TOP_K
    nfill = fill.shape[0]
    return pl.pallas_call(
        functools.partial(_dispatch_kernel, TM=TM, nt=nt, nfill=nfill),
        grid=(nt,),
        in_specs=[pl.BlockSpec(memory_space=pl.ANY), pl.BlockSpec(memory_space=pl.ANY),
                  pl.BlockSpec(memory_space=pl.ANY)],
        out_specs=pl.BlockSpec(memory_space=pl.ANY),
        out_shape=jax.ShapeDtypeStruct((rows * PACK_ROWS, 128), jnp.uint32),
        scratch_shapes=[pltpu.SMEM((2 * TOP_K * TM,), jnp.int32), pltpu.SMEM((nfill,), jnp.int32),
                        pltpu.VMEM((PACK_ROWS, 128), jnp.uint32),
                        pltpu.VMEM((3, TM * PACK_ROWS, 128), jnp.uint32),
                        pltpu.SemaphoreType.DMA((3,)), pltpu.SemaphoreType.DMA((3,)),
                        pltpu.SemaphoreType.DMA((2,)), pltpu.SemaphoreType.DMA, pltpu.SemaphoreType.DMA],
        compiler_params=_cp("arbitrary"),
        name="moe_dispatch",
    )(dest_tiles, fill, hp)


def _expert_kernel(be_ref, na_ref, x_ref, wg_ref, wu_ref, wd_ref, o_ref, wgb_ref, wub_ref, wdb_ref):
    i = pl.program_id(0)
    active = i < na_ref[0]
    new_expert = jnp.logical_or(i == 0, be_ref[i] != be_ref[jnp.maximum(i - 1, 0)])

    @pl.when(jnp.logical_and(active, new_expert))
    def _():
        wgb_ref[...] = wg_ref[0, 0].astype(BF16)
        wub_ref[...] = wu_ref[0, 0].astype(BF16)
        wdb_ref[...] = wd_ref[0, 0].astype(BF16)

    @pl.when(active)
    def _():
        x = jnp.concatenate([c.astype(BF16) for c in _unpack_rows(x_ref, MOE_BLOCK)], axis=1)
        a = _dot(x, wgb_ref[...])
        b = _dot(x, wub_ref[...])
        hmid = (a * jax.nn.sigmoid(a) * b).astype(BF16)
        _pack_rows(o_ref, _dot(hmid, wdb_ref[...]))


def _experts(xs, blk_e, nact, w_gate, w_up, w_down, layer):
    Dm = D_MODEL
    rows = xs.shape[0] // PACK_ROWS
    nblk = rows // MOE_BLOCK
    pblock = MOE_BLOCK * PACK_ROWS

    def row(i, be, na):
        return (jnp.minimum(i, na[0] - 1), 0)

    def wsel(i, be, na):
        return (layer, be[jnp.minimum(i, na[0] - 1)], 0, 0)

    return pl.pallas_call(
        _expert_kernel,
        grid_spec=pltpu.PrefetchScalarGridSpec(
            num_scalar_prefetch=2,
            grid=(nblk,),
            in_specs=[pl.BlockSpec((pblock, 128), row),
                      pl.BlockSpec((1, 1, Dm, D_EXPERT), wsel),
                      pl.BlockSpec((1, 1, Dm, D_EXPERT), wsel),
                      pl.BlockSpec((1, 1, D_EXPERT, Dm), wsel)],
            out_specs=pl.BlockSpec((pblock, 128), row),
            scratch_shapes=[pltpu.VMEM((Dm, D_EXPERT), BF16), pltpu.VMEM((Dm, D_EXPERT), BF16),
                            pltpu.VMEM((D_EXPERT, Dm), BF16)],
        ),
        out_shape=jax.ShapeDtypeStruct((rows * PACK_ROWS, 128), jnp.uint32),
        compiler_params=_cp("arbitrary"),
        name="moe_experts",
    )(blk_e, nact, xs, w_gate, w_up, w_down)


def _combine_kernel(dest_hbm, yb_hbm, w_ref, h_ref, sh_ref, g_ref, b_ref, o_ref, ob_ref,
                    dest_smem, buf_ref, routed_ref, sem, isem, *, TM, nt):
    i = pl.program_id(0)
    R = PACK_ROWS
    G = 8
    half = D_MODEL // 2

    NI = TM * TOP_K

    def idx_copy(tile, region):
        return pltpu.make_async_copy(dest_hbm.at[tile], dest_smem.at[pl.ds(pl.multiple_of(region * NI, NI), NI)],
                                     isem.at[region])

    @pl.when(i == 0)
    def _():
        idx_copy(0, 0).start()
        idx_copy(0, 0).wait()
        idx_copy(min(1, nt - 1), 1).start()

        def first(grp, carry):
            t0 = grp * G
            slots = [[dest_smem[t0 * TOP_K + (u * TOP_K + k)] for k in range(TOP_K)] for u in range(G)]
            for u in range(G):
                for k in range(TOP_K):
                    src = yb_hbm.at[pl.ds(pl.multiple_of(slots[u][k] * R, R), R)]
                    dst = buf_ref.at[0, k, pl.ds(pl.multiple_of((t0 + u) * R, R), R)]
                    pltpu.make_async_copy(src, dst, sem.at[0]).start(priority=k % 2)
            return carry

        lax.fori_loop(0, TM // G, first, 0)

    slot = i % 2
    nreg = (i + 1) % 3
    idx_copy(jnp.minimum(i + 1, nt - 1), nreg).wait()

    @pl.when(i + 1 < nt)
    def _():
        idx_copy(jnp.minimum(i + 2, nt - 1), (i + 2) % 3).start()
    nslot = (i + 1) % 2

    def retire(b):
        for k in range(TOP_K):
            pltpu.make_async_copy(yb_hbm.at[pl.ds(0, TM * R)], buf_ref.at[b, k], sem.at[b]).wait()

    retire(slot)

    def step(grp, carry):
        t0 = grp * G
        base = nreg * NI + t0 * TOP_K
        slots = [[dest_smem[base + (u * TOP_K + k)] for k in range(TOP_K)] for u in range(G)]
        r0 = pl.multiple_of(grp * G, G)
        w = w_ref[pl.ds(r0, G), :]
        lo = [None] * R
        hi = [None] * R
        for k in range(TOP_K):
            wk = w[:, k:k + 1]
            words = [buf_ref[slot, k, pl.ds(grp * (G * R) + j, G, stride=R), :] for j in range(R)]
            u = k
            for kk in range(TOP_K):
                src = yb_hbm.at[pl.ds(pl.multiple_of(slots[u][kk] * R, R), R)]
                dst = buf_ref.at[nslot, kk, pl.ds(pl.multiple_of((t0 + u) * R, R), R)]
                pltpu.make_async_copy(src, dst, sem.at[nslot]).start(priority=kk % 2)
            for j in range(R):
                a = pltpu.bitcast(words[j] << 16, F32) * wk
                b = pltpu.bitcast(words[j] & jnp.uint32(HI_MASK), F32) * wk
                lo[j] = a if lo[j] is None else lo[j] + a
                hi[j] = b if hi[j] is None else hi[j] + b
        for j in range(R):
            routed_ref[pl.ds(r0, G), 128 * j:128 * (j + 1)] = lo[j]
            routed_ref[pl.ds(r0, G), half + 128 * j:half + 128 * (j + 1)] = hi[j]
        return carry

    lax.fori_loop(0, TM // G, step, 0)

    @pl.when(i == nt - 1)
    def _():
        retire(nslot)

    out = _layer_norm_rows(DN_ALPHA * h_ref[...] + (routed_ref[...] + sh_ref[...]), g_ref[...], b_ref[...])
    o_ref[...] = out
    ob_ref[...] = out.astype(BF16)


def _combine_ln(yb, dest_tiles, wts_tok, h, shared, g, b, drop_front=None):
    nt = dest_tiles.shape[0]
    TM = dest_tiles.shape[1] // TOP_K
    T, Dm = h.shape
    row = pl.BlockSpec((TM, Dm), lambda i: (i, 0))
    vec = pl.BlockSpec((1, Dm), lambda i: (0, 0))
    hbm = pl.BlockSpec(memory_space=pl.ANY)
    out_row, out_rows = row, T
    if drop_front is not None:
        assert TM == FRONT
        npb = drop_front
        out_row = pl.BlockSpec((TM, Dm), lambda i: ((i // npb) * (npb - 1) + jnp.maximum(i % npb - 1, 0), 0))
        out_rows = T - (T // (npb * TM)) * FRONT
    return pl.pallas_call(
        functools.partial(_combine_kernel, TM=TM, nt=nt),
        grid=(nt,),
        in_specs=[hbm, hbm, pl.BlockSpec((TM, TOP_K), lambda i: (i, 0)), row, row, vec, vec],
        out_specs=[out_row, row],
        out_shape=[jax.ShapeDtypeStruct((out_rows, Dm), F32), jax.ShapeDtypeStruct((T, Dm), BF16)],
        scratch_shapes=[pltpu.SMEM((3 * TOP_K * TM,), jnp.int32),
                        pltpu.VMEM((2, TOP_K, TM * PACK_ROWS, 128), jnp.uint32),
                        pltpu.VMEM((TM, Dm), F32),
                        pltpu.SemaphoreType.DMA((2,)), pltpu.SemaphoreType.DMA((3,))],
        compiler_params=_cp("arbitrary"),
        name="moe_combine",
    )(dest_tiles, yb, wts_tok, h, shared, g.reshape(1, Dm), b.reshape(1, Dm))


def _shared_kernel(x_ref, wg_ref, wu_ref, wd_ref, o_ref, wgb_ref, wub_ref, wdb_ref):
    @pl.when(pl.program_id(0) == 0)
    def _():
        wgb_ref[...] = wg_ref[...].astype(BF16)
        wub_ref[...] = wu_ref[...].astype(BF16)
        wdb_ref[...] = wd_ref[...].astype(BF16)

    x = x_ref[...]
    a = _dot(x, wgb_ref[...])
    b = _dot(x, wub_ref[...])
    o_ref[...] = _dot((a * jax.nn.sigmoid(a) * b).astype(BF16), wdb_ref[...])


def _shared_expert(hb, wg, wu, wd):
    T, Dm = hb.shape
    Ds = wg.shape[1]
    TM = _pick_tile(T, DENSE_TM)
    full = lambda r, c: pl.BlockSpec((r, c), lambda i: (0, 0))
    return pl.pallas_call(
        _shared_kernel,
        grid=(T // TM,),
        in_specs=[pl.BlockSpec((TM, Dm), lambda i: (i, 0)), full(Dm, Ds), full(Dm, Ds), full(Ds, Dm)],
        out_specs=pl.BlockSpec((TM, Dm), lambda i: (i, 0)),
        out_shape=jax.ShapeDtypeStruct((T, Dm), F32),
        scratch_shapes=[pltpu.VMEM((Dm, Ds), BF16), pltpu.VMEM((Dm, Ds), BF16), pltpu.VMEM((Ds, Dm), BF16)],
        compiler_params=_cp("arbitrary"),
        name="moe_shared",
    )(hb, wg, wu, wd)


def _moe_ln(h, hb, hp, w_router, router_bias, w_gate, w_up, w_down, layer, ws_gate, ws_up, ws_down, g, b,
            drop_front=None):
    T, Dm = h.shape
    eidx, wts, rank, cnt = _router(h, w_router, router_bias)
    counts = cnt[:, 0].astype(jnp.int32)
    pcounts = (counts + MOE_BLOCK - 1) // MOE_BLOCK * MOE_BLOCK
    pends = jnp.cumsum(pcounts)
    pstarts = pends - pcounts
    nblk = -(-T * TOP_K // MOE_BLOCK) + N_EXPERTS
    rows = nblk * MOE_BLOCK
    eid = jnp.arange(N_EXPERTS, dtype=jnp.int32)
    dest = jnp.sum(jnp.where(eidx[..., None] == eid, pstarts.astype(jnp.int32), 0), axis=-1) + rank
    TM = _pick_tile(T, 128)
    dest_tiles = dest.T.reshape(T // TM, TM * TOP_K)
    blk_start = jnp.arange(nblk, dtype=jnp.int32) * MOE_BLOCK
    blk_e = jnp.minimum(jnp.sum((pends[None, :] <= blk_start[:, None]).astype(jnp.int32), axis=1),
                        N_EXPERTS - 1)
    nact = (pends[-1:] // MOE_BLOCK).astype(jnp.int32)
    fill = jnp.concatenate([pstarts + counts, pcounts - counts]).astype(jnp.int32)
    xs = _dispatch(hp, dest_tiles, fill, rows)
    yb = _experts(xs, blk_e, nact, w_gate, w_up, w_down, layer)
    shared = _shared_expert(hb, ws_gate, ws_up, ws_down)
    return _combine_ln(yb, dest_tiles, wts.T, h, shared, g, b, drop_front)


def kernel(x, meta_tokens, ev_w_in, ev_w_out, ev_sink, od_w_in, od_conv_w, od_a_log, od_dt_bias, od_norm_w,
           od_w_out, ln_g, ln_b, w_router, router_bias, w_gate, w_up, w_down, ws_gate, ws_up, ws_down):
    B, S, Dm = x.shape
    Lp = S + FRONT
    meta = jnp.broadcast_to(meta_tokens.astype(x.dtype)[None], (B, N_META, Dm))
    h = jnp.concatenate([jnp.zeros((B, DEAD, Dm), x.dtype), meta, x], axis=1).reshape(B * Lp, Dm)
    hb = h.astype(BF16)
    for layer in range(DEPTH):
        i = layer // 2
        if layer % 2 == 0:
            mix = _even_mixer(hb, B, Lp, i, ev_w_in, ev_w_out, ev_sink[i])
        else:
            mix = _gdn_mixer(hb, B, Lp, i, od_w_in, od_conv_w[i], od_a_log[i], od_dt_bias[i],
                             od_norm_w[i], od_w_out)
        h, hb, hp = _residual_ln(h, mix, ln_g[layer, 0], ln_b[layer, 0])
        last = layer == DEPTH - 1
        h, hb = _moe_ln(h, hb, hp, w_router[layer], router_bias[layer], w_gate, w_up, w_down, layer,
                        ws_gate[layer], ws_up[layer], ws_down[layer], ln_g[layer, 1], ln_b[layer, 1],
                        drop_front=Lp // FRONT if last else None)
    return h.reshape(B, S, Dm)
```

```python
import functools
import math

import numpy as np
import jax
import jax.numpy as jnp
from jax import lax
from jax.experimental import pallas as pl
from jax.experimental.pallas import tpu as pltpu

F32 = jnp.float32
BF16 = jnp.bfloat16

D_MODEL = 2048
DEPTH = 4
N_META = 16
FRONT = 128
DEAD = FRONT - N_META
RET_HEADS = 8
RET_HD = 128
RET_W = RET_HEADS * RET_HD
RET_CHUNK = 128
ATT_HEADS = 8
ATT_KV_HEADS = 2
ATT_GROUP = ATT_HEADS // ATT_KV_HEADS
ATT_HD = 128
ATT_BLOCK = 128
WINDOW = 128
EVEN_IN = 4 * RET_W + ATT_HEADS * ATT_HD + 2 * ATT_KV_HEADS * ATT_HD
GDN_QK_HEADS = 16
GDN_V_HEADS = 32
GDN_HD = 128
GDN_K_W = GDN_QK_HEADS * GDN_HD
GDN_V_W = GDN_V_HEADS * GDN_HD
GDN_CONV_CH = 2 * GDN_K_W + GDN_V_W
GDN_CHUNK = 64
GDN_HG = 32
CONV_WIDTH = 5
N_EXPERTS = 64
TOP_K = 8
N_GROUPS = 8
GROUP_SIZE = N_EXPERTS // N_GROUPS
TOPK_GROUPS = 4
D_EXPERT = 384
ROUTE_SCALE = 2.5
MOE_BLOCK = 768
DENSE_TM = 1280
ISSUE_GROUP = 4
DN_ALPHA = (2 * DEPTH) ** 0.25
LN_EPS = 1e-5
NORM_EPS = 1e-6
NEG = -1e30

VMEM_LIMIT = 56 * 2**20


def _cp(*sem, vmem=VMEM_LIMIT):
    return pltpu.CompilerParams(dimension_semantics=sem, vmem_limit_bytes=vmem)


def _dot(a, b):
    return jnp.dot(a, b, preferred_element_type=F32)


def _dot_nt(a, b):
    return lax.dot_general(a, b, (((1,), (1,)), ((), ())), preferred_element_type=F32)


def _dot_tn(a, b):
    return lax.dot_general(a, b, (((0,), (0,)), ((), ())), preferred_element_type=F32)


def _split3(a):
    hi = a.astype(BF16)
    r1 = a - hi.astype(F32)
    mid = r1.astype(BF16)
    lo = (r1 - mid.astype(F32)).astype(BF16)
    return hi, mid, lo


def _pick_tile(n, cap, mult=8):
    for t in range(min(cap, n), 0, -1):
        if n % t == 0 and t % mult == 0:
            return t
    return n


def _mm_kernel(*refs, ksplits):
    nx = len(ksplits)
    x_refs, w_ref, o_ref, wb_ref = refs[:nx], refs[nx], refs[nx + 1], refs[nx + 2]

    @pl.when(pl.program_id(1) == 0)
    def _():
        wb_ref[...] = w_ref[...].astype(BF16)

    acc = None
    k0 = 0
    for x_ref, kk in zip(x_refs, ksplits):
        part = _dot(x_ref[...].astype(BF16), wb_ref[k0:k0 + kk, :])
        acc = part if acc is None else acc + part
        k0 += kk
    o_ref[...] = acc.astype(o_ref.dtype)


def _matmul(xs, w, idx, col0, ncols, tn, out_dtype=F32, tm_cap=DENSE_TM):
    M = xs[0].shape[0]
    ksplits = tuple(x.shape[1] for x in xs)
    K = sum(ksplits)
    assert w.shape[1] == K and ncols % tn == 0 and col0 % tn == 0
    tm = _pick_tile(M, tm_cap)
    in_specs = [pl.BlockSpec((tm, kk), lambda j, i: (i, 0)) for kk in ksplits]
    in_specs.append(pl.BlockSpec((None, K, tn), lambda j, i: (idx, 0, col0 // tn + j)))
    return pl.pallas_call(
        functools.partial(_mm_kernel, ksplits=ksplits),
        grid=(ncols // tn, M // tm),
        in_specs=in_specs,
        out_specs=pl.BlockSpec((tm, tn), lambda j, i: (i, j)),
        out_shape=jax.ShapeDtypeStruct((M, ncols), out_dtype),
        scratch_shapes=[pltpu.VMEM((K, tn), BF16)],
        compiler_params=_cp("arbitrary", "arbitrary"),
        name="matmul",
    )(*xs, w)


PACK_ROWS = D_MODEL // 256
HI_MASK = 0xFFFF0000


def _pack_rows(ref, x):
    n = x.shape[0]
    half = D_MODEL // 2
    for j in range(PACK_ROWS):
        lo = x[:, 128 * j:128 * (j + 1)].astype(BF16).astype(F32)
        hi = x[:, half + 128 * j:half + 128 * (j + 1)].astype(BF16).astype(F32)
        word = (pltpu.bitcast(lo, jnp.uint32) >> 16) | (pltpu.bitcast(hi, jnp.uint32) & jnp.uint32(HI_MASK))
        ref[pl.ds(j, n, stride=PACK_ROWS), :] = word


def _unpack_rows(ref, n):
    lo, hi = [], []
    for j in range(PACK_ROWS):
        word = ref[pl.ds(j, n, stride=PACK_ROWS), :]
        lo.append(pltpu.bitcast(word << 16, F32))
        hi.append(pltpu.bitcast(word & jnp.uint32(HI_MASK), F32))
    return lo + hi


def _layer_norm_rows(y, g, b):
    mu = jnp.mean(y, axis=-1, keepdims=True)
    d = y - mu
    var = jnp.mean(d * d, axis=-1, keepdims=True)
    return d * lax.rsqrt(var + LN_EPS) * g + b


def _ln_kernel(h_ref, a_ref, g_ref, b_ref, o_ref, ob_ref, op_ref):
    out = _layer_norm_rows(DN_ALPHA * h_ref[...] + a_ref[...], g_ref[...], b_ref[...])
    o_ref[...] = out
    ob_ref[...] = out.astype(BF16)
    _pack_rows(op_ref, out)


def _residual_ln(h, add, g, b):
    M, Dm = h.shape
    tm = _pick_tile(M, 256)
    row = pl.BlockSpec((tm, Dm), lambda i: (i, 0))
    vec = pl.BlockSpec((1, Dm), lambda i: (0, 0))
    return pl.pallas_call(
        _ln_kernel,
        grid=(M // tm,),
        in_specs=[row, row, vec, vec],
        out_specs=[row, row, pl.BlockSpec((tm * PACK_ROWS, 128), lambda i: (i, 0))],
        out_shape=[jax.ShapeDtypeStruct((M, Dm), F32), jax.ShapeDtypeStruct((M, Dm), BF16),
                   jax.ShapeDtypeStruct((M * PACK_ROWS, 128), jnp.uint32)],
        compiler_params=_cp("arbitrary"),
        name="residual_ln",
    )(h, add, g.reshape(1, Dm), b.reshape(1, Dm))


def _ret_tables(C):
    hh = np.arange(RET_HEADS, dtype=np.float64)
    lg = np.log(1.0 - 2.0 ** (-5.0 - hh))[:, None]
    pos = np.arange(C, dtype=np.float64)[None, :]
    vecs = np.stack([np.exp(lg * (pos + 1.0)),
                     np.exp(lg * (C - pos)),
                     np.exp(lg * (C - 1.0 - pos)),
                     np.exp(lg * pos),
                     np.exp(lg * C) * np.ones_like(pos)], axis=1)
    tab = np.broadcast_to(vecs[..., None], (RET_HEADS, 5, C, RET_HD))
    rel = np.abs(pos.T - pos)
    dsym = np.exp(lg[:, :, None] * rel[None])
    return jnp.asarray(tab, F32), jnp.asarray(dsym, F32)


def _ret_kernel(q_ref, k_ref, v_ref, g_ref, tab_ref, d_ref, o_ref, fst_ref, run_ref, *, nc, C):
    s = pl.program_id(1)
    fwd = s < nc
    c = jnp.where(fwd, s, 2 * nc - 1 - s)
    row = c * C + lax.broadcasted_iota(jnp.int32, (C, 1), 0)
    live = row >= DEAD
    heads = range(RET_HEADS)
    HD = RET_HD
    ks = [jnp.where(live, k_ref[0, :, h * HD:(h + 1) * HD] * (RET_HD ** -0.5), 0.0) for h in heads]
    vbs = [jnp.where(live, v_ref[0, :, h * HD:(h + 1) * HD], 0.0).astype(BF16) for h in heads]

    @pl.when(jnp.logical_or(s == 0, s == nc))
    def _():
        run_ref[...] = jnp.zeros_like(run_ref)

    @pl.when(fwd)
    def _():
        for h in heads:
            fst_ref[c, h] = run_ref[h].astype(BF16)
        upd = [_dot_tn((ks[h] * tab_ref[h, 2]).astype(BF16), vbs[h]) for h in heads]
        for h in heads:
            run_ref[h] = run_ref[h] * tab_ref[h, 4] + upd[h]

    @pl.when(jnp.logical_not(fwd))
    def _():
        qs = [q_ref[0, :, h * HD:(h + 1) * HD] for h in heads]
        sc = [(_dot_nt(qs[h].astype(BF16), ks[h].astype(BF16)) * d_ref[h]).astype(BF16) for h in heads]
        left = [_dot((qs[h] * tab_ref[h, 0]).astype(BF16), fst_ref[c, h]) for h in heads]
        right = [_dot((qs[h] * tab_ref[h, 1]).astype(BF16), run_ref[h].astype(BF16)) for h in heads]
        intra = [_dot(sc[h], vbs[h]) for h in heads]
        upd = [_dot_tn((ks[h] * tab_ref[h, 3]).astype(BF16), vbs[h]) for h in heads]
        for h in heads:
            run_ref[h] = run_ref[h] * tab_ref[h, 4] + upd[h]
            out = intra[h] + left[h] + right[h]
            mu = jnp.mean(out, axis=-1, keepdims=True)
            dlt = out - mu
            var = jnp.mean(dlt * dlt, axis=-1, keepdims=True)
            normed = dlt * lax.rsqrt(var + NORM_EPS)
            g = g_ref[0, :, h * HD:(h + 1) * HD]
            o_ref[0, :, h * HD:(h + 1) * HD] = (g * jax.nn.sigmoid(g) * normed).astype(o_ref.dtype)


def _retention(proj):
    B, Lp, _ = proj.shape
    C = RET_CHUNK
    nc = Lp // C
    tab, dsym = _ret_tables(C)

    def cidx(s):
        return jnp.where(s < nc, s, 2 * nc - 1 - s)

    def cidx_out(s):
        return jnp.where(s < nc, nc - 1, 2 * nc - 1 - s)

    return pl.pallas_call(
        functools.partial(_ret_kernel, nc=nc, C=C),
        grid=(B, 2 * nc),
        in_specs=[
            pl.BlockSpec((1, C, RET_W), lambda b, s: (b, cidx_out(s), 0)),
            pl.BlockSpec((1, C, RET_W), lambda b, s: (b, cidx(s), 1)),
            pl.BlockSpec((1, C, RET_W), lambda b, s: (b, cidx(s), 2)),
            pl.BlockSpec((1, C, RET_W), lambda b, s: (b, cidx_out(s), 3)),
            pl.BlockSpec((RET_HEADS, 5, C, RET_HD), lambda b, s: (0, 0, 0, 0)),
            pl.BlockSpec((RET_HEADS, C, C), lambda b, s: (0, 0, 0)),
        ],
        out_specs=pl.BlockSpec((1, C, RET_W), lambda b, s: (b, cidx_out(s), 0)),
        out_shape=jax.ShapeDtypeStruct((B, Lp, RET_W), BF16),
        scratch_shapes=[pltpu.VMEM((nc, RET_HEADS, RET_HD, RET_HD), BF16),
                        pltpu.VMEM((RET_HEADS, RET_HD, RET_HD), F32)],
        compiler_params=_cp("arbitrary", "arbitrary"),
        name="retention",
    )(proj, proj, proj, proj, tab, dsym)


def _att_kernel(sink_ref, slope_ref, q_ref, km_ref, kp_ref, kc_ref, kn_ref,
                vm_ref, vp_ref, vc_ref, vn_ref, o_ref, *, nb):
    qb = pl.program_id(1)
    T = ATT_BLOCK
    HD = ATT_HD
    ii = lax.broadcasted_iota(jnp.int32, (T, T), 0)
    jj = lax.broadcasted_iota(jnp.int32, (T, T), 1)
    meta_ok = jj >= DEAD
    pieces = []
    for off, k_ref, v_ref, ok in ((-T, kp_ref, vp_ref, qb >= 2),
                                  (0, kc_ref, vc_ref, qb >= 1),
                                  (T, kn_ref, vn_ref, qb + 1 <= nb)):
        dist = jnp.abs(jj + off - ii)
        pieces.append((dist.astype(F32), dist <= WINDOW, k_ref, v_ref, ok))
    kv_heads = range(ATT_KV_HEADS)
    keys = [[km_ref[0, :, j * HD:(j + 1) * HD].astype(BF16)]
            + [p[2][0, :, j * HD:(j + 1) * HD].astype(BF16) for p in pieces] for j in kv_heads]
    vals = [[vm_ref[0, :, j * HD:(j + 1) * HD].astype(BF16)]
            + [p[3][0, :, j * HD:(j + 1) * HD].astype(BF16) for p in pieces] for j in kv_heads]
    groups = range(ATT_HEADS)
    qs = [(q_ref[0, :, g * HD:(g + 1) * HD] * (ATT_HD ** -0.5)).astype(BF16) for g in groups]
    raw = [[_dot_nt(qs[g], kk) for kk in keys[g // ATT_GROUP]] for g in groups]
    es, dens = [], []
    for g in groups:
        slope = slope_ref[g // ATT_GROUP, g % ATT_GROUP]
        sink = sink_ref[g // ATT_GROUP, g % ATT_GROUP]
        s_list = [jnp.where(meta_ok, raw[g][0], NEG)]
        for (dist, inwin, _, _, ok), r in zip(pieces, raw[g][1:]):
            s_list.append(jnp.where(ok, jnp.where(inwin, r - slope * dist, NEG), NEG))
        m = jnp.full((T, 1), sink, F32)
        for sb in s_list:
            m = jnp.maximum(m, jnp.max(sb, axis=-1, keepdims=True))
        e_list = [jnp.exp(sb - m) for sb in s_list]
        den = jnp.exp(sink - m)
        for e in e_list:
            den = den + jnp.sum(e, axis=-1, keepdims=True)
        es.append([e.astype(BF16) for e in e_list])
        dens.append(den)
    pv = [[_dot(e, vb) for e, vb in zip(es[g], vals[g // ATT_GROUP])] for g in groups]
    for g in groups:
        acc = pv[g][0] + pv[g][1] + pv[g][2] + pv[g][3]
        o_ref[0, :, g * HD:(g + 1) * HD] = (acc / dens[g]).astype(o_ref.dtype)


def _window_attention(proj, sink):
    B, Lp, _ = proj.shape
    nb = Lp // ATT_BLOCK - 1
    qw = ATT_HEADS * ATT_HD
    kw = ATT_KV_HEADS * ATT_HD
    q0 = 4 * RET_W // qw
    k0 = (4 * RET_W + qw) // kw
    v0 = k0 + 1
    slopes = np.asarray(2.0 ** (-8.0 * (np.arange(ATT_HEADS) + 1.0) / ATT_HEADS), np.float32)
    smem = pl.BlockSpec(memory_space=pltpu.SMEM)

    def kv(c0, f):
        return pl.BlockSpec((1, ATT_BLOCK, kw), lambda b, t: (b, f(t), c0))

    rows = [lambda t: 0, lambda t: jnp.maximum(t - 1, 0), lambda t: t, lambda t: jnp.minimum(t + 1, nb)]
    return pl.pallas_call(
        functools.partial(_att_kernel, nb=nb),
        grid=(B, nb + 1),
        in_specs=[smem, smem, pl.BlockSpec((1, ATT_BLOCK, qw), lambda b, t: (b, t, q0))]
        + [kv(k0, f) for f in rows] + [kv(v0, f) for f in rows],
        out_specs=pl.BlockSpec((1, ATT_BLOCK, qw), lambda b, t: (b, t, 0)),
        out_shape=jax.ShapeDtypeStruct((B, Lp, qw), BF16),
        compiler_params=_cp("arbitrary", "arbitrary"),
        name="window_attention",
    )(sink.astype(F32).reshape(ATT_KV_HEADS, ATT_GROUP), jnp.asarray(slopes).reshape(ATT_KV_HEADS, ATT_GROUP),
      *([proj] * 9))


CONV_HALO = 16


def _conv_shift_matrix(TT):
    half = CONV_WIDTH // 2
    t = np.arange(TT)[:, None]
    s = np.arange(TT + 2 * CONV_HALO)[None, :]
    blocks = [(s == t + CONV_HALO + tap - half) for tap in range(CONV_WIDTH) if tap != half]
    return jnp.asarray(np.concatenate(blocks, axis=0), BF16)


def _conv_kernel(xp_ref, x_ref, xn_ref, w_ref, sh_ref, o_ref, *, mode, TT, nt):
    t = pl.program_id(1)
    H = CONV_HALO
    rows = t * TT + lax.broadcasted_iota(jnp.int32, (TT, 1), 0)
    live = rows >= DEAD
    x = jnp.where(live, x_ref[0].astype(F32), 0.0)
    prow = t * TT - H + lax.broadcasted_iota(jnp.int32, (H, 1), 0)
    prev = jnp.where(prow >= DEAD, xp_ref[0].astype(F32), 0.0)
    nrow = (t + 1) * TT + lax.broadcasted_iota(jnp.int32, (H, 1), 0)
    nxt = jnp.where(jnp.logical_and(nrow >= DEAD, t < nt - 1), xn_ref[0].astype(F32), 0.0)
    staged = jnp.concatenate([prev.astype(BF16), x.astype(BF16), nxt.astype(BF16)], axis=0)
    shifted = _dot(sh_ref[...], staged)
    w = w_ref[...]
    half = CONV_WIDTH // 2
    acc = x * w[half:half + 1, :]
    blk = 0
    for tap in range(CONV_WIDTH):
        if tap == half:
            continue
        acc = acc + shifted[blk * TT:(blk + 1) * TT, :] * w[tap:tap + 1, :]
        blk += 1
    y = acc * jax.nn.sigmoid(acc)
    if mode in ("q", "k"):
        scale = GDN_HD ** -0.5 if mode == "q" else 1.0
        for a in range(y.shape[1] // GDN_HD):
            ya = y[:, a * GDN_HD:(a + 1) * GDN_HD]
            ya = ya * lax.rsqrt(jnp.sum(ya * ya, axis=-1, keepdims=True) + NORM_EPS)
            if mode == "q":
                ya = ya * scale
            o_ref[0, :, a * GDN_HD:(a + 1) * GDN_HD] = jnp.where(live, ya, 0.0).astype(o_ref.dtype)
    else:
        o_ref[0] = jnp.where(live, y, 0.0).astype(o_ref.dtype)


def _short_conv(qkv, conv_w, mode, col0, ncols):
    B, Lp, _ = qkv.shape
    H = CONV_HALO
    TT = _pick_tile(Lp, 256, mult=H)
    TC = 1024
    nt = Lp // TT
    c0 = col0 // TC
    shift = _conv_shift_matrix(TT)
    return pl.pallas_call(
        functools.partial(_conv_kernel, mode=mode, TT=TT, nt=nt),
        grid=(B, nt, ncols // TC),
        in_specs=[
            pl.BlockSpec((1, H, TC), lambda b, t, c: (b, jnp.maximum(t * (TT // H) - 1, 0), c0 + c)),
            pl.BlockSpec((1, TT, TC), lambda b, t, c: (b, t, c0 + c)),
            pl.BlockSpec((1, H, TC), lambda b, t, c: (b, jnp.minimum((t + 1) * (TT // H), Lp // H - 1), c0 + c)),
            pl.BlockSpec((CONV_WIDTH, TC), lambda b, t, c: (0, c0 + c)),
            pl.BlockSpec(shift.shape, lambda b, t, c: (0, 0)),
        ],
        out_specs=pl.BlockSpec((1, TT, TC), lambda b, t, c: (b, t, c)),
        out_shape=jax.ShapeDtypeStruct((B, Lp, ncols), BF16),
        compiler_params=_cp("arbitrary", "arbitrary", "arbitrary"),
        name="short_conv_" + mode,
    )(qkv, qkv, qkv, conv_w, shift)


def _gate_kernel(ba_ref, alog_ref, dtb_ref, o_ref, *, TT):
    t = pl.program_id(1)
    rows = t * TT + lax.broadcasted_iota(jnp.int32, (TT, 1), 0)
    lane = lax.broadcasted_iota(jnp.int32, (1, 4 * GDN_V_HEADS), 1)
    is_g = (lane // GDN_V_HEADS) % 2 == 1
    x = ba_ref[0]
    beta = jax.nn.sigmoid(x)
    xs = x + dtb_ref[...]
    softplus = jnp.maximum(xs, 0.0) + jnp.log(1.0 + jnp.exp(-jnp.abs(xs)))
    g = -jnp.exp(alog_ref[...]) * softplus
    o_ref[0] = jnp.where(rows >= DEAD, jnp.where(is_g, g, beta), 0.0)


def _gates(ba, a_log, dt_bias):
    B, Lp, W = ba.shape
    TT = _pick_tile(Lp, 1024)
    zeros = jnp.zeros((2, 1, GDN_V_HEADS), F32)
    alog = jnp.concatenate([zeros, a_log.astype(F32)[:, None, :]], axis=1).reshape(1, W)
    dtb = jnp.concatenate([zeros, dt_bias.astype(F32)[:, None, :]], axis=1).reshape(1, W)
    vec = pl.BlockSpec((1, W), lambda b, t: (0, 0))
    return pl.pallas_call(
        functools.partial(_gate_kernel, TT=TT),
        grid=(B, Lp // TT),
        in_specs=[pl.BlockSpec((1, TT, W), lambda b, t: (b, t, 0)), vec, vec],
        out_specs=pl.BlockSpec((1, TT, W), lambda b, t: (b, t, 0)),
        out_shape=jax.ShapeDtypeStruct((B, Lp, W), F32),
        compiler_params=_cp("arbitrary", "arbitrary"),
        name="gdn_gates",
    )(ba, alog, dtb)


def _gdn_kernel(*refs, rev, final):
    if final:
        q_ref, k_ref, v_ref, gc_ref, gr_ref, of_ref, z_ref, nw_ref, o_ref, s_ref = refs
    else:
        q_ref, k_ref, v_ref, gc_ref, gr_ref, o_ref, s_ref = refs
    C = GDN_CHUNK
    HG = GDN_HG

    @pl.when(pl.program_id(2) == 0)
    def _():
        s_ref[...] = jnp.zeros_like(s_ref)

    ii = lax.broadcasted_iota(jnp.int32, (C, C), 0)
    jj = lax.broadcasted_iota(jnp.int32, (C, C), 1)
    incl = (jj >= ii) if rev else (jj <= ii)
    strict = (jj > ii) if rev else (jj < ii)
    tri_col = jnp.where(incl, 1.0, 0.0).astype(BF16)
    tri_row = jnp.where((ii >= jj) if rev else (ii <= jj), 1.0, 0.0).astype(BF16)
    gcol = gc_ref[0, 0]
    grow = gr_ref[0, 0, 0]
    gc_col = sum(_dot(tri_col, p) for p in _split3(gcol))
    gc_row = sum(_dot(p, tri_row) for p in _split3(grow))
    last = 0 if rev else C - 1
    d0 = 2 * HG if rev else 0
    HD = GDN_HD
    heads = range(HG)
    pairs = range(HG // 2)
    qs = [q_ref[0, :, p * HD:(p + 1) * HD] for p in pairs]
    ks = [k_ref[0, :, p * HD:(p + 1) * HD] for p in pairs]
    kfs = [k.astype(F32) for k in ks]
    kq = [_dot_nt(jnp.concatenate([ks[p], qs[p]], axis=0), ks[p]) for p in pairs]
    gram = [r[:C] for r in kq]
    qk = [r[C:] for r in kq]
    beta = [gcol[:, d0 + h:d0 + h + 1] for h in heads]
    gcc = [gc_col[:, d0 + HG + h:d0 + HG + h + 1] for h in heads]
    gcr = [gc_row[d0 + HG + h:d0 + HG + h + 1, :] for h in heads]
    gtot = [g[last:last + 1, :] for g in gcc]
    decay = [jnp.where(incl, jnp.exp(jnp.where(incl, gcc[h] - gcr[h], 0.0)), 0.0) for h in heads]
    eg = [jnp.exp(g) for g in gcc]
    ms = [-jnp.where(strict, beta[h] * gram[h // 2] * decay[h], 0.0) for h in heads]
    pbs = [m.astype(BF16) for m in ms]
    ps = [_dot(pb, pb) for pb in pbs]
    pbs = [p.astype(BF16) for p in ps]
    for _ in range(int(math.log2(C)) - 2):
        res = [_dot(jnp.concatenate([ms[h].astype(BF16), pbs[h]], axis=0), pbs[h]) for h in heads]
        ms = [ms[h] + ps[h] + res[h][:C] for h in heads]
        ps = [r[C:] for r in res]
        pbs = [p.astype(BF16) for p in ps]
    ms = [ms[h] + ps[h] + _dot(ms[h].astype(BF16), pbs[h]) for h in heads]
    rhs = [jnp.concatenate([v_ref[0, :, h * HD:(h + 1) * HD].astype(F32) * beta[h],
                            kfs[h // 2] * (beta[h] * eg[h])], axis=1) for h in heads]
    sol = [rhs[h] + _dot(ms[h].astype(BF16), rhs[h].astype(BF16)) for h in heads]
    st = [s_ref[h] for h in heads]
    stb = [s.astype(BF16) for s in st]
    lhs = [jnp.concatenate([sol[h][:, HD:].astype(BF16), (qs[h // 2].astype(F32) * eg[h]).astype(BF16)], axis=0)
           for h in heads]
    ws = [_dot(lhs[h], stb[h]) for h in heads]
    vnb = [(sol[h][:, :HD] - ws[h][:C]).astype(BF16) for h in heads]
    outs = [ws[h][C:] + _dot((qk[h // 2] * decay[h]).astype(BF16), vnb[h]) for h in heads]
    upd = [_dot_tn((kfs[h // 2] * jnp.exp(gtot[h] - gcc[h])).astype(BF16), vnb[h]) for h in heads]
    for h in heads:
        s_ref[h] = st[h] * jnp.exp(gtot[h]) + upd[h]
        sl = slice(h * HD, (h + 1) * HD)
        if final:
            o = outs[h] + of_ref[0, :, sl]
            z = z_ref[0, :, sl]
            o = o * lax.rsqrt(jnp.mean(o * o, axis=-1, keepdims=True) + NORM_EPS) * nw_ref[...]
            o_ref[0, :, sl] = (o * (z * jax.nn.sigmoid(z))).astype(o_ref.dtype)
        else:
            o_ref[0, :, sl] = outs[h]


def _gdn_dir(q, k, v, gcol, grow, rev, o_fwd=None, z=None, norm_w=None):
    B, Lp, _ = v.shape
    C = GDN_CHUNK
    nc = Lp // C
    ng = GDN_V_HEADS // GDN_HG
    qw = GDN_HG // 2 * GDN_HD
    vw = GDN_HG * GDN_HD
    final = o_fwd is not None

    def cc(c):
        return nc - 1 - c if rev else c

    in_specs = [
        pl.BlockSpec((1, C, qw), lambda b, g, c: (b, cc(c), g)),
        pl.BlockSpec((1, C, qw), lambda b, g, c: (b, cc(c), g)),
        pl.BlockSpec((1, C, vw), lambda b, g, c: (b, cc(c), g)),
        pl.BlockSpec((1, 1, C, 4 * GDN_HG), lambda b, g, c: (b, g, cc(c), 0)),
        pl.BlockSpec((1, 1, 1, 4 * GDN_HG, C), lambda b, g, c: (b, g, cc(c), 0, 0)),
    ]
    args = [q, k, v, gcol, grow]
    if final:
        in_specs += [pl.BlockSpec((1, C, vw), lambda b, g, c: (b, cc(c), g)),
                     pl.BlockSpec((1, C, vw), lambda b, g, c: (b, cc(c), g)),
                     pl.BlockSpec((1, GDN_HD), lambda b, g, c: (0, 0))]
        args += [o_fwd, z, norm_w.astype(F32).reshape(1, GDN_HD)]
    return pl.pallas_call(
        functools.partial(_gdn_kernel, rev=rev, final=final),
        grid=(B, ng, nc),
        in_specs=in_specs,
        out_specs=pl.BlockSpec((1, C, vw), lambda b, g, c: (b, cc(c), g)),
        out_shape=jax.ShapeDtypeStruct((B, Lp, GDN_V_W), BF16 if final else F32),
        scratch_shapes=[pltpu.VMEM((GDN_HG, GDN_HD, GDN_HD), F32)],
        compiler_params=_cp("arbitrary", "arbitrary", "arbitrary"),
        name="gdn_bwd" if rev else "gdn_fwd",
    )(*args)


def _gdn_mixer(hb, B, Lp, i, w_in, conv_w, a_log, dt_bias, norm_w, w_out):
    qkv = _matmul([hb], w_in, i, 0, GDN_CONV_CH, 1024, out_dtype=BF16).reshape(B, Lp, GDN_CONV_CH)
    z = _matmul([hb], w_in, i, GDN_CONV_CH, GDN_V_W, 1024).reshape(B, Lp, GDN_V_W)
    ba = _matmul([hb], w_in, i, GDN_CONV_CH + GDN_V_W, 4 * GDN_V_HEADS, 128).reshape(B, Lp, 4 * GDN_V_HEADS)
    q = _short_conv(qkv, conv_w, "q", 0, GDN_K_W)
    k = _short_conv(qkv, conv_w, "k", GDN_K_W, GDN_K_W)
    v = _short_conv(qkv, conv_w, "v", 2 * GDN_K_W, GDN_V_W)
    gb = _gates(ba, a_log, dt_bias)
    ng = GDN_V_HEADS // GDN_HG
    nc = Lp // GDN_CHUNK
    gcol = gb.reshape(B, Lp, 4, ng, GDN_HG).transpose(0, 3, 1, 2, 4).reshape(B, ng, Lp, 4 * GDN_HG)
    grow = gcol.reshape(B, ng, nc, GDN_CHUNK, 4 * GDN_HG).transpose(0, 1, 2, 4, 3)
    o_f = _gdn_dir(q, k, v, gcol, grow, rev=False)
    o = _gdn_dir(q, k, v, gcol, grow, rev=True, o_fwd=o_f, z=z, norm_w=norm_w)
    return _matmul([o.reshape(B * Lp, GDN_V_W)], w_out, i, 0, D_MODEL, 512)


def _even_mixer(hb, B, Lp, i, w_in, w_out, sink):
    proj = _matmul([hb], w_in, i, 0, EVEN_IN, 512, tm_cap=2080).reshape(B, Lp, EVEN_IN)
    ret = _retention(proj).reshape(B * Lp, RET_W)
    att = _window_attention(proj, sink).reshape(B * Lp, ATT_HEADS * ATT_HD)
    return _matmul([ret, att], w_out, i, 0, D_MODEL, 1024)


def _router_kernel(x_ref, wt_ref, bias_ref, eidx_ref, wts_ref, rank_ref, cnt_ref, carry_ref, *, TM):
    E = N_EXPERTS

    @pl.when(pl.program_id(0) == 0)
    def _():
        carry_ref[...] = jnp.zeros_like(carry_ref)

    xh, xm, _ = _split3(x_ref[...])
    wh, wm, _ = _split3(wt_ref[...])
    lead = _dot_nt(jnp.concatenate([wh, wm], axis=0), xh)
    logits = lead[:E] + (lead[E:] + _dot_nt(wh, xm))
    scores = jax.nn.sigmoid(logits)
    choice = scores + bias_ref[...]
    ninf = -jnp.inf
    io8 = lax.broadcasted_iota(jnp.int32, (GROUP_SIZE, TM), 0)
    gs_rows = []
    for g in range(N_GROUPS):
        cg = choice[g * GROUP_SIZE:(g + 1) * GROUP_SIZE, :]
        m1 = jnp.max(cg, axis=0, keepdims=True)
        i1 = jnp.min(jnp.where(cg == m1, io8, GROUP_SIZE), axis=0, keepdims=True)
        m2 = jnp.max(jnp.where(io8 == i1, ninf, cg), axis=0, keepdims=True)
        gs_rows.append(m1 + m2)
    gs = jnp.concatenate(gs_rows, axis=0)
    gsel = jnp.zeros((N_GROUPS, TM), jnp.int32)
    for _ in range(TOPK_GROUPS):
        m = jnp.max(gs, axis=0, keepdims=True)
        idx = jnp.min(jnp.where(gs == m, io8, N_GROUPS), axis=0, keepdims=True)
        hit = io8 == idx
        gsel = jnp.where(hit, 1, gsel)
        gs = jnp.where(hit, ninf, gs)
    masked = jnp.concatenate(
        [jnp.where(gsel[g:g + 1, :] > 0, choice[g * GROUP_SIZE:(g + 1) * GROUP_SIZE, :], ninf)
         for g in range(N_GROUPS)], axis=0)
    ioe = lax.broadcasted_iota(jnp.int32, (E, TM), 0)
    sel = jnp.zeros((E, TM), F32)
    idx_rows, w_rows = [], []
    for _ in range(TOP_K):
        m = jnp.max(masked, axis=0, keepdims=True)
        idx = jnp.min(jnp.where(masked == m, ioe, E), axis=0, keepdims=True)
        hit = ioe == idx
        idx_rows.append(idx)
        w_rows.append(jnp.sum(jnp.where(hit, scores, 0.0), axis=0, keepdims=True))
        sel = jnp.where(hit, 1.0, sel)
        masked = jnp.where(hit, ninf, masked)
    wsum = w_rows[0]
    for w in w_rows[1:]:
        wsum = wsum + w
    ti = lax.broadcasted_iota(jnp.int32, (TM, TM), 0)
    tj = lax.broadcasted_iota(jnp.int32, (TM, TM), 1)
    before = jnp.where(ti < tj, 1.0, 0.0).astype(BF16)
    rank = _dot(sel.astype(BF16), before) + carry_ref[:, 0:1]
    rank_rows = [jnp.sum(jnp.where(ioe == idx, rank, 0.0), axis=0, keepdims=True) for idx in idx_rows]
    eidx_ref[...] = jnp.concatenate(idx_rows, axis=0)
    wts_ref[...] = jnp.concatenate([w / wsum * ROUTE_SCALE for w in w_rows], axis=0)
    rank_ref[...] = jnp.concatenate(rank_rows, axis=0).astype(jnp.int32)
    carry_ref[...] = carry_ref[...] + jnp.sum(sel, axis=1, keepdims=True)
    cnt_ref[...] = carry_ref[...]


def _router(h, w_router, router_bias):
    T, Dm = h.shape
    TM = _pick_tile(T, 256)
    tok = pl.BlockSpec((TOP_K, TM), lambda i: (0, i))
    return pl.pallas_call(
        functools.partial(_router_kernel, TM=TM),
        grid=(T // TM,),
        in_specs=[pl.BlockSpec((TM, Dm), lambda i: (i, 0)),
                  pl.BlockSpec((N_EXPERTS, Dm), lambda i: (0, 0)),
                  pl.BlockSpec((N_EXPERTS, 1), lambda i: (0, 0))],
        out_specs=[tok, tok, tok, pl.BlockSpec((N_EXPERTS, 128), lambda i: (0, 0))],
        out_shape=[jax.ShapeDtypeStruct((TOP_K, T), jnp.int32), jax.ShapeDtypeStruct((TOP_K, T), F32),
                   jax.ShapeDtypeStruct((TOP_K, T), jnp.int32), jax.ShapeDtypeStruct((N_EXPERTS, 128), F32)],
        scratch_shapes=[pltpu.VMEM((N_EXPERTS, 128), F32)],
        compiler_params=_cp("arbitrary"),
        name="moe_router",
    )(h, w_router.T, router_bias.astype(F32).reshape(N_EXPERTS, 1))


def _dispatch_kernel(dest_hbm, fill_hbm, hp_hbm, xs_hbm, dest_smem, fill_smem, zrow_ref, xbuf_ref,
                     sem, tsem, isem, fsem, zsem, *, TM, nt, nfill):
    i = pl.program_id(0)
    R = PACK_ROWS
    NI = TM * TOP_K

    def idx_copy(tile, region):
        return pltpu.make_async_copy(dest_hbm.at[tile], dest_smem.at[pl.ds(pl.multiple_of(region * NI, NI), NI)],
                                     isem.at[region])

    @pl.when(i == 0)
    def _():
        idx_copy(0, 0).start()

    @pl.when(i + 1 < nt)
    def _():
        idx_copy(i + 1, (i + 1) % 2).start()

    def slot_rows(d):
        return xs_hbm.at[pl.ds(pl.multiple_of(d * R, R), R)]

    @pl.when(i == 0)
    def _():
        zrow_ref[...] = jnp.zeros_like(zrow_ref)
        fc = pltpu.make_async_copy(fill_hbm, fill_smem, fsem)
        fc.start()
        fc.wait()

        def per_expert(e, carry):
            first = fill_smem[e]

            def body(r, c):
                pltpu.make_async_copy(zrow_ref, slot_rows(first + r), zsem).start()
                return c

            return lax.fori_loop(0, fill_smem[N_EXPERTS + e], body, carry)

        lax.fori_loop(0, N_EXPERTS, per_expert, 0)

        def per_expert_wait(e, carry):
            def wbody(r, c):
                pltpu.make_async_copy(zrow_ref, slot_rows(0), zsem).wait()
                return c

            return lax.fori_loop(0, fill_smem[N_EXPERTS + e], wbody, carry)

        lax.fori_loop(0, N_EXPERTS, per_expert_wait, 0)

    def tile_copy(tile, b):
        rows = pl.ds(pl.multiple_of(tile * (TM * R), TM * R), TM * R)
        return pltpu.make_async_copy(hp_hbm.at[rows], xbuf_ref.at[b], tsem.at[b])

    @pl.when(i == 0)
    def _():
        tile_copy(0, 0).start()
        if nt > 1:
            tile_copy(1, 1).start()

    idx_copy(i, i % 2).wait()
    ibase = (i % 2) * NI
    cur = i % 3
    tile_copy(i, cur).wait()

    def issue(grp, carry):
        t0 = grp * ISSUE_GROUP
        base = ibase + t0 * TOP_K
        slots = [[dest_smem[base + (u * TOP_K + k)] for k in range(TOP_K)] for u in range(ISSUE_GROUP)]
        for u in range(ISSUE_GROUP):
            src = xbuf_ref.at[cur, pl.ds(pl.multiple_of((t0 + u) * R, R), R)]
            for k in range(TOP_K):
                pltpu.make_async_copy(src, slot_rows(slots[u][k]), sem.at[cur]).start(priority=k % 2)
        return carry

    lax.fori_loop(0, TM // ISSUE_GROUP, issue, 0)

    def retire(b):
        for k in range(TOP_K):
            pltpu.make_async_copy(xbuf_ref.at[b], xs_hbm.at[pl.ds(0, TM * R)], sem.at[b]).wait()

    @pl.when(i > 0)
    def _():
        retire((i + 2) % 3)

    @pl.when(i + 2 < nt)
    def _():
        tile_copy(i + 2, (i + 2) % 3).start()

    @pl.when(i == nt - 1)
    def _():
        retire(cur)


def _dispatch(hp, dest_tiles, fill, rows):
    nt = dest_tiles.shape[0]
    TM = dest_tiles.shape[1] // TOP_K
    nfill = fill.shape[0]
    return pl.pallas_call(
        functools.partial(_dispatch_kernel, TM=TM, nt=nt, nfill=nfill),
        grid=(nt,),
        in_specs=[pl.BlockSpec(memory_space=pl.ANY), pl.BlockSpec(memory_space=pl.ANY),
                  pl.BlockSpec(memory_space=pl.ANY)],
        out_specs=pl.BlockSpec(memory_space=pl.ANY),
        out_shape=jax.ShapeDtypeStruct((rows * PACK_ROWS, 128), jnp.uint32),
        scratch_shapes=[pltpu.SMEM((2 * TOP_K * TM,), jnp.int32), pltpu.SMEM((nfill,), jnp.int32),
                        pltpu.VMEM((PACK_ROWS, 128), jnp.uint32),
                        pltpu.VMEM((3, TM * PACK_ROWS, 128), jnp.uint32),
                        pltpu.SemaphoreType.DMA((3,)), pltpu.SemaphoreType.DMA((3,)),
                        pltpu.SemaphoreType.DMA((2,)), pltpu.SemaphoreType.DMA, pltpu.SemaphoreType.DMA],
        compiler_params=_cp("arbitrary"),
        name="moe_dispatch",
    )(dest_tiles, fill, hp)


def _expert_kernel(be_ref, na_ref, x_ref, wg_ref, wu_ref, wd_ref, o_ref, wgb_ref, wub_ref, wdb_ref):
    i = pl.program_id(0)
    active = i < na_ref[0]
    new_expert = jnp.logical_or(i == 0, be_ref[i] != be_ref[jnp.maximum(i - 1, 0)])

    @pl.when(jnp.logical_and(active, new_expert))
    def _():
        wgb_ref[...] = wg_ref[0, 0].astype(BF16)
        wub_ref[...] = wu_ref[0, 0].astype(BF16)
        wdb_ref[...] = wd_ref[0, 0].astype(BF16)

    @pl.when(active)
    def _():
        x = jnp.concatenate([c.astype(BF16) for c in _unpack_rows(x_ref, MOE_BLOCK)], axis=1)
        a = _dot(x, wgb_ref[...])
        b = _dot(x, wub_ref[...])
        hmid = (a * jax.nn.sigmoid(a) * b).astype(BF16)
        _pack_rows(o_ref, _dot(hmid, wdb_ref[...]))


def _experts(xs, blk_e, nact, w_gate, w_up, w_down, layer):
    Dm = D_MODEL
    rows = xs.shape[0] // PACK_ROWS
    nblk = rows // MOE_BLOCK
    pblock = MOE_BLOCK * PACK_ROWS

    def row(i, be, na):
        return (jnp.minimum(i, na[0] - 1), 0)

    def wsel(i, be, na):
        return (layer, be[jnp.minimum(i, na[0] - 1)], 0, 0)

    return pl.pallas_call(
        _expert_kernel,
        grid_spec=pltpu.PrefetchScalarGridSpec(
            num_scalar_prefetch=2,
            grid=(nblk,),
            in_specs=[pl.BlockSpec((pblock, 128), row),
                      pl.BlockSpec((1, 1, Dm, D_EXPERT), wsel),
                      pl.BlockSpec((1, 1, Dm, D_EXPERT), wsel),
                      pl.BlockSpec((1, 1, D_EXPERT, Dm), wsel)],
            out_specs=pl.BlockSpec((pblock, 128), row),
            scratch_shapes=[pltpu.VMEM((Dm, D_EXPERT), BF16), pltpu.VMEM((Dm, D_EXPERT), BF16),
                            pltpu.VMEM((D_EXPERT, Dm), BF16)],
        ),
        out_shape=jax.ShapeDtypeStruct((rows * PACK_ROWS, 128), jnp.uint32),
        compiler_params=_cp("arbitrary"),
        name="moe_experts",
    )(blk_e, nact, xs, w_gate, w_up, w_down)


def _combine_kernel(dest_hbm, yb_hbm, w_ref, h_ref, sh_ref, g_ref, b_ref, o_ref, ob_ref,
                    dest_smem, buf_ref, routed_ref, sem, isem, *, TM, nt):
    i = pl.program_id(0)
    R = PACK_ROWS
    G = 8
    half = D_MODEL // 2

    NI = TM * TOP_K

    def idx_copy(tile, region):
        return pltpu.make_async_copy(dest_hbm.at[tile], dest_smem.at[pl.ds(pl.multiple_of(region * NI, NI), NI)],
                                     isem.at[region])

    @pl.when(i == 0)
    def _():
        idx_copy(0, 0).start()
        idx_copy(0, 0).wait()
        idx_copy(min(1, nt - 1), 1).start()

        def first(grp, carry):
            t0 = grp * G
            slots = [[dest_smem[t0 * TOP_K + (u * TOP_K + k)] for k in range(TOP_K)] for u in range(G)]
            for u in range(G):
                for k in range(TOP_K):
                    src = yb_hbm.at[pl.ds(pl.multiple_of(slots[u][k] * R, R), R)]
                    dst = buf_ref.at[0, k, pl.ds(pl.multiple_of((t0 + u) * R, R), R)]
                    pltpu.make_async_copy(src, dst, sem.at[0]).start(priority=k % 2)
            return carry

        lax.fori_loop(0, TM // G, first, 0)

    slot = i % 2
    nreg = (i + 1) % 3
    idx_copy(jnp.minimum(i + 1, nt - 1), nreg).wait()

    @pl.when(i + 1 < nt)
    def _():
        idx_copy(jnp.minimum(i + 2, nt - 1), (i + 2) % 3).start()
    nslot = (i + 1) % 2

    def retire(b):
        for k in range(TOP_K):
            pltpu.make_async_copy(yb_hbm.at[pl.ds(0, TM * R)], buf_ref.at[b, k], sem.at[b]).wait()

    retire(slot)

    def step(grp, carry):
        t0 = grp * G
        base = nreg * NI + t0 * TOP_K
        slots = [[dest_smem[base + (u * TOP_K + k)] for k in range(TOP_K)] for u in range(G)]
        r0 = pl.multiple_of(grp * G, G)
        w = w_ref[pl.ds(r0, G), :]
        lo = [None] * R
        hi = [None] * R
        for k in range(TOP_K):
            wk = w[:, k:k + 1]
            words = [buf_ref[slot, k, pl.ds(grp * (G * R) + j, G, stride=R), :] for j in range(R)]
            u = k
            for kk in range(TOP_K):
                src = yb_hbm.at[pl.ds(pl.multiple_of(slots[u][kk] * R, R), R)]
                dst = buf_ref.at[nslot, kk, pl.ds(pl.multiple_of((t0 + u) * R, R), R)]
                pltpu.make_async_copy(src, dst, sem.at[nslot]).start(priority=kk % 2)
            for j in range(R):
                a = pltpu.bitcast(words[j] << 16, F32) * wk
                b = pltpu.bitcast(words[j] & jnp.uint32(HI_MASK), F32) * wk
                lo[j] = a if lo[j] is None else lo[j] + a
                hi[j] = b if hi[j] is None else hi[j] + b
        for j in range(R):
            routed_ref[pl.ds(r0, G), 128 * j:128 * (j + 1)] = lo[j]
            routed_ref[pl.ds(r0, G), half + 128 * j:half + 128 * (j + 1)] = hi[j]
        return carry

    lax.fori_loop(0, TM // G, step, 0)

    @pl.when(i == nt - 1)
    def _():
        retire(nslot)

    out = _layer_norm_rows(DN_ALPHA * h_ref[...] + (routed_ref[...] + sh_ref[...]), g_ref[...], b_ref[...])
    o_ref[...] = out
    ob_ref[...] = out.astype(BF16)


def _combine_ln(yb, dest_tiles, wts_tok, h, shared, g, b, drop_front=None):
    nt = dest_tiles.shape[0]
    TM = dest_tiles.shape[1] // TOP_K
    T, Dm = h.shape
    row = pl.BlockSpec((TM, Dm), lambda i: (i, 0))
    vec = pl.BlockSpec((1, Dm), lambda i: (0, 0))
    hbm = pl.BlockSpec(memory_space=pl.ANY)
    out_row, out_rows = row, T
    if drop_front is not None:
        assert TM == FRONT
        npb = drop_front
        out_row = pl.BlockSpec((TM, Dm), lambda i: ((i // npb) * (npb - 1) + jnp.maximum(i % npb - 1, 0), 0))
        out_rows = T - (T // (npb * TM)) * FRONT
    return pl.pallas_call(
        functools.partial(_combine_kernel, TM=TM, nt=nt),
        grid=(nt,),
        in_specs=[hbm, hbm, pl.BlockSpec((TM, TOP_K), lambda i: (i, 0)), row, row, vec, vec],
        out_specs=[out_row, row],
        out_shape=[jax.ShapeDtypeStruct((out_rows, Dm), F32), jax.ShapeDtypeStruct((T, Dm), BF16)],
        scratch_shapes=[pltpu.SMEM((3 * TOP_K * TM,), jnp.int32),
                        pltpu.VMEM((2, TOP_K, TM * PACK_ROWS, 128), jnp.uint32),
                        pltpu.VMEM((TM, Dm), F32),
                        pltpu.SemaphoreType.DMA((2,)), pltpu.SemaphoreType.DMA((3,))],
        compiler_params=_cp("arbitrary"),
        name="moe_combine",
    )(dest_tiles, yb, wts_tok, h, shared, g.reshape(1, Dm), b.reshape(1, Dm))


def _shared_kernel(x_ref, wg_ref, wu_ref, wd_ref, o_ref, wgb_ref, wub_ref, wdb_ref):
    @pl.when(pl.program_id(0) == 0)
    def _():
        wgb_ref[...] = wg_ref[...].astype(BF16)
        wub_ref[...] = wu_ref[...].astype(BF16)
        wdb_ref[...] = wd_ref[...].astype(BF16)

    x = x_ref[...]
    a = _dot(x, wgb_ref[...])
    b = _dot(x, wub_ref[...])
    o_ref[...] = _dot((a * jax.nn.sigmoid(a) * b).astype(BF16), wdb_ref[...])


def _shared_expert(hb, wg, wu, wd):
    T, Dm = hb.shape
    Ds = wg.shape[1]
    TM = _pick_tile(T, DENSE_TM)
    full = lambda r, c: pl.BlockSpec((r, c), lambda i: (0, 0))
    return pl.pallas_call(
        _shared_kernel,
        grid=(T // TM,),
        in_specs=[pl.BlockSpec((TM, Dm), lambda i: (i, 0)), full(Dm, Ds), full(Dm, Ds), full(Ds, Dm)],
        out_specs=pl.BlockSpec((TM, Dm), lambda i: (i, 0)),
        out_shape=jax.ShapeDtypeStruct((T, Dm), F32),
        scratch_shapes=[pltpu.VMEM((Dm, Ds), BF16), pltpu.VMEM((Dm, Ds), BF16), pltpu.VMEM((Ds, Dm), BF16)],
        compiler_params=_cp("arbitrary"),
        name="moe_shared",
    )(hb, wg, wu, wd)


def _moe_ln(h, hb, hp, w_router, router_bias, w_gate, w_up, w_down, layer, ws_gate, ws_up, ws_down, g, b,
            drop_front=None):
    T, Dm = h.shape
    eidx, wts, rank, cnt = _router(h, w_router, router_bias)
    counts = cnt[:, 0].astype(jnp.int32)
    pcounts = (counts + MOE_BLOCK - 1) // MOE_BLOCK * MOE_BLOCK
    pends = jnp.cumsum(pcounts)
    pstarts = pends - pcounts
    nblk = -(-T * TOP_K // MOE_BLOCK) + N_EXPERTS
    rows = nblk * MOE_BLOCK
    eid = jnp.arange(N_EXPERTS, dtype=jnp.int32)
    dest = jnp.sum(jnp.where(eidx[..., None] == eid, pstarts.astype(jnp.int32), 0), axis=-1) + rank
    TM = _pick_tile(T, 128)
    dest_tiles = dest.T.reshape(T // TM, TM * TOP_K)
    blk_start = jnp.arange(nblk, dtype=jnp.int32) * MOE_BLOCK
    blk_e = jnp.minimum(jnp.sum((pends[None, :] <= blk_start[:, None]).astype(jnp.int32), axis=1),
                        N_EXPERTS - 1)
    nact = (pends[-1:] // MOE_BLOCK).astype(jnp.int32)
    fill = jnp.concatenate([pstarts + counts, pcounts - counts]).astype(jnp.int32)
    xs = _dispatch(hp, dest_tiles, fill, rows)
    yb = _experts(xs, blk_e, nact, w_gate, w_up, w_down, layer)
    shared = _shared_expert(hb, ws_gate, ws_up, ws_down)
    return _combine_ln(yb, dest_tiles, wts.T, h, shared, g, b, drop_front)


def kernel(x, meta_tokens, ev_w_in, ev_w_out, ev_sink, od_w_in, od_conv_w, od_a_log, od_dt_bias, od_norm_w,
           od_w_out, ln_g, ln_b, w_router, router_bias, w_gate, w_up, w_down, ws_gate, ws_up, ws_down):
    B, S, Dm = x.shape
    Lp = S + FRONT
    meta = jnp.broadcast_to(meta_tokens.astype(x.dtype)[None], (B, N_META, Dm))
    h = jnp.concatenate([jnp.zeros((B, DEAD, Dm), x.dtype), meta, x], axis=1).reshape(B * Lp, Dm)
    hb = h.astype(BF16)
    for layer in range(DEPTH):
        i = layer // 2
        if layer % 2 == 0:
            mix = _even_mixer(hb, B, Lp, i, ev_w_in, ev_w_out, ev_sink[i])
        else:
            mix = _gdn_mixer(hb, B, Lp, i, od_w_in, od_conv_w[i], od_a_log[i], od_dt_bias[i],
                             od_norm_w[i], od_w_out)
        h, hb, hp = _residual_ln(h, mix, ln_g[layer, 0], ln_b[layer, 0])
        last = layer == DEPTH - 1
        h, hb = _moe_ln(h, hb, hp, w_router[layer], router_bias[layer], w_gate, w_up, w_down, layer,
                        ws_gate[layer], ws_up[layer], ws_down[layer], ln_g[layer, 1], ln_b[layer, 1],
                        drop_front=Lp // FRONT if last else None)
    return h.reshape(B, S, Dm)
```

```python
import functools
import math

import numpy as np
import jax
import jax.numpy as jnp
from jax import lax
from jax.experimental import pallas as pl
from jax.experimental.pallas import tpu as pltpu

F32 = jnp.float32
BF16 = jnp.bfloat16

D_MODEL = 2048
DEPTH = 4
N_META = 16
FRONT = 128
DEAD = FRONT - N_META
RET_HEADS = 8
RET_HD = 128
RET_W = RET_HEADS * RET_HD
RET_CHUNK = 128
ATT_HEADS = 8
ATT_KV_HEADS = 2
ATT_GROUP = ATT_HEADS // ATT_KV_HEADS
ATT_HD = 128
ATT_BLOCK = 128
WINDOW = 128
EVEN_IN = 4 * RET_W + ATT_HEADS * ATT_HD + 2 * ATT_KV_HEADS * ATT_HD
GDN_QK_HEADS = 16
GDN_V_HEADS = 32
GDN_HD = 128
GDN_K_W = GDN_QK_HEADS * GDN_HD
GDN_V_W = GDN_V_HEADS * GDN_HD
GDN_CONV_CH = 2 * GDN_K_W + GDN_V_W
GDN_CHUNK = 64
GDN_HG = 32
CONV_WIDTH = 5
N_EXPERTS = 64
TOP_K = 8
N_GROUPS = 8
GROUP_SIZE = N_EXPERTS // N_GROUPS
TOPK_GROUPS = 4
D_EXPERT = 384
ROUTE_SCALE = 2.5
MOE_BLOCK = 512
DENSE_TM = 1280
ZCH = 32
ISSUE_GROUP = 4
DN_ALPHA = (2 * DEPTH) ** 0.25
LN_EPS = 1e-5
NORM_EPS = 1e-6
NEG = -1e30

VMEM_LIMIT = 56 * 2**20


def _cp(*sem, vmem=VMEM_LIMIT):
    return pltpu.CompilerParams(dimension_semantics=sem, vmem_limit_bytes=vmem)


def _dot(a, b):
    return jnp.dot(a, b, preferred_element_type=F32)


def _dot_nt(a, b):
    return lax.dot_general(a, b, (((1,), (1,)), ((), ())), preferred_element_type=F32)


def _dot_tn(a, b):
    return lax.dot_general(a, b, (((0,), (0,)), ((), ())), preferred_element_type=F32)


def _split3(a):
    hi = a.astype(BF16)
    r1 = a - hi.astype(F32)
    mid = r1.astype(BF16)
    lo = (r1 - mid.astype(F32)).astype(BF16)
    return hi, mid, lo


def _pick_tile(n, cap, mult=8):
    for t in range(min(cap, n), 0, -1):
        if n % t == 0 and t % mult == 0:
            return t
    return n


def _mm_kernel(*refs, ksplits):
    nx = len(ksplits)
    x_refs, w_ref, o_ref, wb_ref = refs[:nx], refs[nx], refs[nx + 1], refs[nx + 2]

    @pl.when(pl.program_id(1) == 0)
    def _():
        wb_ref[...] = w_ref[...].astype(BF16)

    acc = None
    k0 = 0
    for x_ref, kk in zip(x_refs, ksplits):
        part = _dot(x_ref[...].astype(BF16), wb_ref[k0:k0 + kk, :])
        acc = part if acc is None else acc + part
        k0 += kk
    o_ref[...] = acc.astype(o_ref.dtype)


def _matmul(xs, w, idx, col0, ncols, tn, out_dtype=F32, tm_cap=DENSE_TM):
    M = xs[0].shape[0]
    ksplits = tuple(x.shape[1] for x in xs)
    K = sum(ksplits)
    assert w.shape[1] == K and ncols % tn == 0 and col0 % tn == 0
    tm = _pick_tile(M, tm_cap)
    in_specs = [pl.BlockSpec((tm, kk), lambda j, i: (i, 0)) for kk in ksplits]
    in_specs.append(pl.BlockSpec((None, K, tn), lambda j, i: (idx, 0, col0 // tn + j)))
    return pl.pallas_call(
        functools.partial(_mm_kernel, ksplits=ksplits),
        grid=(ncols // tn, M // tm),
        in_specs=in_specs,
        out_specs=pl.BlockSpec((tm, tn), lambda j, i: (i, j)),
        out_shape=jax.ShapeDtypeStruct((M, ncols), out_dtype),
        scratch_shapes=[pltpu.VMEM((K, tn), BF16)],
        compiler_params=_cp("arbitrary", "arbitrary"),
        name="matmul",
    )(*xs, w)


PACK_ROWS = D_MODEL // 256
HI_MASK = 0xFFFF0000


def _pack_rows(ref, x):
    n = x.shape[0]
    half = D_MODEL // 2
    for j in range(PACK_ROWS):
        lo = x[:, 128 * j:128 * (j + 1)].astype(BF16).astype(F32)
        hi = x[:, half + 128 * j:half + 128 * (j + 1)].astype(BF16).astype(F32)
        word = (pltpu.bitcast(lo, jnp.uint32) >> 16) | (pltpu.bitcast(hi, jnp.uint32) & jnp.uint32(HI_MASK))
        ref[pl.ds(j, n, stride=PACK_ROWS), :] = word


def _unpack_rows(ref, n):
    lo, hi = [], []
    for j in range(PACK_ROWS):
        word = ref[pl.ds(j, n, stride=PACK_ROWS), :]
        lo.append(pltpu.bitcast(word << 16, F32))
        hi.append(pltpu.bitcast(word & jnp.uint32(HI_MASK), F32))
    return lo + hi


def _layer_norm_rows(y, g, b):
    mu = jnp.mean(y, axis=-1, keepdims=True)
    d = y - mu
    var = jnp.mean(d * d, axis=-1, keepdims=True)
    return d * lax.rsqrt(var + LN_EPS) * g + b


def _ln_kernel(h_ref, a_ref, g_ref, b_ref, o_ref, ob_ref, op_ref):
    out = _layer_norm_rows(DN_ALPHA * h_ref[...] + a_ref[...], g_ref[...], b_ref[...])
    o_ref[...] = out
    ob_ref[...] = out.astype(BF16)
    _pack_rows(op_ref, out)


def _residual_ln(h, add, g, b):
    M, Dm = h.shape
    tm = _pick_tile(M, 256)
    row = pl.BlockSpec((tm, Dm), lambda i: (i, 0))
    vec = pl.BlockSpec((1, Dm), lambda i: (0, 0))
    return pl.pallas_call(
        _ln_kernel,
        grid=(M // tm,),
        in_specs=[row, row, vec, vec],
        out_specs=[row, row, pl.BlockSpec((tm * PACK_ROWS, 128), lambda i: (i, 0))],
        out_shape=[jax.ShapeDtypeStruct((M, Dm), F32), jax.ShapeDtypeStruct((M, Dm), BF16),
                   jax.ShapeDtypeStruct((M * PACK_ROWS, 128), jnp.uint32)],
        compiler_params=_cp("arbitrary"),
        name="residual_ln",
    )(h, add, g.reshape(1, Dm), b.reshape(1, Dm))


def _ret_tables(C):
    hh = np.arange(RET_HEADS, dtype=np.float64)
    lg = np.log(1.0 - 2.0 ** (-5.0 - hh))[:, None]
    pos = np.arange(C, dtype=np.float64)[None, :]
    vecs = np.stack([np.exp(lg * (pos + 1.0)),
                     np.exp(lg * (C - pos)),
                     np.exp(lg * (C - 1.0 - pos)),
                     np.exp(lg * pos),
                     np.exp(lg * C) * np.ones_like(pos)], axis=1)
    tab = np.broadcast_to(vecs[..., None], (RET_HEADS, 5, C, RET_HD))
    rel = np.abs(pos.T - pos)
    dsym = np.exp(lg[:, :, None] * rel[None])
    return jnp.asarray(tab, F32), jnp.asarray(dsym, F32)


def _ret_kernel(q_ref, k_ref, v_ref, g_ref, tab_ref, d_ref, o_ref, fst_ref, run_ref, *, nc, C):
    s = pl.program_id(1)
    fwd = s < nc
    c = jnp.where(fwd, s, 2 * nc - 1 - s)
    row = c * C + lax.broadcasted_iota(jnp.int32, (C, 1), 0)
    live = row >= DEAD
    heads = range(RET_HEADS)
    HD = RET_HD
    ks = [jnp.where(live, k_ref[0, :, h * HD:(h + 1) * HD] * (RET_HD ** -0.5), 0.0) for h in heads]
    vbs = [jnp.where(live, v_ref[0, :, h * HD:(h + 1) * HD], 0.0).astype(BF16) for h in heads]

    @pl.when(jnp.logical_or(s == 0, s == nc))
    def _():
        run_ref[...] = jnp.zeros_like(run_ref)

    @pl.when(fwd)
    def _():
        for h in heads:
            fst_ref[c, h] = run_ref[h].astype(BF16)
        upd = [_dot_tn((ks[h] * tab_ref[h, 2]).astype(BF16), vbs[h]) for h in heads]
        for h in heads:
            run_ref[h] = run_ref[h] * tab_ref[h, 4] + upd[h]

    @pl.when(jnp.logical_not(fwd))
    def _():
        qs = [q_ref[0, :, h * HD:(h + 1) * HD] for h in heads]
        sc = [(_dot_nt(qs[h].astype(BF16), ks[h].astype(BF16)) * d_ref[h]).astype(BF16) for h in heads]
        left = [_dot((qs[h] * tab_ref[h, 0]).astype(BF16), fst_ref[c, h]) for h in heads]
        right = [_dot((qs[h] * tab_ref[h, 1]).astype(BF16), run_ref[h].astype(BF16)) for h in heads]
        intra = [_dot(sc[h], vbs[h]) for h in heads]
        upd = [_dot_tn((ks[h] * tab_ref[h, 3]).astype(BF16), vbs[h]) for h in heads]
        for h in heads:
            run_ref[h] = run_ref[h] * tab_ref[h, 4] + upd[h]
            out = intra[h] + left[h] + right[h]
            mu = jnp.mean(out, axis=-1, keepdims=True)
            dlt = out - mu
            var = jnp.mean(dlt * dlt, axis=-1, keepdims=True)
            normed = dlt * lax.rsqrt(var + NORM_EPS)
            g = g_ref[0, :, h * HD:(h + 1) * HD]
            o_ref[0, :, h * HD:(h + 1) * HD] = (g * jax.nn.sigmoid(g) * normed).astype(o_ref.dtype)


def _retention(proj):
    B, Lp, _ = proj.shape
    C = RET_CHUNK
    nc = Lp // C
    tab, dsym = _ret_tables(C)

    def cidx(s):
        return jnp.where(s < nc, s, 2 * nc - 1 - s)

    def cidx_out(s):
        return jnp.where(s < nc, nc - 1, 2 * nc - 1 - s)

    return pl.pallas_call(
        functools.partial(_ret_kernel, nc=nc, C=C),
        grid=(B, 2 * nc),
        in_specs=[
            pl.BlockSpec((1, C, RET_W), lambda b, s: (b, cidx_out(s), 0)),
            pl.BlockSpec((1, C, RET_W), lambda b, s: (b, cidx(s), 1)),
            pl.BlockSpec((1, C, RET_W), lambda b, s: (b, cidx(s), 2)),
            pl.BlockSpec((1, C, RET_W), lambda b, s: (b, cidx_out(s), 3)),
            pl.BlockSpec((RET_HEADS, 5, C, RET_HD), lambda b, s: (0, 0, 0, 0)),
            pl.BlockSpec((RET_HEADS, C, C), lambda b, s: (0, 0, 0)),
        ],
        out_specs=pl.BlockSpec((1, C, RET_W), lambda b, s: (b, cidx_out(s), 0)),
        out_shape=jax.ShapeDtypeStruct((B, Lp, RET_W), BF16),
        scratch_shapes=[pltpu.VMEM((nc, RET_HEADS, RET_HD, RET_HD), BF16),
                        pltpu.VMEM((RET_HEADS, RET_HD, RET_HD), F32)],
        compiler_params=_cp("arbitrary", "arbitrary"),
        name="retention",
    )(proj, proj, proj, proj, tab, dsym)


def _att_kernel(sink_ref, slope_ref, q_ref, km_ref, kp_ref, kc_ref, kn_ref,
                vm_ref, vp_ref, vc_ref, vn_ref, o_ref, *, nb):
    qb = pl.program_id(1)
    T = ATT_BLOCK
    HD = ATT_HD
    ii = lax.broadcasted_iota(jnp.int32, (T, T), 0)
    jj = lax.broadcasted_iota(jnp.int32, (T, T), 1)
    meta_ok = jj >= DEAD
    pieces = []
    for off, k_ref, v_ref, ok in ((-T, kp_ref, vp_ref, qb >= 2),
                                  (0, kc_ref, vc_ref, qb >= 1),
                                  (T, kn_ref, vn_ref, qb + 1 <= nb)):
        dist = jnp.abs(jj + off - ii)
        pieces.append((dist.astype(F32), dist <= WINDOW, k_ref, v_ref, ok))
    kv_heads = range(ATT_KV_HEADS)
    keys = [[km_ref[0, :, j * HD:(j + 1) * HD].astype(BF16)]
            + [p[2][0, :, j * HD:(j + 1) * HD].astype(BF16) for p in pieces] for j in kv_heads]
    vals = [[vm_ref[0, :, j * HD:(j + 1) * HD].astype(BF16)]
            + [p[3][0, :, j * HD:(j + 1) * HD].astype(BF16) for p in pieces] for j in kv_heads]
    groups = range(ATT_HEADS)
    qs = [(q_ref[0, :, g * HD:(g + 1) * HD] * (ATT_HD ** -0.5)).astype(BF16) for g in groups]
    raw = [[_dot_nt(qs[g], kk) for kk in keys[g // ATT_GROUP]] for g in groups]
    es, dens = [], []
    for g in groups:
        slope = slope_ref[g // ATT_GROUP, g % ATT_GROUP]
        sink = sink_ref[g // ATT_GROUP, g % ATT_GROUP]
        s_list = [jnp.where(meta_ok, raw[g][0], NEG)]
        for (dist, inwin, _, _, ok), r in zip(pieces, raw[g][1:]):
            s_list.append(jnp.where(ok, jnp.where(inwin, r - slope * dist, NEG), NEG))
        m = jnp.full((T, 1), sink, F32)
        for sb in s_list:
            m = jnp.maximum(m, jnp.max(sb, axis=-1, keepdims=True))
        e_list = [jnp.exp(sb - m) for sb in s_list]
        den = jnp.exp(sink - m)
        for e in e_list:
            den = den + jnp.sum(e, axis=-1, keepdims=True)
        es.append([e.astype(BF16) for e in e_list])
        dens.append(den)
    pv = [[_dot(e, vb) for e, vb in zip(es[g], vals[g // ATT_GROUP])] for g in groups]
    for g in groups:
        acc = pv[g][0] + pv[g][1] + pv[g][2] + pv[g][3]
        o_ref[0, :, g * HD:(g + 1) * HD] = (acc / dens[g]).astype(o_ref.dtype)


def _window_attention(proj, sink):
    B, Lp, _ = proj.shape
    nb = Lp // ATT_BLOCK - 1
    qw = ATT_HEADS * ATT_HD
    kw = ATT_KV_HEADS * ATT_HD
    q0 = 4 * RET_W // qw
    k0 = (4 * RET_W + qw) // kw
    v0 = k0 + 1
    slopes = np.asarray(2.0 ** (-8.0 * (np.arange(ATT_HEADS) + 1.0) / ATT_HEADS), np.float32)
    smem = pl.BlockSpec(memory_space=pltpu.SMEM)

    def kv(c0, f):
        return pl.BlockSpec((1, ATT_BLOCK, kw), lambda b, t: (b, f(t), c0))

    rows = [lambda t: 0, lambda t: jnp.maximum(t - 1, 0), lambda t: t, lambda t: jnp.minimum(t + 1, nb)]
    return pl.pallas_call(
        functools.partial(_att_kernel, nb=nb),
        grid=(B, nb + 1),
        in_specs=[smem, smem, pl.BlockSpec((1, ATT_BLOCK, qw), lambda b, t: (b, t, q0))]
        + [kv(k0, f) for f in rows] + [kv(v0, f) for f in rows],
        out_specs=pl.BlockSpec((1, ATT_BLOCK, qw), lambda b, t: (b, t, 0)),
        out_shape=jax.ShapeDtypeStruct((B, Lp, qw), BF16),
        compiler_params=_cp("arbitrary", "arbitrary"),
        name="window_attention",
    )(sink.astype(F32).reshape(ATT_KV_HEADS, ATT_GROUP), jnp.asarray(slopes).reshape(ATT_KV_HEADS, ATT_GROUP),
      *([proj] * 9))


CONV_HALO = 16


def _conv_shift_matrix(TT):
    half = CONV_WIDTH // 2
    t = np.arange(TT)[:, None]
    s = np.arange(TT + 2 * CONV_HALO)[None, :]
    blocks = [(s == t + CONV_HALO + tap - half) for tap in range(CONV_WIDTH) if tap != half]
    return jnp.asarray(np.concatenate(blocks, axis=0), BF16)


def _conv_kernel(xp_ref, x_ref, xn_ref, w_ref, sh_ref, o_ref, *, mode, TT, nt):
    t = pl.program_id(1)
    H = CONV_HALO
    rows = t * TT + lax.broadcasted_iota(jnp.int32, (TT, 1), 0)
    live = rows >= DEAD
    x = jnp.where(live, x_ref[0].astype(F32), 0.0)
    prow = t * TT - H + lax.broadcasted_iota(jnp.int32, (H, 1), 0)
    prev = jnp.where(prow >= DEAD, xp_ref[0].astype(F32), 0.0)
    nrow = (t + 1) * TT + lax.broadcasted_iota(jnp.int32, (H, 1), 0)
    nxt = jnp.where(jnp.logical_and(nrow >= DEAD, t < nt - 1), xn_ref[0].astype(F32), 0.0)
    staged = jnp.concatenate([prev.astype(BF16), x.astype(BF16), nxt.astype(BF16)], axis=0)
    shifted = _dot(sh_ref[...], staged)
    w = w_ref[...]
    half = CONV_WIDTH // 2
    acc = x * w[half:half + 1, :]
    blk = 0
    for tap in range(CONV_WIDTH):
        if tap == half:
            continue
        acc = acc + shifted[blk * TT:(blk + 1) * TT, :] * w[tap:tap + 1, :]
        blk += 1
    y = acc * jax.nn.sigmoid(acc)
    if mode in ("q", "k"):
        scale = GDN_HD ** -0.5 if mode == "q" else 1.0
        for a in range(y.shape[1] // GDN_HD):
            ya = y[:, a * GDN_HD:(a + 1) * GDN_HD]
            ya = ya * lax.rsqrt(jnp.sum(ya * ya, axis=-1, keepdims=True) + NORM_EPS)
            if mode == "q":
                ya = ya * scale
            o_ref[0, :, a * GDN_HD:(a + 1) * GDN_HD] = jnp.where(live, ya, 0.0).astype(o_ref.dtype)
    else:
        o_ref[0] = jnp.where(live, y, 0.0).astype(o_ref.dtype)


def _short_conv(qkv, conv_w, mode, col0, ncols):
    B, Lp, _ = qkv.shape
    H = CONV_HALO
    TT = _pick_tile(Lp, 256, mult=H)
    TC = 1024
    nt = Lp // TT
    c0 = col0 // TC
    shift = _conv_shift_matrix(TT)
    return pl.pallas_call(
        functools.partial(_conv_kernel, mode=mode, TT=TT, nt=nt),
        grid=(B, nt, ncols // TC),
        in_specs=[
            pl.BlockSpec((1, H, TC), lambda b, t, c: (b, jnp.maximum(t * (TT // H) - 1, 0), c0 + c)),
            pl.BlockSpec((1, TT, TC), lambda b, t, c: (b, t, c0 + c)),
            pl.BlockSpec((1, H, TC), lambda b, t, c: (b, jnp.minimum((t + 1) * (TT // H), Lp // H - 1), c0 + c)),
            pl.BlockSpec((CONV_WIDTH, TC), lambda b, t, c: (0, c0 + c)),
            pl.BlockSpec(shift.shape, lambda b, t, c: (0, 0)),
        ],
        out_specs=pl.BlockSpec((1, TT, TC), lambda b, t, c: (b, t, c)),
        out_shape=jax.ShapeDtypeStruct((B, Lp, ncols), BF16),
        compiler_params=_cp("arbitrary", "arbitrary", "arbitrary"),
        name="short_conv_" + mode,
    )(qkv, qkv, qkv, conv_w, shift)


def _gate_kernel(ba_ref, alog_ref, dtb_ref, o_ref, *, TT):
    t = pl.program_id(1)
    rows = t * TT + lax.broadcasted_iota(jnp.int32, (TT, 1), 0)
    lane = lax.broadcasted_iota(jnp.int32, (1, 4 * GDN_V_HEADS), 1)
    is_g = (lane // GDN_V_HEADS) % 2 == 1
    x = ba_ref[0]
    beta = jax.nn.sigmoid(x)
    xs = x + dtb_ref[...]
    softplus = jnp.maximum(xs, 0.0) + jnp.log(1.0 + jnp.exp(-jnp.abs(xs)))
    g = -jnp.exp(alog_ref[...]) * softplus
    o_ref[0] = jnp.where(rows >= DEAD, jnp.where(is_g, g, beta), 0.0)


def _gates(ba, a_log, dt_bias):
    B, Lp, W = ba.shape
    TT = _pick_tile(Lp, 1024)
    zeros = jnp.zeros((2, 1, GDN_V_HEADS), F32)
    alog = jnp.concatenate([zeros, a_log.astype(F32)[:, None, :]], axis=1).reshape(1, W)
    dtb = jnp.concatenate([zeros, dt_bias.astype(F32)[:, None, :]], axis=1).reshape(1, W)
    vec = pl.BlockSpec((1, W), lambda b, t: (0, 0))
    return pl.pallas_call(
        functools.partial(_gate_kernel, TT=TT),
        grid=(B, Lp // TT),
        in_specs=[pl.BlockSpec((1, TT, W), lambda b, t: (b, t, 0)), vec, vec],
        out_specs=pl.BlockSpec((1, TT, W), lambda b, t: (b, t, 0)),
        out_shape=jax.ShapeDtypeStruct((B, Lp, W), F32),
        compiler_params=_cp("arbitrary", "arbitrary"),
        name="gdn_gates",
    )(ba, alog, dtb)


def _gdn_kernel(*refs, rev, final):
    if final:
        q_ref, k_ref, v_ref, gc_ref, gr_ref, of_ref, z_ref, nw_ref, o_ref, s_ref = refs
    else:
        q_ref, k_ref, v_ref, gc_ref, gr_ref, o_ref, s_ref = refs
    C = GDN_CHUNK
    HG = GDN_HG

    @pl.when(pl.program_id(2) == 0)
    def _():
        s_ref[...] = jnp.zeros_like(s_ref)

    ii = lax.broadcasted_iota(jnp.int32, (C, C), 0)
    jj = lax.broadcasted_iota(jnp.int32, (C, C), 1)
    incl = (jj >= ii) if rev else (jj <= ii)
    strict = (jj > ii) if rev else (jj < ii)
    tri_col = jnp.where(incl, 1.0, 0.0).astype(BF16)
    tri_row = jnp.where((ii >= jj) if rev else (ii <= jj), 1.0, 0.0).astype(BF16)
    gcol = gc_ref[0, 0]
    grow = gr_ref[0, 0, 0]
    gc_col = sum(_dot(tri_col, p) for p in _split3(gcol))
    gc_row = sum(_dot(p, tri_row) for p in _split3(grow))
    last = 0 if rev else C - 1
    d0 = 2 * HG if rev else 0
    HD = GDN_HD
    heads = range(HG)
    pairs = range(HG // 2)
    qs = [q_ref[0, :, p * HD:(p + 1) * HD] for p in pairs]
    ks = [k_ref[0, :, p * HD:(p + 1) * HD] for p in pairs]
    kfs = [k.astype(F32) for k in ks]
    kq = [_dot_nt(jnp.concatenate([ks[p], qs[p]], axis=0), ks[p]) for p in pairs]
    gram = [r[:C] for r in kq]
    qk = [r[C:] for r in kq]
    beta = [gcol[:, d0 + h:d0 + h + 1] for h in heads]
    gcc = [gc_col[:, d0 + HG + h:d0 + HG + h + 1] for h in heads]
    gcr = [gc_row[d0 + HG + h:d0 + HG + h + 1, :] for h in heads]
    gtot = [g[last:last + 1, :] for g in gcc]
    decay = [jnp.where(incl, jnp.exp(jnp.where(incl, gcc[h] - gcr[h], 0.0)), 0.0) for h in heads]
    eg = [jnp.exp(g) for g in gcc]
    ms = [-jnp.where(strict, beta[h] * gram[h // 2] * decay[h], 0.0) for h in heads]
    pbs = [m.astype(BF16) for m in ms]
    ps = [_dot(pb, pb) for pb in pbs]
    pbs = [p.astype(BF16) for p in ps]
    for _ in range(int(math.log2(C)) - 2):
        res = [_dot(jnp.concatenate([ms[h].astype(BF16), pbs[h]], axis=0), pbs[h]) for h in heads]
        ms = [ms[h] + ps[h] + res[h][:C] for h in heads]
        ps = [r[C:] for r in res]
        pbs = [p.astype(BF16) for p in ps]
    ms = [ms[h] + ps[h] + _dot(ms[h].astype(BF16), pbs[h]) for h in heads]
    rhs = [jnp.concatenate([v_ref[0, :, h * HD:(h + 1) * HD].astype(F32) * beta[h],
                            kfs[h // 2] * (beta[h] * eg[h])], axis=1) for h in heads]
    sol = [rhs[h] + _dot(ms[h].astype(BF16), rhs[h].astype(BF16)) for h in heads]
    st = [s_ref[h] for h in heads]
    stb = [s.astype(BF16) for s in st]
    lhs = [jnp.concatenate([sol[h][:, HD:].astype(BF16), (qs[h // 2].astype(F32) * eg[h]).astype(BF16)], axis=0)
           for h in heads]
    ws = [_dot(lhs[h], stb[h]) for h in heads]
    vnb = [(sol[h][:, :HD] - ws[h][:C]).astype(BF16) for h in heads]
    outs = [ws[h][C:] + _dot((qk[h // 2] * decay[h]).astype(BF16), vnb[h]) for h in heads]
    upd = [_dot_tn((kfs[h // 2] * jnp.exp(gtot[h] - gcc[h])).astype(BF16), vnb[h]) for h in heads]
    for h in heads:
        s_ref[h] = st[h] * jnp.exp(gtot[h]) + upd[h]
        sl = slice(h * HD, (h + 1) * HD)
        if final:
            o = outs[h] + of_ref[0, :, sl]
            z = z_ref[0, :, sl]
            o = o * lax.rsqrt(jnp.mean(o * o, axis=-1, keepdims=True) + NORM_EPS) * nw_ref[...]
            o_ref[0, :, sl] = (o * (z * jax.nn.sigmoid(z))).astype(o_ref.dtype)
        else:
            o_ref[0, :, sl] = outs[h]


def _gdn_dir(q, k, v, gcol, grow, rev, o_fwd=None, z=None, norm_w=None):
    B, Lp, _ = v.shape
    C = GDN_CHUNK
    nc = Lp // C
    ng = GDN_V_HEADS // GDN_HG
    qw = GDN_HG // 2 * GDN_HD
    vw = GDN_HG * GDN_HD
    final = o_fwd is not None

    def cc(c):
        return nc - 1 - c if rev else c

    in_specs = [
        pl.BlockSpec((1, C, qw), lambda b, g, c: (b, cc(c), g)),
        pl.BlockSpec((1, C, qw), lambda b, g, c: (b, cc(c), g)),
        pl.BlockSpec((1, C, vw), lambda b, g, c: (b, cc(c), g)),
        pl.BlockSpec((1, 1, C, 4 * GDN_HG), lambda b, g, c: (b, g, cc(c), 0)),
        pl.BlockSpec((1, 1, 1, 4 * GDN_HG, C), lambda b, g, c: (b, g, cc(c), 0, 0)),
    ]
    args = [q, k, v, gcol, grow]
    if final:
        in_specs += [pl.BlockSpec((1, C, vw), lambda b, g, c: (b, cc(c), g)),
                     pl.BlockSpec((1, C, vw), lambda b, g, c: (b, cc(c), g)),
                     pl.BlockSpec((1, GDN_HD), lambda b, g, c: (0, 0))]
        args += [o_fwd, z, norm_w.astype(F32).reshape(1, GDN_HD)]
    return pl.pallas_call(
        functools.partial(_gdn_kernel, rev=rev, final=final),
        grid=(B, ng, nc),
        in_specs=in_specs,
        out_specs=pl.BlockSpec((1, C, vw), lambda b, g, c: (b, cc(c), g)),
        out_shape=jax.ShapeDtypeStruct((B, Lp, GDN_V_W), BF16 if final else F32),
        scratch_shapes=[pltpu.VMEM((GDN_HG, GDN_HD, GDN_HD), F32)],
        compiler_params=_cp("arbitrary", "arbitrary", "arbitrary"),
        name="gdn_bwd" if rev else "gdn_fwd",
    )(*args)


def _gdn_mixer(hb, B, Lp, i, w_in, conv_w, a_log, dt_bias, norm_w, w_out):
    qkv = _matmul([hb], w_in, i, 0, GDN_CONV_CH, 1024, out_dtype=BF16).reshape(B, Lp, GDN_CONV_CH)
    z = _matmul([hb], w_in, i, GDN_CONV_CH, GDN_V_W, 1024).reshape(B, Lp, GDN_V_W)
    ba = _matmul([hb], w_in, i, GDN_CONV_CH + GDN_V_W, 4 * GDN_V_HEADS, 128).reshape(B, Lp, 4 * GDN_V_HEADS)
    q = _short_conv(qkv, conv_w, "q", 0, GDN_K_W)
    k = _short_conv(qkv, conv_w, "k", GDN_K_W, GDN_K_W)
    v = _short_conv(qkv, conv_w, "v", 2 * GDN_K_W, GDN_V_W)
    gb = _gates(ba, a_log, dt_bias)
    ng = GDN_V_HEADS // GDN_HG
    nc = Lp // GDN_CHUNK
    gcol = gb.reshape(B, Lp, 4, ng, GDN_HG).transpose(0, 3, 1, 2, 4).reshape(B, ng, Lp, 4 * GDN_HG)
    grow = gcol.reshape(B, ng, nc, GDN_CHUNK, 4 * GDN_HG).transpose(0, 1, 2, 4, 3)
    o_f = _gdn_dir(q, k, v, gcol, grow, rev=False)
    o = _gdn_dir(q, k, v, gcol, grow, rev=True, o_fwd=o_f, z=z, norm_w=norm_w)
    return _matmul([o.reshape(B * Lp, GDN_V_W)], w_out, i, 0, D_MODEL, 512)


def _even_mixer(hb, B, Lp, i, w_in, w_out, sink):
    proj = _matmul([hb], w_in, i, 0, EVEN_IN, 512).reshape(B, Lp, EVEN_IN)
    ret = _retention(proj).reshape(B * Lp, RET_W)
    att = _window_attention(proj, sink).reshape(B * Lp, ATT_HEADS * ATT_HD)
    return _matmul([ret, att], w_out, i, 0, D_MODEL, 1024)


def _router_kernel(x_ref, wt_ref, bias_ref, eidx_ref, wts_ref, rank_ref, cnt_ref, carry_ref, *, TM):
    E = N_EXPERTS

    @pl.when(pl.program_id(0) == 0)
    def _():
        carry_ref[...] = jnp.zeros_like(carry_ref)

    xh, xm, _ = _split3(x_ref[...])
    wh, wm, _ = _split3(wt_ref[...])
    lead = _dot_nt(jnp.concatenate([wh, wm], axis=0), xh)
    logits = lead[:E] + (lead[E:] + _dot_nt(wh, xm))
    scores = jax.nn.sigmoid(logits)
    choice = scores + bias_ref[...]
    ninf = -jnp.inf
    io8 = lax.broadcasted_iota(jnp.int32, (GROUP_SIZE, TM), 0)
    gs_rows = []
    for g in range(N_GROUPS):
        cg = choice[g * GROUP_SIZE:(g + 1) * GROUP_SIZE, :]
        m1 = jnp.max(cg, axis=0, keepdims=True)
        i1 = jnp.min(jnp.where(cg == m1, io8, GROUP_SIZE), axis=0, keepdims=True)
        m2 = jnp.max(jnp.where(io8 == i1, ninf, cg), axis=0, keepdims=True)
        gs_rows.append(m1 + m2)
    gs = jnp.concatenate(gs_rows, axis=0)
    gsel = jnp.zeros((N_GROUPS, TM), jnp.int32)
    for _ in range(TOPK_GROUPS):
        m = jnp.max(gs, axis=0, keepdims=True)
        idx = jnp.min(jnp.where(gs == m, io8, N_GROUPS), axis=0, keepdims=True)
        hit = io8 == idx
        gsel = jnp.where(hit, 1, gsel)
        gs = jnp.where(hit, ninf, gs)
    masked = jnp.concatenate(
        [jnp.where(gsel[g:g + 1, :] > 0, choice[g * GROUP_SIZE:(g + 1) * GROUP_SIZE, :], ninf)
         for g in range(N_GROUPS)], axis=0)
    ioe = lax.broadcasted_iota(jnp.int32, (E, TM), 0)
    sel = jnp.zeros((E, TM), F32)
    idx_rows, w_rows = [], []
    for _ in range(TOP_K):
        m = jnp.max(masked, axis=0, keepdims=True)
        idx = jnp.min(jnp.where(masked == m, ioe, E), axis=0, keepdims=True)
        hit = ioe == idx
        idx_rows.append(idx)
        w_rows.append(jnp.sum(jnp.where(hit, scores, 0.0), axis=0, keepdims=True))
        sel = jnp.where(hit, 1.0, sel)
        masked = jnp.where(hit, ninf, masked)
    wsum = w_rows[0]
    for w in w_rows[1:]:
        wsum = wsum + w
    ti = lax.broadcasted_iota(jnp.int32, (TM, TM), 0)
    tj = lax.broadcasted_iota(jnp.int32, (TM, TM), 1)
    before = jnp.where(ti < tj, 1.0, 0.0).astype(BF16)
    rank = _dot(sel.astype(BF16), before) + carry_ref[:, 0:1]
    rank_rows = [jnp.sum(jnp.where(ioe == idx, rank, 0.0), axis=0, keepdims=True) for idx in idx_rows]
    eidx_ref[...] = jnp.concatenate(idx_rows, axis=0)
    wts_ref[...] = jnp.concatenate([w / wsum * ROUTE_SCALE for w in w_rows], axis=0)
    rank_ref[...] = jnp.concatenate(rank_rows, axis=0).astype(jnp.int32)
    carry_ref[...] = carry_ref[...] + jnp.sum(sel, axis=1, keepdims=True)
    cnt_ref[...] = carry_ref[...]


def _router(h, w_router, router_bias):
    T, Dm = h.shape
    TM = _pick_tile(T, 256)
    tok = pl.BlockSpec((TOP_K, TM), lambda i: (0, i))
    return pl.pallas_call(
        functools.partial(_router_kernel, TM=TM),
        grid=(T // TM,),
        in_specs=[pl.BlockSpec((TM, Dm), lambda i: (i, 0)),
                  pl.BlockSpec((N_EXPERTS, Dm), lambda i: (0, 0)),
                  pl.BlockSpec((N_EXPERTS, 1), lambda i: (0, 0))],
        out_specs=[tok, tok, tok, pl.BlockSpec((N_EXPERTS, 128), lambda i: (0, 0))],
        out_shape=[jax.ShapeDtypeStruct((TOP_K, T), jnp.int32), jax.ShapeDtypeStruct((TOP_K, T), F32),
                   jax.ShapeDtypeStruct((TOP_K, T), jnp.int32), jax.ShapeDtypeStruct((N_EXPERTS, 128), F32)],
        scratch_shapes=[pltpu.VMEM((N_EXPERTS, 128), F32)],
        compiler_params=_cp("arbitrary"),
        name="moe_router",
    )(h, w_router.T, router_bias.astype(F32).reshape(N_EXPERTS, 1))


def _dispatch_kernel(dest_hbm, fill_hbm, hp_hbm, xs_hbm, dest_smem, fill_smem, zrow_ref, xbuf_ref,
                     sem, tsem, isem, fsem, zsem, *, TM, nt, nfill):
    i = pl.program_id(0)
    R = PACK_ROWS
    NI = TM * TOP_K

    def idx_copy(tile, region):
        return pltpu.make_async_copy(dest_hbm.at[tile], dest_smem.at[pl.ds(pl.multiple_of(region * NI, NI), NI)],
                                     isem.at[region])

    @pl.when(i == 0)
    def _():
        idx_copy(0, 0).start()

    @pl.when(i + 1 < nt)
    def _():
        idx_copy(i + 1, (i + 1) % 2).start()

    def slot_rows(d):
        return xs_hbm.at[pl.ds(pl.multiple_of(d * R, R), R)]

    @pl.when(i == 0)
    def _():
        zrow_ref[...] = jnp.zeros_like(zrow_ref)
        fc = pltpu.make_async_copy(fill_hbm, fill_smem, fsem)
        fc.start()
        fc.wait()

        def per_expert(e, carry):
            first = fill_smem[e]

            def body(r, c):
                pltpu.make_async_copy(zrow_ref, slot_rows(first + r), zsem).start()
                return c

            return lax.fori_loop(0, fill_smem[N_EXPERTS + e], body, carry)

        lax.fori_loop(0, N_EXPERTS, per_expert, 0)


    def tile_copy(tile, b):
        rows = pl.ds(pl.multiple_of(tile * (TM * R), TM * R), TM * R)
        return pltpu.make_async_copy(hp_hbm.at[rows], xbuf_ref.at[b], tsem.at[b])

    @pl.when(i == 0)
    def _():
        tile_copy(0, 0).start()
        if nt > 1:
            tile_copy(1, 1).start()

    idx_copy(i, i % 2).wait()
    ibase = (i % 2) * NI
    cur = i % 3
    tile_copy(i, cur).wait()

    def issue(grp, carry):
        t0 = grp * ISSUE_GROUP
        base = ibase + t0 * TOP_K
        slots = [[dest_smem[base + (u * TOP_K + k)] for k in range(TOP_K)] for u in range(ISSUE_GROUP)]
        for u in range(ISSUE_GROUP):
            src = xbuf_ref.at[cur, pl.ds(pl.multiple_of((t0 + u) * R, R), R)]
            for k in range(TOP_K):
                pltpu.make_async_copy(src, slot_rows(slots[u][k]), sem.at[cur]).start(priority=k % 2)
        return carry

    lax.fori_loop(0, TM // ISSUE_GROUP, issue, 0)

    def retire(b):
        for k in range(TOP_K):
            pltpu.make_async_copy(xbuf_ref.at[b], xs_hbm.at[pl.ds(0, TM * R)], sem.at[b]).wait()

    @pl.when(i > 0)
    def _():
        retire((i + 2) % 3)

    @pl.when(i + 2 < nt)
    def _():
        tile_copy(i + 2, (i + 2) % 3).start()

    @pl.when(i == nt - 1)
    def _():
        retire(cur)

        def per_expert_wait(e, carry):
            n = fill_smem[N_EXPERTS + e]

            def wchunk(r, c):
                pltpu.make_async_copy(xbuf_ref.at[0, pl.ds(0, ZCH * R)], xs_hbm.at[pl.ds(0, ZCH * R)], zsem).wait()
                return c

            def wone(r, c):
                pltpu.make_async_copy(zrow_ref, slot_rows(0), zsem).wait()
                return c

            carry = lax.fori_loop(0, n // ZCH, wchunk, carry)
            return lax.fori_loop(0, n % ZCH, wone, carry)

        lax.fori_loop(0, N_EXPERTS, per_expert_wait, 0)


def _dispatch(hp, dest_tiles, fill, rows):
    nt = dest_tiles.shape[0]
    TM = dest_tiles.shape[1] // TOP_K
    nfill = fill.shape[0]
    return pl.pallas_call(
        functools.partial(_dispatch_kernel, TM=TM, nt=nt, nfill=nfill),
        grid=(nt,),
        in_specs=[pl.BlockSpec(memory_space=pl.ANY), pl.BlockSpec(memory_space=pl.ANY),
                  pl.BlockSpec(memory_space=pl.ANY)],
        out_specs=pl.BlockSpec(memory_space=pl.ANY),
        out_shape=jax.ShapeDtypeStruct((rows * PACK_ROWS, 128), jnp.uint32),
        scratch_shapes=[pltpu.SMEM((2 * TOP_K * TM,), jnp.int32), pltpu.SMEM((nfill,), jnp.int32),
                        pltpu.VMEM((PACK_ROWS, 128), jnp.uint32),
                        pltpu.VMEM((3, TM * PACK_ROWS, 128), jnp.uint32),
                        pltpu.SemaphoreType.DMA((3,)), pltpu.SemaphoreType.DMA((3,)),
                        pltpu.SemaphoreType.DMA((2,)), pltpu.SemaphoreType.DMA, pltpu.SemaphoreType.DMA],
        compiler_params=_cp("arbitrary"),
        name="moe_dispatch",
    )(dest_tiles, fill, hp)


def _expert_kernel(be_ref, na_ref, x_ref, wg_ref, wu_ref, wd_ref, o_ref, wgb_ref, wub_ref, wdb_ref):
    i = pl.program_id(0)
    active = i < na_ref[0]
    new_expert = jnp.logical_or(i == 0, be_ref[i] != be_ref[jnp.maximum(i - 1, 0)])

    @pl.when(jnp.logical_and(active, new_expert))
    def _():
        wgb_ref[...] = wg_ref[0, 0].astype(BF16)
        wub_ref[...] = wu_ref[0, 0].astype(BF16)
        wdb_ref[...] = wd_ref[0, 0].astype(BF16)

    @pl.when(active)
    def _():
        x = jnp.concatenate([c.astype(BF16) for c in _unpack_rows(x_ref, MOE_BLOCK)], axis=1)
        a = _dot(x, wgb_ref[...])
        b = _dot(x, wub_ref[...])
        hmid = (a * jax.nn.sigmoid(a) * b).astype(BF16)
        _pack_rows(o_ref, _dot(hmid, wdb_ref[...]))


def _experts(xs, blk_e, nact, w_gate, w_up, w_down, layer):
    Dm = D_MODEL
    rows = xs.shape[0] // PACK_ROWS
    nblk = rows // MOE_BLOCK
    pblock = MOE_BLOCK * PACK_ROWS

    def row(i, be, na):
        return (jnp.minimum(i, na[0] - 1), 0)

    def wsel(i, be, na):
        return (layer, be[jnp.minimum(i, na[0] - 1)], 0, 0)

    return pl.pallas_call(
        _expert_kernel,
        grid_spec=pltpu.PrefetchScalarGridSpec(
            num_scalar_prefetch=2,
            grid=(nblk,),
            in_specs=[pl.BlockSpec((pblock, 128), row),
                      pl.BlockSpec((1, 1, Dm, D_EXPERT), wsel),
                      pl.BlockSpec((1, 1, Dm, D_EXPERT), wsel),
                      pl.BlockSpec((1, 1, D_EXPERT, Dm), wsel)],
            out_specs=pl.BlockSpec((pblock, 128), row),
            scratch_shapes=[pltpu.VMEM((Dm, D_EXPERT), BF16), pltpu.VMEM((Dm, D_EXPERT), BF16),
                            pltpu.VMEM((D_EXPERT, Dm), BF16)],
        ),
        out_shape=jax.ShapeDtypeStruct((rows * PACK_ROWS, 128), jnp.uint32),
        compiler_params=_cp("arbitrary"),
        name="moe_experts",
    )(blk_e, nact, xs, w_gate, w_up, w_down)


def _combine_kernel(dest_hbm, yb_hbm, w_ref, h_ref, sh_ref, g_ref, b_ref, o_ref, ob_ref,
                    dest_smem, buf_ref, routed_ref, sem, isem, *, TM, nt):
    i = pl.program_id(0)
    R = PACK_ROWS
    G = 8
    half = D_MODEL // 2

    NI = TM * TOP_K

    def idx_copy(tile, region):
        return pltpu.make_async_copy(dest_hbm.at[tile], dest_smem.at[pl.ds(pl.multiple_of(region * NI, NI), NI)],
                                     isem.at[region])

    @pl.when(i == 0)
    def _():
        idx_copy(0, 0).start()
        idx_copy(0, 0).wait()
        idx_copy(min(1, nt - 1), 1).start()

        def first(grp, carry):
            t0 = grp * G
            slots = [[dest_smem[t0 * TOP_K + (u * TOP_K + k)] for k in range(TOP_K)] for u in range(G)]
            for u in range(G):
                for k in range(TOP_K):
                    src = yb_hbm.at[pl.ds(pl.multiple_of(slots[u][k] * R, R), R)]
                    dst = buf_ref.at[0, k, pl.ds(pl.multiple_of((t0 + u) * R, R), R)]
                    pltpu.make_async_copy(src, dst, sem.at[0]).start(priority=k % 2)
            return carry

        lax.fori_loop(0, TM // G, first, 0)

    slot = i % 2
    nreg = (i + 1) % 3
    idx_copy(jnp.minimum(i + 1, nt - 1), nreg).wait()

    @pl.when(i + 1 < nt)
    def _():
        idx_copy(jnp.minimum(i + 2, nt - 1), (i + 2) % 3).start()
    nslot = (i + 1) % 2

    def retire(b):
        for k in range(TOP_K):
            pltpu.make_async_copy(yb_hbm.at[pl.ds(0, TM * R)], buf_ref.at[b, k], sem.at[b]).wait()

    retire(slot)

    def step(grp, carry):
        t0 = grp * G
        base = nreg * NI + t0 * TOP_K
        slots = [[dest_smem[base + (u * TOP_K + k)] for k in range(TOP_K)] for u in range(G)]
        r0 = pl.multiple_of(grp * G, G)
        w = w_ref[pl.ds(r0, G), :]
        lo = [None] * R
        hi = [None] * R
        for k in range(TOP_K):
            wk = w[:, k:k + 1]
            words = [buf_ref[slot, k, pl.ds(grp * (G * R) + j, G, stride=R), :] for j in range(R)]
            u = k
            for kk in range(TOP_K):
                src = yb_hbm.at[pl.ds(pl.multiple_of(slots[u][kk] * R, R), R)]
                dst = buf_ref.at[nslot, kk, pl.ds(pl.multiple_of((t0 + u) * R, R), R)]
                pltpu.make_async_copy(src, dst, sem.at[nslot]).start(priority=kk % 2)
            for j in range(R):
                a = pltpu.bitcast(words[j] << 16, F32) * wk
                b = pltpu.bitcast(words[j] & jnp.uint32(HI_MASK), F32) * wk
                lo[j] = a if lo[j] is None else lo[j] + a
                hi[j] = b if hi[j] is None else hi[j] + b
        for j in range(R):
            routed_ref[pl.ds(r0, G), 128 * j:128 * (j + 1)] = lo[j]
            routed_ref[pl.ds(r0, G), half + 128 * j:half + 128 * (j + 1)] = hi[j]
        return carry

    lax.fori_loop(0, TM // G, step, 0)

    @pl.when(i == nt - 1)
    def _():
        retire(nslot)

    out = _layer_norm_rows(DN_ALPHA * h_ref[...] + (routed_ref[...] + sh_ref[...]), g_ref[...], b_ref[...])
    o_ref[...] = out
    ob_ref[...] = out.astype(BF16)


def _combine_ln(yb, dest_tiles, wts_tok, h, shared, g, b, drop_front=None):
    nt = dest_tiles.shape[0]
    TM = dest_tiles.shape[1] // TOP_K
    T, Dm = h.shape
    row = pl.BlockSpec((TM, Dm), lambda i: (i, 0))
    vec = pl.BlockSpec((1, Dm), lambda i: (0, 0))
    hbm = pl.BlockSpec(memory_space=pl.ANY)
    out_row, out_rows = row, T
    if drop_front is not None:
        assert TM == FRONT
        npb = drop_front
        out_row = pl.BlockSpec((TM, Dm), lambda i: ((i // npb) * (npb - 1) + jnp.maximum(i % npb - 1, 0), 0))
        out_rows = T - (T // (npb * TM)) * FRONT
    return pl.pallas_call(
        functools.partial(_combine_kernel, TM=TM, nt=nt),
        grid=(nt,),
        in_specs=[hbm, hbm, pl.BlockSpec((TM, TOP_K), lambda i: (i, 0)), row, row, vec, vec],
        out_specs=[out_row, row],
        out_shape=[jax.ShapeDtypeStruct((out_rows, Dm), F32), jax.ShapeDtypeStruct((T, Dm), BF16)],
        scratch_shapes=[pltpu.SMEM((3 * TOP_K * TM,), jnp.int32),
                        pltpu.VMEM((2, TOP_K, TM * PACK_ROWS, 128), jnp.uint32),
                        pltpu.VMEM((TM, Dm), F32),
                        pltpu.SemaphoreType.DMA((2,)), pltpu.SemaphoreType.DMA((3,))],
        compiler_params=_cp("arbitrary"),
        name="moe_combine",
    )(dest_tiles, yb, wts_tok, h, shared, g.reshape(1, Dm), b.reshape(1, Dm))


def _shared_kernel(x_ref, wg_ref, wu_ref, wd_ref, o_ref, wgb_ref, wub_ref, wdb_ref):
    @pl.when(pl.program_id(0) == 0)
    def _():
        wgb_ref[...] = wg_ref[...].astype(BF16)
        wub_ref[...] = wu_ref[...].astype(BF16)
        wdb_ref[...] = wd_ref[...].astype(BF16)

    x = x_ref[...]
    a = _dot(x, wgb_ref[...])
    b = _dot(x, wub_ref[...])
    o_ref[...] = _dot((a * jax.nn.sigmoid(a) * b).astype(BF16), wdb_ref[...])


def _shared_expert(hb, wg, wu, wd):
    T, Dm = hb.shape
    Ds = wg.shape[1]
    TM = _pick_tile(T, DENSE_TM)
    full = lambda r, c: pl.BlockSpec((r, c), lambda i: (0, 0))
    return pl.pallas_call(
        _shared_kernel,
        grid=(T // TM,),
        in_specs=[pl.BlockSpec((TM, Dm), lambda i: (i, 0)), full(Dm, Ds), full(Dm, Ds), full(Ds, Dm)],
        out_specs=pl.BlockSpec((TM, Dm), lambda i: (i, 0)),
        out_shape=jax.ShapeDtypeStruct((T, Dm), F32),
        scratch_shapes=[pltpu.VMEM((Dm, Ds), BF16), pltpu.VMEM((Dm, Ds), BF16), pltpu.VMEM((Ds, Dm), BF16)],
        compiler_params=_cp("arbitrary"),
        name="moe_shared",
    )(hb, wg, wu, wd)


def _moe_ln(h, hb, hp, w_router, router_bias, w_gate, w_up, w_down, layer, ws_gate, ws_up, ws_down, g, b,
            drop_front=None):
    T, Dm = h.shape
    eidx, wts, rank, cnt = _router(h, w_router, router_bias)
    counts = cnt[:, 0].astype(jnp.int32)
    pcounts = (counts + MOE_BLOCK - 1) // MOE_BLOCK * MOE_BLOCK
    pends = jnp.cumsum(pcounts)
    pstarts = pends - pcounts
    nblk = -(-T * TOP_K // MOE_BLOCK) + N_EXPERTS
    rows = nblk * MOE_BLOCK
    eid = jnp.arange(N_EXPERTS, dtype=jnp.int32)
    dest = jnp.sum(jnp.where(eidx[..., None] == eid, pstarts.astype(jnp.int32), 0), axis=-1) + rank
    TM = _pick_tile(T, 128)
    dest_tiles = dest.T.reshape(T // TM, TM * TOP_K)
    blk_start = jnp.arange(nblk, dtype=jnp.int32) * MOE_BLOCK
    blk_e = jnp.minimum(jnp.sum((pends[None, :] <= blk_start[:, None]).astype(jnp.int32), axis=1),
                        N_EXPERTS - 1)
    nact = (pends[-1:] // MOE_BLOCK).astype(jnp.int32)
    fill = jnp.concatenate([pstarts + counts, pcounts - counts]).astype(jnp.int32)
    xs = _dispatch(hp, dest_tiles, fill, rows)
    yb = _experts(xs, blk_e, nact, w_gate, w_up, w_down, layer)
    shared = _shared_expert(hb, ws_gate, ws_up, ws_down)
    return _combine_ln(yb, dest_tiles, wts.T, h, shared, g, b, drop_front)


def kernel(x, meta_tokens, ev_w_in, ev_w_out, ev_sink, od_w_in, od_conv_w, od_a_log, od_dt_bias, od_norm_w,
           od_w_out, ln_g, ln_b, w_router, router_bias, w_gate, w_up, w_down, ws_gate, ws_up, ws_down):
    B, S, Dm = x.shape
    Lp = S + FRONT
    meta = jnp.broadcast_to(meta_tokens.astype(x.dtype)[None], (B, N_META, Dm))
    h = jnp.concatenate([jnp.zeros((B, DEAD, Dm), x.dtype), meta, x], axis=1).reshape(B * Lp, Dm)
    hb = h.astype(BF16)
    for layer in range(DEPTH):
        i = layer // 2
        if layer % 2 == 0:
            mix = _even_mixer(hb, B, Lp, i, ev_w_in, ev_w_out, ev_sink[i])
        else:
            mix = _gdn_mixer(hb, B, Lp, i, od_w_in, od_conv_w[i], od_a_log[i], od_dt_bias[i],
                             od_norm_w[i], od_w_out)
        h, hb, hp = _residual_ln(h, mix, ln_g[layer, 0], ln_b[layer, 0])
        last = layer == DEPTH - 1
        h, hb = _moe_ln(h, hb, hp, w_router[layer], router_bias[layer], w_gate, w_up, w_down, layer,
                        ws_gate[layer], ws_up[layer], ws_down[layer], ln_g[layer, 1], ln_b[layer, 1],
                        drop_front=Lp // FRONT if last else None)
    return h.reshape(B, S, Dm)
```
